```python
import jax, jax.numpy as jnp
from jax import lax
import numpy as np

D_MODEL = 1024
BATCH = 8
SEQ = 8192
DEPTH = 1

D_MIX = D_MODEL
D_LRU = D_MIX // 2
LRU_HEADS = 8
LRU_HEAD_DIM = D_LRU // LRU_HEADS
CONV_WIDTH = 4
LRU_C = 8.0
D_GLA = D_MIX - D_LRU
GLA_HEADS = 4
GLA_DV = D_GLA // GLA_HEADS
GLA_DK = GLA_DV // 2
GLA_GATE_RANK = 16
GLA_GATE_NORM = 16.0
GLA_CHUNK = 64
N_GROUPS = 4
EXPERTS_PER_GROUP = 8
N_EXPERTS = N_GROUPS * EXPERTS_PER_GROUP
TOP_K = 2
D_EXPERT = D_MODEL // 2
MOE_BLOCK = 256
D_IN_PROJ = 2 * D_LRU + 2 * GLA_HEADS * GLA_DK + 2 * D_GLA + GLA_GATE_RANK
EPS = 1e-6

kernel_name = "hybrid_rglru_gla_hmoe_adaln"


def rms_norm(x, gain):
    x32 = x.astype(jnp.float32)
    y = x32 * lax.rsqrt(jnp.mean(x32 * x32, axis=-1, keepdims=True) + EPS)
    return (y * gain.astype(jnp.float32)).astype(x.dtype)


def causal_depthwise_conv(x, w, b):
    seq = x.shape[1]
    xp = jnp.pad(x, ((0, 0), (CONV_WIDTH - 1, 0), (0, 0)))
    return b + sum(w[k] * xp[:, k:k + seq] for k in range(CONV_WIDTH))


def rg_lru(x, w_r, b_r, w_i, b_i, lam):
    bsz, seq, width = x.shape
    xh = x.reshape(bsz, seq, LRU_HEADS, LRU_HEAD_DIM)
    r = jax.nn.sigmoid(jnp.einsum("bshi,hij->bshj", xh, w_r).reshape(bsz, seq, width) + b_r)
    gate_i = jax.nn.sigmoid(jnp.einsum("bshi,hij->bshj", xh, w_i).reshape(bsz, seq, width) + b_i)
    log_a = -LRU_C * r.astype(jnp.float32) * jax.nn.softplus(-lam.astype(jnp.float32))
    a = jnp.exp(log_a)
    u = jnp.sqrt(-jnp.expm1(2.0 * log_a)) * (gate_i * x).astype(jnp.float32)

    def step(h, au):
        a_t, u_t = au
        h = a_t * h + u_t
        return h, h

    _, hs = lax.scan(step, jnp.zeros((bsz, width), jnp.float32),
                     (a.swapaxes(0, 1), u.swapaxes(0, 1)))
    return hs.swapaxes(0, 1).astype(x.dtype)


def gla_chunked(q, k, v, log_alpha):
    bsz, seq, nh, dk = q.shape
    dv = v.shape[-1]
    n = seq // GLA_CHUNK
    f32 = jnp.float32
    q = q.astype(f32).reshape(bsz, n, GLA_CHUNK, nh, dk) * dk ** -0.5
    k = k.astype(f32).reshape(bsz, n, GLA_CHUNK, nh, dk)
    v32 = v.astype(f32).reshape(bsz, n, GLA_CHUNK, nh, dv)
    b = jnp.cumsum(log_alpha.astype(f32).reshape(bsz, n, GLA_CHUNK, nh, dk), axis=2)
    b_last = b[:, :, -1]
    q_dec = q * jnp.exp(b)
    scores = jnp.einsum("bnchd,bnshd->bnhcs", q_dec, k * jnp.exp(-b))
    causal = jnp.tril(jnp.ones((GLA_CHUNK, GLA_CHUNK), dtype=bool))
    scores = jnp.where(causal, scores, 0.0)
    o_intra = jnp.einsum("bnhcs,bnshe->bnche", scores, v32)
    chunk_kv = jnp.einsum("bnchd,bnche->bnhde", k * jnp.exp(b_last[:, :, None] - b), v32)

    def step(state, inp):
        decay, kv = inp
        return decay[..., None] * state + kv, state

    _, states = lax.scan(step, jnp.zeros((bsz, nh, dk, dv), f32),
                         (jnp.exp(b_last).swapaxes(0, 1), chunk_kv.swapaxes(0, 1)))
    o_inter = jnp.einsum("bnchd,bnhde->bnche", q_dec, states.swapaxes(0, 1))
    return (o_intra + o_inter).reshape(bsz, seq, nh, dv).astype(v.dtype)


def hierarchical_moe(h, w_coarse, b_coarse, w_fine, b_fine, w_gate, w_up, w_down):
    bsz, seq, d = h.shape
    f32 = jnp.float32
    n_tok = bsz * seq
    xf = h.reshape(n_tok, d)
    coarse_logits = (xf @ w_coarse).astype(f32) + b_coarse.astype(f32)
    coarse_p = jax.nn.softmax(coarse_logits, axis=-1)
    grp = jnp.argmax(coarse_logits, axis=-1).astype(jnp.int32)
    p_grp = jnp.take_along_axis(coarse_p, grp[:, None], axis=-1)
    fine_logits = ((xf @ w_fine).astype(f32) + b_fine.astype(f32)).reshape(
        n_tok, N_GROUPS, EXPERTS_PER_GROUP)
    fine_logits = jnp.take_along_axis(fine_logits, grp[:, None, None], axis=1)[:, 0]
    top_p, top_i = lax.top_k(jax.nn.softmax(fine_logits, axis=-1), TOP_K)
    weights = p_grp * top_p / jnp.sum(top_p, axis=-1, keepdims=True)
    experts = grp[:, None] * EXPERTS_PER_GROUP + top_i.astype(jnp.int32)

    n_assign = n_tok * TOP_K
    flat_e = experts.reshape(n_assign)
    flat_t = jnp.repeat(jnp.arange(n_tok, dtype=jnp.int32), TOP_K)
    flat_w = weights.reshape(n_assign)
    order = jnp.argsort(flat_e)
    se, st, sw = flat_e[order], flat_t[order], flat_w[order]
    counts = jnp.bincount(flat_e, length=N_EXPERTS)
    starts = jnp.cumsum(counts) - counts
    padded = (counts + MOE_BLOCK - 1) // MOE_BLOCK * MOE_BLOCK
    pends = jnp.cumsum(padded)
    dest = (pends - padded)[se] + jnp.arange(n_assign, dtype=jnp.int32) - starts[se]
    cap = (n_assign + MOE_BLOCK - 1) // MOE_BLOCK * MOE_BLOCK + N_EXPERTS * MOE_BLOCK
    n_blocks = cap // MOE_BLOCK
    buf_t = jnp.zeros((cap,), jnp.int32).at[dest].set(st)
    buf_w = jnp.zeros((cap,), f32).at[dest].set(sw)
    block_e = jnp.minimum(
        jnp.searchsorted(pends, jnp.arange(n_blocks, dtype=jnp.int32) * MOE_BLOCK, side="right"),
        N_EXPERTS - 1)

    def run_block(args):
        tok, e = args
        xb = xf[tok]
        hid = jax.nn.silu(xb @ w_gate[e]) * (xb @ w_up[e])
        return hid @ w_down[e]

    outs = lax.map(run_block, (buf_t.reshape(n_blocks, MOE_BLOCK), block_e))
    y = jnp.zeros((n_tok, d), f32).at[buf_t].add(
        outs.reshape(cap, d).astype(f32) * buf_w[:, None])
    return y.reshape(bsz, seq, d).astype(h.dtype)


def setup_inputs(seed: int = 0) -> dict:
    key = jax.random.key(seed)
    ks = jax.random.split(key, 32)
    f32 = jnp.float32
    L = DEPTH
    qk = GLA_HEADS * GLA_DK

    def nrm(k, shape, scale):
        return jax.random.normal(k, shape, f32) * scale

    a_c = jax.random.uniform(ks[14], (L, D_LRU), f32, 0.9, 0.999)
    a_base = a_c ** (1.0 / LRU_C)
    lru_lambda = jnp.log(a_base) - jnp.log1p(-a_base)
    return {
        "x": nrm(ks[0], (BATCH, SEQ, D_MODEL), 1.0),
        "c": nrm(ks[1], (BATCH, D_MODEL), 1.0),
        "w_ada": nrm(ks[2], (L, D_MODEL, 6 * D_MODEL), D_MODEL ** -0.5),
        "b_ada": nrm(ks[3], (L, 6 * D_MODEL), 0.02),
        "g_mix": 1.0 + nrm(ks[4], (L, D_MODEL), 0.1),
        "g_ffn": 1.0 + nrm(ks[5], (L, D_MODEL), 0.1),
        "g_final": 1.0 + nrm(ks[6], (D_MODEL,), 0.1),
        "w_in": nrm(ks[7], (L, D_MODEL, D_IN_PROJ), D_MODEL ** -0.5),
        "conv_w": nrm(ks[8], (L, CONV_WIDTH, D_LRU), CONV_WIDTH ** -0.5),
        "conv_b": nrm(ks[9], (L, D_LRU), 0.02),
        "lru_wr": nrm(ks[10], (L, LRU_HEADS, LRU_HEAD_DIM, LRU_HEAD_DIM), LRU_HEAD_DIM ** -0.5),
        "lru_br": nrm(ks[11], (L, D_LRU), 0.02),
        "lru_wi": nrm(ks[12], (L, LRU_HEADS, LRU_HEAD_DIM, LRU_HEAD_DIM), LRU_HEAD_DIM ** -0.5),
        "lru_bi": nrm(ks[13], (L, D_LRU), 0.02),
        "lru_lambda": lru_lambda,
        "gla_wa2": nrm(ks[15], (L, GLA_GATE_RANK, qk), GLA_GATE_RANK ** -0.5),
        "gla_ba": nrm(ks[16], (L, qk), 0.02),
        "gla_gnorm": 1.0 + nrm(ks[17], (L, GLA_DV), 0.1),
        "w_out": nrm(ks[18], (L, D_MIX, D_MODEL), D_MIX ** -0.5),
        "w_coarse": nrm(ks[19], (L, D_MODEL, N_GROUPS), D_MODEL ** -0.5),
        "b_coarse": nrm(ks[20], (L, N_GROUPS), 0.01),
        "w_fine": nrm(ks[21], (L, D_MODEL, N_EXPERTS), D_MODEL ** -0.5),
        "b_fine": nrm(ks[22], (L, N_EXPERTS), 0.01),
        "w_gate": nrm(ks[23], (L, N_EXPERTS, D_MODEL, D_EXPERT), D_MODEL ** -0.5),
        "w_up": nrm(ks[24], (L, N_EXPERTS, D_MODEL, D_EXPERT), D_MODEL ** -0.5),
        "w_down": nrm(ks[25], (L, N_EXPERTS, D_EXPERT, D_MODEL), D_EXPERT ** -0.5),
    }


def reference(x, c, w_ada, b_ada, g_mix, g_ffn, g_final, w_in, conv_w, conv_b, lru_wr, lru_br,
              lru_wi, lru_bi, lru_lambda, gla_wa2, gla_ba, gla_gnorm, w_out, w_coarse, b_coarse,
              w_fine, b_fine, w_gate, w_up, w_down):
    bsz, seq, _ = x.shape
    qk = GLA_HEADS * GLA_DK
    splits = [D_LRU, 2 * D_LRU, 2 * D_LRU + qk, 2 * D_LRU + 2 * qk,
              2 * D_LRU + 2 * qk + D_GLA, 2 * D_LRU + 2 * qk + 2 * D_GLA]
    for l in range(DEPTH):
        mod = jax.nn.silu(c) @ w_ada[l] + b_ada[l]
        sh1, sc1, gt1, sh2, sc2, gt2 = [m[:, None, :] for m in jnp.split(mod, 6, axis=-1)]

        h = rms_norm(x, g_mix[l]) * (1 + sc1) + sh1
        proj = h @ w_in[l]
        lru_x, lru_y, q, k, v, g_out, gate_lr = jnp.split(proj, splits, axis=-1)

        lru_h = rg_lru(causal_depthwise_conv(lru_x, conv_w[l], conv_b[l]),
                       lru_wr[l], lru_br[l], lru_wi[l], lru_bi[l], lru_lambda[l])
        lru_out = lru_h * jax.nn.gelu(lru_y)

        log_alpha = jax.nn.log_sigmoid(
            (gate_lr @ gla_wa2[l] + gla_ba[l]).astype(jnp.float32)) / GLA_GATE_NORM
        o = gla_chunked(q.reshape(bsz, seq, GLA_HEADS, GLA_DK),
                        k.reshape(bsz, seq, GLA_HEADS, GLA_DK),
                        v.reshape(bsz, seq, GLA_HEADS, GLA_DV),
                        log_alpha.reshape(bsz, seq, GLA_HEADS, GLA_DK))
        gla_out = rms_norm(o, gla_gnorm[l]).reshape(bsz, seq, D_GLA) * jax.nn.silu(g_out)

        mix = jnp.concatenate([lru_out, gla_out], axis=-1) @ w_out[l]
        x = x + gt1 * mix

        h2 = rms_norm(x, g_ffn[l]) * (1 + sc2) + sh2
        x = x + gt2 * hierarchical_moe(h2, w_coarse[l], b_coarse[l], w_fine[l], b_fine[l],
                                       w_gate[l], w_up[l], w_down[l])
    return rms_norm(x, g_final)
```

```python
import functools

import jax
import jax.numpy as jnp
import numpy as np
from jax import lax
from jax.experimental import pallas as pl
from jax.experimental.pallas import tpu as pltpu

F32 = jnp.float32
BF16 = jnp.bfloat16

SUBLANES = 8
LANES = 128
VMEM_LIMIT_BYTES = 56 * 1024 * 1024

D_MODEL = 1024
BATCH = SUBLANES
D_LRU = 512
LRU_HEADS = 8
CONV_WIDTH = 4
LRU_C = 8.0
D_GLA = 512
GLA_HEADS = 4
GLA_DV = 128
GLA_DK = 64
GLA_GATE_RANK = 16
GLA_GATE_NORM = 16.0
GLA_CHUNK = 64
N_GROUPS = 4
EXPERTS_PER_GROUP = 8
N_EXPERTS = 32
D_EXPERT = 512
MOE_BLOCK = 256
EPS = 1e-6

QK = GLA_HEADS * GLA_DK
ROWS = GLA_CHUNK * BATCH
HALO = (CONV_WIDTH - 1) * BATCH
SLAB = D_MODEL // LANES
C_LX, C_LY, C_Q, C_K, C_V, C_G, C_GL = 0, 512, 1024, 1280, 1536, 2048, 2560
D_IN_PAD = 2688
ROUTE_TILE = 512
TOK_TILE = 256


def _dot(a, b):
    return jnp.dot(a, b, preferred_element_type=F32)


def _dot_nt(a, b):
    return lax.dot_general(a, b, (((1,), (1,)), ((), ())), preferred_element_type=F32)


def _softplus(z):
    return jnp.maximum(z, 0.0) + jnp.log1p(jnp.exp(-jnp.abs(z)))


def _sigmoid(z):
    return 1.0 / (1.0 + jnp.exp(-z))


def _rms(x):
    return x * lax.rsqrt(jnp.mean(x * x, axis=-1, keepdims=True) + EPS)


def _slab_bcast(v, n):
    c = v.shape[-1]
    return jnp.broadcast_to(v[None], (n, SUBLANES, c)).reshape(n * SUBLANES, c)


def _ada_kernel(c_ref, w_ref, b_ref, o_ref):
    c = c_ref[...]
    s = c * _sigmoid(c)
    o_ref[...] = jnp.dot(s, w_ref[...], preferred_element_type=F32,
                         precision=lax.Precision.HIGHEST) + b_ref[...]


def _ada(c, w, b):
    n_out = w.shape[1]
    tn = 1024
    return pl.pallas_call(
        _ada_kernel,
        grid=(n_out // tn,),
        in_specs=[pl.BlockSpec((BATCH, D_MODEL), lambda j: (0, 0)),
                  pl.BlockSpec((D_MODEL, tn), lambda j: (0, j)),
                  pl.BlockSpec((1, tn), lambda j: (0, j))],
        out_specs=pl.BlockSpec((BATCH, tn), lambda j: (0, j)),
        out_shape=jax.ShapeDtypeStruct((BATCH, n_out), F32),
        name="ada",
    )(c, w, b.reshape(1, n_out))


def _mixer_kernel(x_ref, sc_ref, sh_ref, gt_ref, gmix_ref, win_ref, cw_ref, cb_ref,
                  wr_ref, br_ref, wi_ref, bi_ref, lam_ref, wa2_ref, ba_ref, gn_ref, wout_ref,
                  o_ref,
                  xbuf, a_s, u_s, hs_s, hc_s, la_s, bc_s, st_s, cm_s):
    i = pl.program_id(0)
    nt = GLA_CHUNK

    @pl.when(i == 0)
    def _init():
        xbuf[0:HALO, :] = jnp.zeros((HALO, D_LRU), F32)
        hc_s[...] = jnp.zeros_like(hc_s)
        st_s[...] = jnp.zeros_like(st_s)
        r = lax.broadcasted_iota(jnp.int32, (ROWS, ROWS), 0)
        c = lax.broadcasted_iota(jnp.int32, (ROWS, ROWS), 1)
        same_seq = (r & (BATCH - 1)) == (c & (BATCH - 1))
        cm_s[...] = jnp.where(same_seq & ((c >> 3) <= (r >> 3)), 1.0, 0.0).astype(F32)

    x = x_ref[...]
    y = _rms(x) * gmix_ref[...]
    h = (y * _slab_bcast(sc_ref[...], nt) + _slab_bcast(sh_ref[...], nt)).astype(BF16)

    def proj(c0, c1):
        return _dot(h, win_ref[:, c0:c1])

    xbuf[HALO:HALO + ROWS, :] = proj(C_LX, C_LY)
    cw = cw_ref[...]
    cx = cb_ref[...] + sum(cw[k:k + 1, :] * xbuf[k * BATCH:k * BATCH + ROWS, :]
                           for k in range(CONV_WIDTH))
    xbuf[0:HALO, :] = xbuf[ROWS:ROWS + HALO, :]
    cxb = cx.astype(BF16)
    r_gate = _sigmoid(_dot(cxb, wr_ref[...]) + br_ref[...])
    i_gate = _sigmoid(_dot(cxb, wi_ref[...]) + bi_ref[...])
    log_a = (-LRU_C) * r_gate * _softplus(-lam_ref[...])
    a_s[...] = jnp.exp(log_a)
    th = jnp.tanh(log_a)
    u_s[...] = jnp.sqrt(-2.0 * th / (1.0 - th)) * (i_gate * cx)
    hcur = hc_s[...]
    for t in range(nt):
        sl = slice(t * BATCH, (t + 1) * BATCH)
        hcur = a_s[sl, :] * hcur + u_s[sl, :]
        hs_s[sl, :] = hcur
    hc_s[...] = hcur
    lru_out = hs_s[...] * jax.nn.gelu(proj(C_LY, C_Q), approximate=True)

    gate_lr = proj(C_GL, D_IN_PAD).astype(BF16)
    z = _dot(gate_lr, wa2_ref[...]) + ba_ref[...]
    la_s[...] = -_softplus(-z) * (1.0 / GLA_GATE_NORM)
    bcur = jnp.zeros((BATCH, QK), F32)
    for t in range(nt):
        sl = slice(t * BATCH, (t + 1) * BATCH)
        bcur = bcur + la_s[sl, :]
        bc_s[sl, :] = bcur
    bc = bc_s[...]
    e_last = jnp.exp(bcur)
    q_dec = proj(C_Q, C_K) * (GLA_DK ** -0.5) * jnp.exp(bc)
    kk = proj(C_K, C_V)
    k_dec = kk * jnp.exp(-bc)
    k_last = kk * jnp.exp(_slab_bcast(bcur, nt) - bc)
    vv = proj(C_V, C_G)
    gg = proj(C_G, C_GL)
    causal = cm_s[...] > 0.5

    lane = lax.broadcasted_iota(jnp.int32, (SUBLANES, LANES), 1)
    sub = lax.broadcasted_iota(jnp.int32, (SUBLANES, LANES), 0)
    half = lane >> 6
    seq_sel = [(sub == 2 * j + half).astype(F32) for j in range(BATCH // 2)]

    def expand(m):
        n = m.shape[0] // SUBLANES
        m3 = m.reshape(n, SUBLANES, LANES)
        return jnp.concatenate([(m3 * s[None]).reshape(m.shape) for s in seq_sel], axis=1)

    def both_halves(m, hh):
        keep = (lax.broadcasted_iota(jnp.int32, m.shape, 1) >> 6) == hh
        mh = jnp.where(keep, m, 0.0)
        return mh, mh + pltpu.roll(mh, GLA_DK, axis=1)

    gla_parts = []
    for hd in range(GLA_HEADS):
        p, hh = hd // 2, hd % 2
        pc = slice(p * LANES, (p + 1) * LANES)
        qh, q_both = both_halves(q_dec[:, pc], hh)
        _, k_both = both_halves(k_last[:, pc], hh)
        _, e_both = both_halves(e_last[:, pc], hh)
        v_h = vv[:, hd * GLA_DV:(hd + 1) * GLA_DV]
        v_hb = v_h.astype(BF16)
        scores = _dot_nt(qh.astype(BF16), k_dec[:, pc].astype(BF16))
        scores = jnp.where(causal, scores, 0.0)
        o_h = _dot(scores.astype(BF16), v_hb)
        st = st_s[hd]
        o_h = o_h + _dot_nt(expand(q_both).astype(BF16), st.astype(BF16))
        kv_t = _dot(v_h.T.astype(BF16), expand(k_both).astype(BF16))
        decay = jnp.concatenate(
            [jnp.sum(e_both * s, axis=0, keepdims=True) for s in seq_sel], axis=1)
        st_s[hd] = st * decay + kv_t
        o_n = _rms(o_h) * gn_ref[...]
        g_h = gg[:, hd * GLA_DV:(hd + 1) * GLA_DV]
        gla_parts.append(o_n * (g_h * _sigmoid(g_h)))

    mix_in = jnp.concatenate([lru_out] + gla_parts, axis=1).astype(BF16)
    mix = _dot(mix_in, wout_ref[...])
    o_ref[...] = x + _slab_bcast(gt_ref[...], nt) * mix


def _const_spec(shape):
    nd = len(shape)
    return pl.BlockSpec(shape, lambda i: (0,) * nd)


def _mixer(xt, sc1p, sh1, gt1, g_mix, w_in_p, conv_w, conv_b, wr_d, br, wi_d, bi, lam,
           wa2_p, ba, gn, w_out_b):
    n = xt.shape[0]
    consts = [sc1p, sh1, gt1, g_mix, w_in_p, conv_w, conv_b, wr_d, br, wi_d, bi, lam,
              wa2_p, ba, gn, w_out_b]
    return pl.pallas_call(
        _mixer_kernel,
        grid=(n // ROWS,),
        in_specs=[pl.BlockSpec((ROWS, D_MODEL), lambda i: (i, 0))]
        + [_const_spec(a.shape) for a in consts],
        out_specs=pl.BlockSpec((ROWS, D_MODEL), lambda i: (i, 0)),
        out_shape=jax.ShapeDtypeStruct((n, D_MODEL), F32),
        scratch_shapes=[
            pltpu.VMEM((HALO + ROWS, D_LRU), F32),
            pltpu.VMEM((ROWS, D_LRU), F32),
            pltpu.VMEM((ROWS, D_LRU), F32),
            pltpu.VMEM((ROWS, D_LRU), F32),
            pltpu.VMEM((BATCH, D_LRU), F32),
            pltpu.VMEM((ROWS, QK), F32),
            pltpu.VMEM((ROWS, QK), F32),
            pltpu.VMEM((GLA_HEADS, GLA_DV, BATCH * GLA_DK), F32),
            pltpu.VMEM((ROWS, ROWS), F32),
        ],
        compiler_params=pltpu.CompilerParams(dimension_semantics=("arbitrary",),
                                             vmem_limit_bytes=VMEM_LIMIT_BYTES),
        name="mixer",
    )(xt, *consts)


def _router_kernel(x_ref, sc_ref, sh_ref, gffn_ref, wr_ref, br_ref,
                   h2_ref, meta_ref, metat_ref, cnt_ref, base_s, tri_s):
    i = pl.program_id(0)
    nt = ROUTE_TILE // BATCH

    @pl.when(i == 0)
    def _init():
        base_s[...] = jnp.zeros_like(base_s)
        r = lax.broadcasted_iota(jnp.int32, (ROUTE_TILE, ROUTE_TILE), 0)
        c = lax.broadcasted_iota(jnp.int32, (ROUTE_TILE, ROUTE_TILE), 1)
        tri_s[...] = jnp.where(c < r, 1.0, 0.0).astype(BF16)

    x = x_ref[...]
    h2 = _rms(x) * gffn_ref[...] * _slab_bcast(sc_ref[...], nt) + _slab_bcast(sh_ref[...], nt)
    h2_ref[...] = h2
    logits = _dot(h2.astype(BF16), wr_ref[...]) + br_ref[...]
    lane = lax.broadcasted_iota(jnp.int32, logits.shape, 1)
    neg = jnp.float32(-jnp.inf)

    def first_max(vals):
        m = jnp.max(vals, axis=-1, keepdims=True)
        idx = jnp.min(jnp.where(vals == m, lane, LANES), axis=-1, keepdims=True)
        return m, idx

    cl = jnp.where(lane < N_GROUPS, logits, neg)
    cmax, grp = first_max(cl)
    p_grp = 1.0 / jnp.sum(jnp.exp(cl - cmax), axis=-1, keepdims=True)
    lo = N_GROUPS + EXPERTS_PER_GROUP * grp
    fl = jnp.where((lane >= lo) & (lane < lo + EXPERTS_PER_GROUP), logits, neg)
    f1, i1 = first_max(fl)
    f2, i2 = first_max(jnp.where(lane == i1, neg, fl))
    z = jnp.sum(jnp.exp(fl - f1), axis=-1, keepdims=True)
    tp1 = 1.0 / z
    tp2 = jnp.exp(f2 - f1) / z
    w1 = p_grp * tp1 / (tp1 + tp2)
    w2 = p_grp * tp2 / (tp1 + tp2)

    hit1 = lane == i1
    hit2 = lane == i2
    assign = jnp.where(hit1 | hit2, 1.0, 0.0)
    before = _dot(tri_s[...], assign.astype(BF16)) + base_s[...]
    rank1 = jnp.sum(jnp.where(hit1, before, 0.0), axis=-1, keepdims=True)
    rank2 = jnp.sum(jnp.where(hit2, before, 0.0), axis=-1, keepdims=True)
    base_s[...] = base_s[...] + jnp.sum(assign, axis=0, keepdims=True)
    cnt_ref[...] = jnp.broadcast_to(base_s[...], cnt_ref.shape)

    cols = [(i1 - N_GROUPS).astype(F32), (i2 - N_GROUPS).astype(F32), rank1, rank2, w1, w2]
    meta = jnp.zeros(logits.shape, F32)
    for j, col in enumerate(cols):
        meta = jnp.where(lane == j, col, meta)
    meta_ref[...] = meta
    metat_ref[...] = meta.T[0:SUBLANES, :]


def _router(x1, sc2p, sh2, g_ffn, w_r, b_r):
    n = x1.shape[0]
    consts = [sc2p, sh2, g_ffn, w_r, b_r]
    return pl.pallas_call(
        _router_kernel,
        grid=(n // ROUTE_TILE,),
        in_specs=[pl.BlockSpec((ROUTE_TILE, D_MODEL), lambda i: (i, 0))]
        + [_const_spec(a.shape) for a in consts],
        out_specs=[pl.BlockSpec((ROUTE_TILE, D_MODEL), lambda i: (i, 0)),
                   pl.BlockSpec((ROUTE_TILE, LANES), lambda i: (i, 0)),
                   pl.BlockSpec((SUBLANES, ROUTE_TILE), lambda i: (0, i)),
                   pl.BlockSpec((SUBLANES, LANES), lambda i: (0, 0))],
        out_shape=[jax.ShapeDtypeStruct((n, D_MODEL), F32),
                   jax.ShapeDtypeStruct((n, LANES), F32),
                   jax.ShapeDtypeStruct((SUBLANES, n), F32),
                   jax.ShapeDtypeStruct((SUBLANES, LANES), F32)],
        scratch_shapes=[pltpu.VMEM((1, LANES), F32),
                        pltpu.VMEM((ROUTE_TILE, ROUTE_TILE), BF16)],
        compiler_params=pltpu.CompilerParams(dimension_semantics=("arbitrary",),
                                             vmem_limit_bytes=VMEM_LIMIT_BYTES),
        name="router",
    )(x1, *consts)


def _row_copy(src, dst, sem):
    return pltpu.make_async_copy(src, dst, sem)


def _dispatch_kernel(dest_ref, h_ref, xs_in_ref, xs_ref, sem):
    del xs_in_ref

    def issue(r, carry):
        for k in range(2):
            _row_copy(h_ref.at[r], xs_ref.at[dest_ref[k, r]], sem).start()
        return carry

    lax.fori_loop(0, TOK_TILE, issue, 0)

    def drain(r, carry):
        for k in range(2):
            _row_copy(h_ref.at[r], xs_ref.at[dest_ref[k, r]], sem).wait()
        return carry

    lax.fori_loop(0, TOK_TILE, drain, 0)


def _dispatch(dest, h2s, xs0):
    n = h2s.shape[0]
    return pl.pallas_call(
        _dispatch_kernel,
        grid=(n // TOK_TILE,),
        in_specs=[pl.BlockSpec((2, TOK_TILE), lambda i: (0, i), memory_space=pltpu.SMEM),
                  pl.BlockSpec((TOK_TILE, SLAB, LANES), lambda i: (i, 0, 0)),
                  pl.BlockSpec(memory_space=pl.ANY)],
        out_specs=pl.BlockSpec(memory_space=pl.ANY),
        out_shape=jax.ShapeDtypeStruct(xs0.shape, xs0.dtype),
        scratch_shapes=[pltpu.SemaphoreType.DMA(())],
        input_output_aliases={2: 0},
        compiler_params=pltpu.CompilerParams(dimension_semantics=("arbitrary",),
                                             has_side_effects=True),
        name="dispatch",
    )(dest, h2s, xs0)


def _slab_rows_to_matrix(ref, rows):
    return jnp.concatenate([ref[pl.ds(s, rows, stride=SLAB), :] for s in range(SLAB)], axis=1)


def _matrix_to_slab_rows(ref, val, rows):
    for s in range(SLAB):
        ref[pl.ds(s, rows, stride=SLAB), :] = val[:, s * LANES:(s + 1) * LANES]


def _expert_kernel(be_ref, nb_ref, xs_ref, wg_ref, wu_ref, wd_ref, ys_ref):
    i = pl.program_id(0)

    @pl.when(i < nb_ref[0])
    def _compute():
        xb = _slab_rows_to_matrix(xs_ref, MOE_BLOCK).astype(BF16)
        g = _dot(xb, wg_ref[0])
        u = _dot(xb, wu_ref[0])
        hid = (g * _sigmoid(g) * u).astype(BF16)
        _matrix_to_slab_rows(ys_ref, _dot(hid, wd_ref[0]), MOE_BLOCK)

    @pl.when(i >= nb_ref[0])
    def _skip():
        ys_ref[...] = jnp.zeros_like(ys_ref)


def _experts(block_e, n_used, xs, wg, wu, wd):
    n_blocks = xs.shape[0] // (MOE_BLOCK * SLAB)
    blk = MOE_BLOCK * SLAB
    grid_spec = pltpu.PrefetchScalarGridSpec(
        num_scalar_prefetch=2,
        grid=(n_blocks,),
        in_specs=[pl.BlockSpec((blk, LANES), lambda i, be, nb: (i, 0)),
                  pl.BlockSpec((1, D_MODEL, D_EXPERT), lambda i, be, nb: (be[i], 0, 0)),
                  pl.BlockSpec((1, D_MODEL, D_EXPERT), lambda i, be, nb: (be[i], 0, 0)),
                  pl.BlockSpec((1, D_EXPERT, D_MODEL), lambda i, be, nb: (be[i], 0, 0))],
        out_specs=pl.BlockSpec((blk, LANES), lambda i, be, nb: (i, 0)),
    )
    return pl.pallas_call(
        _expert_kernel,
        grid_spec=grid_spec,
        out_shape=jax.ShapeDtypeStruct(xs.shape, F32),
        compiler_params=pltpu.CompilerParams(dimension_semantics=("arbitrary",),
                                             vmem_limit_bytes=VMEM_LIMIT_BYTES),
        name="experts",
    )(block_e, n_used, xs, wg, wu, wd)


def _combine_kernel(dest_ref, x_ref, meta_ref, gt_ref, gfin_ref, ys_ref, o_ref, buf, sem):
    def slot(k, r):
        return buf.at[k, pl.ds(pl.multiple_of(r * SLAB, SLAB), SLAB)]

    def issue(r, carry):
        for k in range(2):
            _row_copy(ys_ref.at[dest_ref[k, r]], slot(k, r), sem).start()
        return carry

    lax.fori_loop(0, TOK_TILE, issue, 0)

    def drain(r, carry):
        for k in range(2):
            _row_copy(ys_ref.at[dest_ref[k, r]], slot(k, r), sem).wait()
        return carry

    lax.fori_loop(0, TOK_TILE, drain, 0)

    meta = meta_ref[...]
    y = (meta[:, 4:5] * _slab_rows_to_matrix(buf.at[0], TOK_TILE)
         + meta[:, 5:6] * _slab_rows_to_matrix(buf.at[1], TOK_TILE))
    x2 = x_ref[...] + _slab_bcast(gt_ref[...], TOK_TILE // BATCH) * y
    o_ref[...] = _rms(x2) * gfin_ref[...]


def _combine(dest, x1, meta, gt2, g_final, ys):
    n = x1.shape[0]
    return pl.pallas_call(
        _combine_kernel,
        grid=(n // TOK_TILE,),
        in_specs=[pl.BlockSpec((2, TOK_TILE), lambda i: (0, i), memory_space=pltpu.SMEM),
                  pl.BlockSpec((TOK_TILE, D_MODEL), lambda i: (i, 0)),
                  pl.BlockSpec((TOK_TILE, LANES), lambda i: (i, 0)),
                  _const_spec(gt2.shape),
                  _const_spec(g_final.shape),
                  pl.BlockSpec(memory_space=pl.ANY)],
        out_specs=pl.BlockSpec((TOK_TILE, D_MODEL), lambda i: (i, 0)),
        out_shape=jax.ShapeDtypeStruct((n, D_MODEL), F32),
        scratch_shapes=[pltpu.VMEM((2, TOK_TILE * SLAB, LANES), F32),
                        pltpu.SemaphoreType.DMA(())],
        compiler_params=pltpu.CompilerParams(dimension_semantics=("arbitrary",),
                                             vmem_limit_bytes=VMEM_LIMIT_BYTES),
        name="combine",
    )(dest, x1, meta, gt2, g_final, ys)


def _block_diag(w):
    h, d, _ = w.shape
    eye = jnp.eye(h, dtype=w.dtype)
    return (eye[:, None, :, None] * w[:, :, None, :]).reshape(h * d, h * d)


def kernel(x, c, w_ada, b_ada, g_mix, g_ffn, g_final, w_in, conv_w, conv_b, lru_wr, lru_br,
           lru_wi, lru_bi, lru_lambda, gla_wa2, gla_ba, gla_gnorm, w_out, w_coarse, b_coarse,
           w_fine, b_fine, w_gate, w_up, w_down):
    bsz, seq, d = x.shape
    assert bsz == BATCH and d == D_MODEL and seq % GLA_CHUNK == 0
    assert w_ada.shape[0] == 1, "single-layer problem"
    n = bsz * seq
    row = lambda v: v.reshape(1, -1)

    mod = _ada(c, w_ada[0], b_ada[0])
    sh1, sc1, gt1, sh2, sc2, gt2 = jnp.split(mod, 6, axis=-1)

    xt = x.transpose(1, 0, 2).reshape(n, d)
    w_in_p = jnp.pad(w_in[0], ((0, 0), (0, D_IN_PAD - w_in.shape[2]))).astype(BF16)
    wa2_p = jnp.pad(gla_wa2[0], ((0, LANES - GLA_GATE_RANK), (0, 0))).astype(BF16)
    x1 = _mixer(xt, 1.0 + sc1, sh1, gt1, row(g_mix[0]), w_in_p, conv_w[0], row(conv_b[0]),
                _block_diag(lru_wr[0]).astype(BF16), row(lru_br[0]),
                _block_diag(lru_wi[0]).astype(BF16), row(lru_bi[0]), row(lru_lambda[0]),
                wa2_p, row(gla_ba[0]), row(gla_gnorm[0]), w_out[0].astype(BF16))

    n_route = N_GROUPS + N_EXPERTS
    w_r = jnp.pad(jnp.concatenate([w_coarse[0], w_fine[0]], axis=1),
                  ((0, 0), (0, LANES - n_route))).astype(BF16)
    b_r = jnp.pad(jnp.concatenate([b_coarse[0], b_fine[0]]), (0, LANES - n_route)).reshape(1, LANES)
    h2, meta, meta_t, cnt = _router(x1, 1.0 + sc2, sh2, row(g_ffn[0]), w_r, b_r)

    counts = cnt[0, N_GROUPS:N_GROUPS + N_EXPERTS].astype(jnp.int32)
    padded = (counts + MOE_BLOCK - 1) // MOE_BLOCK * MOE_BLOCK
    pends = jnp.cumsum(padded)
    offs = pends - padded
    cap = (2 * n + MOE_BLOCK - 1) // MOE_BLOCK * MOE_BLOCK + N_EXPERTS * MOE_BLOCK
    n_blocks = cap // MOE_BLOCK
    block_e = jnp.minimum(
        jnp.searchsorted(pends, jnp.arange(n_blocks, dtype=jnp.int32) * MOE_BLOCK, side="right"),
        N_EXPERTS - 1).astype(jnp.int32)
    n_used = (pends[-1] // MOE_BLOCK).astype(jnp.int32).reshape(1)
    mi = meta_t[0:4].astype(jnp.int32)
    dest = offs[mi[0:2]] + mi[2:4]

    xs0 = jnp.zeros((cap, SLAB, LANES), F32)
    xs = _dispatch(dest, h2.reshape(n, SLAB, LANES), xs0)
    ys = _experts(block_e, n_used, xs.reshape(cap * SLAB, LANES),
                  w_gate[0].astype(BF16), w_up[0].astype(BF16), w_down[0].astype(BF16))
    out_t = _combine(dest, x1, meta, gt2, row(g_final), ys.reshape(cap, SLAB, LANES))
    return out_t.reshape(seq, bsz, d).transpose(1, 0, 2)
```

```python
import functools

import jax
import jax.numpy as jnp
import numpy as np
from jax import lax
from jax.experimental import pallas as pl
from jax.experimental.pallas import tpu as pltpu

F32 = jnp.float32
BF16 = jnp.bfloat16

SUBLANES = 8
LANES = 128
VMEM_LIMIT_BYTES = 56 * 1024 * 1024

D_MODEL = 1024
BATCH = SUBLANES
D_LRU = 512
LRU_HEADS = 8
CONV_WIDTH = 4
LRU_C = 8.0
D_GLA = 512
GLA_HEADS = 4
GLA_DV = 128
GLA_DK = 64
GLA_GATE_RANK = 16
GLA_GATE_NORM = 16.0
GLA_CHUNK = 64
N_GROUPS = 4
EXPERTS_PER_GROUP = 8
N_EXPERTS = 32
D_EXPERT = 512
MOE_BLOCK = 256
EPS = 1e-6

QK = GLA_HEADS * GLA_DK
ROWS = GLA_CHUNK * BATCH
HALO = (CONV_WIDTH - 1) * BATCH
SLAB = D_MODEL // LANES
C_LX, C_LY, C_Q, C_K, C_V, C_G, C_GL = 0, 512, 1024, 1280, 1536, 2048, 2560
D_IN_PAD = 2688
ROUTE_TILE = 512
TOK_TILE = 256


def _dot(a, b):
    return jnp.dot(a, b, preferred_element_type=F32)


def _dot_nt(a, b):
    return lax.dot_general(a, b, (((1,), (1,)), ((), ())), preferred_element_type=F32)


def _softplus(z):
    return jnp.maximum(z, 0.0) + jnp.log1p(jnp.exp(-jnp.abs(z)))


def _sigmoid(z):
    return 1.0 / (1.0 + jnp.exp(-z))


def _rms(x):
    return x * lax.rsqrt(jnp.mean(x * x, axis=-1, keepdims=True) + EPS)


def _slab_bcast(v, n):
    c = v.shape[-1]
    return jnp.broadcast_to(v[None], (n, SUBLANES, c)).reshape(n * SUBLANES, c)


def _ada_kernel(c_ref, w_ref, b_ref, o_ref):
    c = c_ref[...]
    s = c * _sigmoid(c)
    o_ref[...] = jnp.dot(s, w_ref[...], preferred_element_type=F32,
                         precision=lax.Precision.HIGHEST) + b_ref[...]


def _ada(c, w, b):
    n_out = w.shape[1]
    tn = 1024
    return pl.pallas_call(
        _ada_kernel,
        grid=(n_out // tn,),
        in_specs=[pl.BlockSpec((BATCH, D_MODEL), lambda j: (0, 0)),
                  pl.BlockSpec((D_MODEL, tn), lambda j: (0, j)),
                  pl.BlockSpec((1, tn), lambda j: (0, j))],
        out_specs=pl.BlockSpec((BATCH, tn), lambda j: (0, j)),
        out_shape=jax.ShapeDtypeStruct((BATCH, n_out), F32),
        name="ada",
    )(c, w, b.reshape(1, n_out))


def _mixer_kernel(x_ref, sc_ref, sh_ref, gt_ref, gmix_ref, win_ref, cw_ref, cb_ref,
                  wr_ref, br_ref, wi_ref, bi_ref, lam_ref, wa2_ref, ba_ref, gn_ref, wout_ref,
                  o_ref,
                  xbuf, a_s, u_s, hs_s, hc_s, la_s, bc_s, st_s, cm_s):
    i = pl.program_id(0)
    nt = GLA_CHUNK

    @pl.when(i == 0)
    def _init():
        xbuf[0:HALO, :] = jnp.zeros((HALO, D_LRU), F32)
        hc_s[...] = jnp.zeros_like(hc_s)
        st_s[...] = jnp.zeros_like(st_s)
        r = lax.broadcasted_iota(jnp.int32, (ROWS, ROWS), 0)
        c = lax.broadcasted_iota(jnp.int32, (ROWS, ROWS), 1)
        same_seq = (r & (BATCH - 1)) == (c & (BATCH - 1))
        cm_s[...] = jnp.where(same_seq & ((c >> 3) <= (r >> 3)), 1.0, 0.0).astype(F32)

    x = x_ref[...]
    y = _rms(x) * gmix_ref[...]
    h = (y * _slab_bcast(sc_ref[...], nt) + _slab_bcast(sh_ref[...], nt)).astype(BF16)

    def proj(c0, c1):
        return _dot(h, win_ref[:, c0:c1])

    xbuf[HALO:HALO + ROWS, :] = proj(C_LX, C_LY)
    cw = cw_ref[...]
    cx = cb_ref[...] + sum(cw[k:k + 1, :] * xbuf[k * BATCH:k * BATCH + ROWS, :]
                           for k in range(CONV_WIDTH))
    xbuf[0:HALO, :] = xbuf[ROWS:ROWS + HALO, :]
    cxb = cx.astype(BF16)
    r_gate = _sigmoid(_dot(cxb, wr_ref[...]) + br_ref[...])
    i_gate = _sigmoid(_dot(cxb, wi_ref[...]) + bi_ref[...])
    log_a = (-LRU_C) * r_gate * _softplus(-lam_ref[...])
    a_s[...] = jnp.exp(log_a)
    th = jnp.tanh(log_a)
    u_s[...] = jnp.sqrt(-2.0 * th / (1.0 - th)) * (i_gate * cx)
    hcur = hc_s[...]
    for t in range(nt):
        sl = slice(t * BATCH, (t + 1) * BATCH)
        hcur = a_s[sl, :] * hcur + u_s[sl, :]
        hs_s[sl, :] = hcur
    hc_s[...] = hcur
    lru_out = hs_s[...] * jax.nn.gelu(proj(C_LY, C_Q), approximate=True)

    gate_lr = proj(C_GL, D_IN_PAD).astype(BF16)
    z = _dot(gate_lr, wa2_ref[...]) + ba_ref[...]
    la_s[...] = -_softplus(-z) * (1.0 / GLA_GATE_NORM)
    bcur = jnp.zeros((BATCH, QK), F32)
    for t in range(nt):
        sl = slice(t * BATCH, (t + 1) * BATCH)
        bcur = bcur + la_s[sl, :]
        bc_s[sl, :] = bcur
    bc = bc_s[...]
    e_last = jnp.exp(bcur)
    q_dec = proj(C_Q, C_K) * (GLA_DK ** -0.5) * jnp.exp(bc)
    kk = proj(C_K, C_V)
    k_dec = kk * jnp.exp(-bc)
    k_last = kk * jnp.exp(_slab_bcast(bcur, nt) - bc)
    vv = proj(C_V, C_G)
    gg = proj(C_G, C_GL)
    causal = cm_s[...] > 0.5

    lane = lax.broadcasted_iota(jnp.int32, (SUBLANES, LANES), 1)
    sub = lax.broadcasted_iota(jnp.int32, (SUBLANES, LANES), 0)
    half = lane >> 6
    seq_sel = [(sub == 2 * j + half).astype(F32) for j in range(BATCH // 2)]

    def expand(m):
        n = m.shape[0] // SUBLANES
        m3 = m.reshape(n, SUBLANES, LANES)
        return jnp.concatenate([(m3 * s[None]).reshape(m.shape) for s in seq_sel], axis=1)

    def both_halves(m, hh):
        keep = (lax.broadcasted_iota(jnp.int32, m.shape, 1) >> 6) == hh
        mh = jnp.where(keep, m, 0.0)
        return mh, mh + pltpu.roll(mh, GLA_DK, axis=1)

    gla_parts = []
    for hd in range(GLA_HEADS):
        p, hh = hd // 2, hd % 2
        pc = slice(p * LANES, (p + 1) * LANES)
        qh, q_both = both_halves(q_dec[:, pc], hh)
        _, k_both = both_halves(k_last[:, pc], hh)
        _, e_both = both_halves(e_last[:, pc], hh)
        v_h = vv[:, hd * GLA_DV:(hd + 1) * GLA_DV]
        v_hb = v_h.astype(BF16)
        scores = _dot_nt(qh.astype(BF16), k_dec[:, pc].astype(BF16))
        scores = jnp.where(causal, scores, 0.0)
        o_h = _dot(scores.astype(BF16), v_hb)
        st = st_s[hd]
        o_h = o_h + _dot_nt(expand(q_both).astype(BF16), st.astype(BF16))
        kv_t = _dot(v_h.T.astype(BF16), expand(k_both).astype(BF16))
        decay = jnp.concatenate(
            [jnp.sum(e_both * s, axis=0, keepdims=True) for s in seq_sel], axis=1)
        st_s[hd] = st * decay + kv_t
        o_n = _rms(o_h) * gn_ref[...]
        g_h = gg[:, hd * GLA_DV:(hd + 1) * GLA_DV]
        gla_parts.append(o_n * (g_h * _sigmoid(g_h)))

    mix_in = jnp.concatenate([lru_out] + gla_parts, axis=1).astype(BF16)
    mix = _dot(mix_in, wout_ref[...])
    o_ref[...] = x + _slab_bcast(gt_ref[...], nt) * mix


def _const_spec(shape):
    nd = len(shape)
    return pl.BlockSpec(shape, lambda i: (0,) * nd)


def _mixer(xt, sc1p, sh1, gt1, g_mix, w_in_p, conv_w, conv_b, wr_d, br, wi_d, bi, lam,
           wa2_p, ba, gn, w_out_b):
    n = xt.shape[0]
    consts = [sc1p, sh1, gt1, g_mix, w_in_p, conv_w, conv_b, wr_d, br, wi_d, bi, lam,
              wa2_p, ba, gn, w_out_b]
    return pl.pallas_call(
        _mixer_kernel,
        grid=(n // ROWS,),
        in_specs=[pl.BlockSpec((ROWS, D_MODEL), lambda i: (i, 0))]
        + [_const_spec(a.shape) for a in consts],
        out_specs=pl.BlockSpec((ROWS, D_MODEL), lambda i: (i, 0)),
        out_shape=jax.ShapeDtypeStruct((n, D_MODEL), F32),
        scratch_shapes=[
            pltpu.VMEM((HALO + ROWS, D_LRU), F32),
            pltpu.VMEM((ROWS, D_LRU), F32),
            pltpu.VMEM((ROWS, D_LRU), F32),
            pltpu.VMEM((ROWS, D_LRU), F32),
            pltpu.VMEM((BATCH, D_LRU), F32),
            pltpu.VMEM((ROWS, QK), F32),
            pltpu.VMEM((ROWS, QK), F32),
            pltpu.VMEM((GLA_HEADS, GLA_DV, BATCH * GLA_DK), F32),
            pltpu.VMEM((ROWS, ROWS), F32),
        ],
        compiler_params=pltpu.CompilerParams(dimension_semantics=("arbitrary",),
                                             vmem_limit_bytes=VMEM_LIMIT_BYTES),
        name="mixer",
    )(xt, *consts)


def _router_kernel(x_ref, sc_ref, sh_ref, gffn_ref, wr_ref, br_ref,
                   h2_ref, meta_ref, metat_ref, cnt_ref, base_s, tri_s):
    i = pl.program_id(0)
    nt = ROUTE_TILE // BATCH

    @pl.when(i == 0)
    def _init():
        base_s[...] = jnp.zeros_like(base_s)
        r = lax.broadcasted_iota(jnp.int32, (ROUTE_TILE, ROUTE_TILE), 0)
        c = lax.broadcasted_iota(jnp.int32, (ROUTE_TILE, ROUTE_TILE), 1)
        tri_s[...] = jnp.where(c < r, 1.0, 0.0).astype(BF16)

    x = x_ref[...]
    h2 = _rms(x) * gffn_ref[...] * _slab_bcast(sc_ref[...], nt) + _slab_bcast(sh_ref[...], nt)
    h2_ref[...] = h2
    logits = _dot(h2.astype(BF16), wr_ref[...]) + br_ref[...]
    lane = lax.broadcasted_iota(jnp.int32, logits.shape, 1)
    neg = jnp.float32(-jnp.inf)

    def first_max(vals):
        m = jnp.max(vals, axis=-1, keepdims=True)
        idx = jnp.min(jnp.where(vals == m, lane, LANES), axis=-1, keepdims=True)
        return m, idx

    cl = jnp.where(lane < N_GROUPS, logits, neg)
    cmax, grp = first_max(cl)
    p_grp = 1.0 / jnp.sum(jnp.exp(cl - cmax), axis=-1, keepdims=True)
    lo = N_GROUPS + EXPERTS_PER_GROUP * grp
    fl = jnp.where((lane >= lo) & (lane < lo + EXPERTS_PER_GROUP), logits, neg)
    f1, i1 = first_max(fl)
    f2, i2 = first_max(jnp.where(lane == i1, neg, fl))
    z = jnp.sum(jnp.exp(fl - f1), axis=-1, keepdims=True)
    tp1 = 1.0 / z
    tp2 = jnp.exp(f2 - f1) / z
    w1 = p_grp * tp1 / (tp1 + tp2)
    w2 = p_grp * tp2 / (tp1 + tp2)

    hit1 = lane == i1
    hit2 = lane == i2
    assign = jnp.where(hit1 | hit2, 1.0, 0.0)
    before = _dot(tri_s[...], assign.astype(BF16)) + base_s[...]
    rank1 = jnp.sum(jnp.where(hit1, before, 0.0), axis=-1, keepdims=True)
    rank2 = jnp.sum(jnp.where(hit2, before, 0.0), axis=-1, keepdims=True)
    base_s[...] = base_s[...] + jnp.sum(assign, axis=0, keepdims=True)
    cnt_ref[...] = jnp.broadcast_to(base_s[...], cnt_ref.shape)

    cols = [(i1 - N_GROUPS).astype(F32), (i2 - N_GROUPS).astype(F32), rank1, rank2, w1, w2]
    meta = jnp.zeros(logits.shape, F32)
    for j, col in enumerate(cols):
        meta = jnp.where(lane == j, col, meta)
    meta_ref[...] = meta
    metat_ref[...] = meta.T[0:SUBLANES, :]


def _router(x1, sc2p, sh2, g_ffn, w_r, b_r):
    n = x1.shape[0]
    consts = [sc2p, sh2, g_ffn, w_r, b_r]
    return pl.pallas_call(
        _router_kernel,
        grid=(n // ROUTE_TILE,),
        in_specs=[pl.BlockSpec((ROUTE_TILE, D_MODEL), lambda i: (i, 0))]
        + [_const_spec(a.shape) for a in consts],
        out_specs=[pl.BlockSpec((ROUTE_TILE, D_MODEL), lambda i: (i, 0)),
                   pl.BlockSpec((ROUTE_TILE, LANES), lambda i: (i, 0)),
                   pl.BlockSpec((SUBLANES, ROUTE_TILE), lambda i: (0, i)),
                   pl.BlockSpec((SUBLANES, LANES), lambda i: (0, 0))],
        out_shape=[jax.ShapeDtypeStruct((n, D_MODEL), F32),
                   jax.ShapeDtypeStruct((n, LANES), F32),
                   jax.ShapeDtypeStruct((SUBLANES, n), F32),
                   jax.ShapeDtypeStruct((SUBLANES, LANES), F32)],
        scratch_shapes=[pltpu.VMEM((1, LANES), F32),
                        pltpu.VMEM((ROUTE_TILE, ROUTE_TILE), BF16)],
        compiler_params=pltpu.CompilerParams(dimension_semantics=("arbitrary",),
                                             vmem_limit_bytes=VMEM_LIMIT_BYTES),
        name="router",
    )(x1, *consts)


def _slots_kernel(offs_ref, mt_ref, dest_ref):
    mt = mt_ref[...]
    e = mt[0:2, :].astype(jnp.int32)
    acc = mt[2:4, :].astype(jnp.int32)
    for j in range(N_EXPERTS):
        acc = acc + jnp.where(e == j, offs_ref[j], 0)
    dest_ref[...] = acc


def _slots(offs, meta_t):
    n = meta_t.shape[1]
    tl = min(n, 8192)
    grid_spec = pltpu.PrefetchScalarGridSpec(
        num_scalar_prefetch=1,
        grid=(n // tl,),
        in_specs=[pl.BlockSpec((SUBLANES, tl), lambda i, offs: (0, i))],
        out_specs=pl.BlockSpec((2, tl), lambda i, offs: (0, i)),
    )
    return pl.pallas_call(
        _slots_kernel,
        grid_spec=grid_spec,
        out_shape=jax.ShapeDtypeStruct((2, n), jnp.int32),
        name="slots",
    )(offs, meta_t)


def _row_copy(src, dst, sem):
    return pltpu.make_async_copy(src, dst, sem)


def _dispatch_kernel(nv_ref, dest_ref, h_ref, xs_ref, zbuf, sem, zsem):
    n_blocks = xs_ref.shape[0] // MOE_BLOCK

    @pl.when(pl.program_id(0) == 0)
    def _zero_partial_blocks():
        zbuf[...] = jnp.zeros_like(zbuf)

        def fill(b):
            return _row_copy(zbuf, xs_ref.at[pl.ds(pl.multiple_of(b * MOE_BLOCK, MOE_BLOCK),
                                                   MOE_BLOCK)], zsem)

        def start(b, carry):
            @pl.when(nv_ref[b] < MOE_BLOCK)
            def _():
                fill(b).start()
            return carry

        def wait(b, carry):
            @pl.when(nv_ref[b] < MOE_BLOCK)
            def _():
                fill(b).wait()
            return carry

        lax.fori_loop(0, n_blocks, start, 0)
        lax.fori_loop(0, n_blocks, wait, 0)

    def issue(r, carry):
        for k in range(2):
            _row_copy(h_ref.at[r], xs_ref.at[dest_ref[k, r]], sem).start()
        return carry

    lax.fori_loop(0, TOK_TILE, issue, 0)
    for k in range(2):
        _row_copy(h_ref, xs_ref.at[pl.ds(0, TOK_TILE)], sem).wait()


def _dispatch(n_valid, dest, h2s, cap):
    n = h2s.shape[0]
    grid_spec = pltpu.PrefetchScalarGridSpec(
        num_scalar_prefetch=1,
        grid=(n // TOK_TILE,),
        in_specs=[pl.BlockSpec((2, TOK_TILE), lambda i, nv: (0, i), memory_space=pltpu.SMEM),
                  pl.BlockSpec((TOK_TILE, SLAB, LANES), lambda i, nv: (i, 0, 0))],
        out_specs=pl.BlockSpec(memory_space=pl.ANY),
        scratch_shapes=[pltpu.VMEM((MOE_BLOCK, SLAB, LANES), F32),
                        pltpu.SemaphoreType.DMA(()),
                        pltpu.SemaphoreType.DMA(())],
    )
    return pl.pallas_call(
        _dispatch_kernel,
        grid_spec=grid_spec,
        out_shape=jax.ShapeDtypeStruct((cap, SLAB, LANES), F32),
        compiler_params=pltpu.CompilerParams(dimension_semantics=("arbitrary",)),
        name="dispatch",
    )(n_valid, dest, h2s)


def _slab_rows_to_matrix(ref, rows):
    return jnp.concatenate([ref[pl.ds(s, rows, stride=SLAB), :] for s in range(SLAB)], axis=1)


def _matrix_to_slab_rows(ref, val, rows):
    for s in range(SLAB):
        ref[pl.ds(s, rows, stride=SLAB), :] = val[:, s * LANES:(s + 1) * LANES]


def _expert_kernel(be_ref, nv_ref, xs_ref, wg_ref, wu_ref, wd_ref, ys_ref):
    i = pl.program_id(0)
    nv = nv_ref[i]

    @pl.when(nv > 0)
    def _compute():
        xb = _slab_rows_to_matrix(xs_ref, MOE_BLOCK).astype(BF16)
        g = _dot(xb, wg_ref[0])
        u = _dot(xb, wu_ref[0])
        hid = (g * _sigmoid(g) * u).astype(BF16)
        _matrix_to_slab_rows(ys_ref, _dot(hid, wd_ref[0]), MOE_BLOCK)

    @pl.when(nv == 0)
    def _skip():
        ys_ref[...] = jnp.zeros_like(ys_ref)


def _experts(block_e, n_valid, xs, wg, wu, wd):
    n_blocks = xs.shape[0] // (MOE_BLOCK * SLAB)
    blk = MOE_BLOCK * SLAB
    grid_spec = pltpu.PrefetchScalarGridSpec(
        num_scalar_prefetch=2,
        grid=(n_blocks,),
        in_specs=[pl.BlockSpec((blk, LANES), lambda i, be, nv: (i, 0)),
                  pl.BlockSpec((1, D_MODEL, D_EXPERT), lambda i, be, nv: (be[i], 0, 0)),
                  pl.BlockSpec((1, D_MODEL, D_EXPERT), lambda i, be, nv: (be[i], 0, 0)),
                  pl.BlockSpec((1, D_EXPERT, D_MODEL), lambda i, be, nv: (be[i], 0, 0))],
        out_specs=pl.BlockSpec((blk, LANES), lambda i, be, nv: (i, 0)),
    )
    return pl.pallas_call(
        _expert_kernel,
        grid_spec=grid_spec,
        out_shape=jax.ShapeDtypeStruct(xs.shape, F32),
        compiler_params=pltpu.CompilerParams(dimension_semantics=("arbitrary",),
                                             vmem_limit_bytes=VMEM_LIMIT_BYTES),
        name="experts",
    )(block_e, n_valid, xs, wg, wu, wd)


def _combine_kernel(dest_ref, x_ref, meta_ref, gt_ref, gfin_ref, ys_ref, ys2d_ref, o_ref, buf,
                    sem):
    def slot(k, r):
        return buf.at[k, pl.ds(pl.multiple_of(r * SLAB, SLAB), SLAB)]

    def issue(r, carry):
        for k in range(2):
            _row_copy(ys_ref.at[dest_ref[k, r]], slot(k, r), sem).start()
        return carry

    lax.fori_loop(0, TOK_TILE, issue, 0)
    for k in range(2):
        _row_copy(ys2d_ref.at[pl.ds(0, TOK_TILE * SLAB)], buf.at[k], sem).wait()

    meta = meta_ref[...]
    y = (meta[:, 4:5] * _slab_rows_to_matrix(buf.at[0], TOK_TILE)
         + meta[:, 5:6] * _slab_rows_to_matrix(buf.at[1], TOK_TILE))
    x2 = x_ref[...] + _slab_bcast(gt_ref[...], TOK_TILE // BATCH) * y
    o_ref[...] = _rms(x2) * gfin_ref[...]


def _combine(dest, x1, meta, gt2, g_final, ys):
    n = x1.shape[0]
    return pl.pallas_call(
        _combine_kernel,
        grid=(n // TOK_TILE,),
        in_specs=[pl.BlockSpec((2, TOK_TILE), lambda i: (0, i), memory_space=pltpu.SMEM),
                  pl.BlockSpec((TOK_TILE, D_MODEL), lambda i: (i, 0)),
                  pl.BlockSpec((TOK_TILE, LANES), lambda i: (i, 0)),
                  _const_spec(gt2.shape),
                  _const_spec(g_final.shape),
                  pl.BlockSpec(memory_space=pl.ANY),
                  pl.BlockSpec(memory_space=pl.ANY)],
        out_specs=pl.BlockSpec((TOK_TILE, D_MODEL), lambda i: (i, 0)),
        out_shape=jax.ShapeDtypeStruct((n, D_MODEL), F32),
        scratch_shapes=[pltpu.VMEM((2, TOK_TILE * SLAB, LANES), F32),
                        pltpu.SemaphoreType.DMA(())],
        compiler_params=pltpu.CompilerParams(dimension_semantics=("arbitrary",),
                                             vmem_limit_bytes=VMEM_LIMIT_BYTES),
        name="combine",
    )(dest, x1, meta, gt2, g_final, ys.reshape(-1, SLAB, LANES), ys)


def _block_diag(w):
    h, d, _ = w.shape
    eye = jnp.eye(h, dtype=w.dtype)
    return (eye[:, None, :, None] * w[:, :, None, :]).reshape(h * d, h * d)


def kernel(x, c, w_ada, b_ada, g_mix, g_ffn, g_final, w_in, conv_w, conv_b, lru_wr, lru_br,
           lru_wi, lru_bi, lru_lambda, gla_wa2, gla_ba, gla_gnorm, w_out, w_coarse, b_coarse,
           w_fine, b_fine, w_gate, w_up, w_down):
    bsz, seq, d = x.shape
    assert bsz == BATCH and d == D_MODEL and seq % GLA_CHUNK == 0
    assert w_ada.shape[0] == 1, "single-layer problem"
    n = bsz * seq
    row = lambda v: v.reshape(1, -1)

    mod = _ada(c, w_ada[0], b_ada[0])
    sh1, sc1, gt1, sh2, sc2, gt2 = jnp.split(mod, 6, axis=-1)

    xt = x.transpose(1, 0, 2).reshape(n, d)
    w_in_p = jnp.pad(w_in[0], ((0, 0), (0, D_IN_PAD - w_in.shape[2]))).astype(BF16)
    wa2_p = jnp.pad(gla_wa2[0], ((0, LANES - GLA_GATE_RANK), (0, 0))).astype(BF16)
    x1 = _mixer(xt, 1.0 + sc1, sh1, gt1, row(g_mix[0]), w_in_p, conv_w[0], row(conv_b[0]),
                _block_diag(lru_wr[0]).astype(BF16), row(lru_br[0]),
                _block_diag(lru_wi[0]).astype(BF16), row(lru_bi[0]), row(lru_lambda[0]),
                wa2_p, row(gla_ba[0]), row(gla_gnorm[0]), w_out[0].astype(BF16))

    n_route = N_GROUPS + N_EXPERTS
    w_r = jnp.pad(jnp.concatenate([w_coarse[0], w_fine[0]], axis=1),
                  ((0, 0), (0, LANES - n_route))).astype(BF16)
    b_r = jnp.pad(jnp.concatenate([b_coarse[0], b_fine[0]]), (0, LANES - n_route)).reshape(1, LANES)
    h2, meta, meta_t, cnt = _router(x1, 1.0 + sc2, sh2, row(g_ffn[0]), w_r, b_r)

    counts = cnt[0, N_GROUPS:N_GROUPS + N_EXPERTS].astype(jnp.int32)
    padded = (counts + MOE_BLOCK - 1) // MOE_BLOCK * MOE_BLOCK
    pends = jnp.cumsum(padded)
    offs = pends - padded
    cap = (2 * n + MOE_BLOCK - 1) // MOE_BLOCK * MOE_BLOCK + N_EXPERTS * MOE_BLOCK
    n_blocks = cap // MOE_BLOCK
    starts = jnp.arange(n_blocks, dtype=jnp.int32) * MOE_BLOCK
    block_e = jnp.minimum(jnp.sum((pends[None, :] <= starts[:, None]).astype(jnp.int32), axis=1),
                          N_EXPERTS - 1)
    n_valid = jnp.clip(offs[block_e] + counts[block_e] - starts, 0, MOE_BLOCK).astype(jnp.int32)
    dest = _slots(offs.astype(jnp.int32), meta_t)

    xs = _dispatch(n_valid, dest, h2.reshape(n, SLAB, LANES), cap)
    ys = _experts(block_e, n_valid, xs.reshape(cap * SLAB, LANES),
                  w_gate[0].astype(BF16), w_up[0].astype(BF16), w_down[0].astype(BF16))
    out_t = _combine(dest, x1, meta, gt2, row(g_final), ys)
    return out_t.reshape(seq, bsz, d).transpose(1, 0, 2)
```

```python
import functools

import jax
import jax.numpy as jnp
import numpy as np
from jax import lax
from jax.experimental import pallas as pl
from jax.experimental.pallas import tpu as pltpu

F32 = jnp.float32
BF16 = jnp.bfloat16

SUBLANES = 8
LANES = 128
VMEM_LIMIT_BYTES = 56 * 1024 * 1024

D_MODEL = 1024
BATCH = SUBLANES
D_LRU = 512
LRU_HEADS = 8
CONV_WIDTH = 4
LRU_C = 8.0
D_GLA = 512
GLA_HEADS = 4
GLA_DV = 128
GLA_DK = 64
GLA_GATE_RANK = 16
GLA_GATE_NORM = 16.0
GLA_CHUNK = 64
N_GROUPS = 4
EXPERTS_PER_GROUP = 8
N_EXPERTS = 32
D_EXPERT = 512
MOE_BLOCK = 256
EPS = 1e-6

QK = GLA_HEADS * GLA_DK
ROWS = GLA_CHUNK * BATCH
HALO = (CONV_WIDTH - 1) * BATCH
SLAB = D_MODEL // LANES
C_LX, C_LY, C_Q, C_K, C_V, C_G, C_GL = 0, 512, 1024, 1280, 1536, 2048, 2560
D_IN_PAD = 2688
ROUTE_TILE = 512
TOK_TILE = 256


def _dot(a, b):
    return jnp.dot(a, b, preferred_element_type=F32)


def _dot_nt(a, b):
    return lax.dot_general(a, b, (((1,), (1,)), ((), ())), preferred_element_type=F32)


def _softplus(z):
    return jnp.maximum(z, 0.0) + jnp.log1p(jnp.exp(-jnp.abs(z)))


def _sigmoid(z):
    return 1.0 / (1.0 + jnp.exp(-z))


def _rms(x):
    return x * lax.rsqrt(jnp.mean(x * x, axis=-1, keepdims=True) + EPS)


def _slab_bcast(v, n):
    c = v.shape[-1]
    return jnp.broadcast_to(v[None], (n, SUBLANES, c)).reshape(n * SUBLANES, c)


def _ada_kernel(c_ref, w_ref, b_ref, o_ref):
    c = c_ref[...]
    s = c * _sigmoid(c)
    o_ref[...] = jnp.dot(s, w_ref[...], preferred_element_type=F32,
                         precision=lax.Precision.HIGHEST) + b_ref[...]


def _ada(c, w, b):
    n_out = w.shape[1]
    tn = 1024
    return pl.pallas_call(
        _ada_kernel,
        grid=(n_out // tn,),
        in_specs=[pl.BlockSpec((BATCH, D_MODEL), lambda j: (0, 0)),
                  pl.BlockSpec((D_MODEL, tn), lambda j: (0, j)),
                  pl.BlockSpec((1, tn), lambda j: (0, j))],
        out_specs=pl.BlockSpec((BATCH, tn), lambda j: (0, j)),
        out_shape=jax.ShapeDtypeStruct((BATCH, n_out), F32),
        name="ada",
    )(c, w, b.reshape(1, n_out))


def _mixer_kernel(x_ref, sc_ref, sh_ref, gt_ref, gmix_ref, win_ref, cw_ref, cb_ref,
                  wr_ref, br_ref, wi_ref, bi_ref, lam_ref, wa2_ref, ba_ref, gn_ref, wout_ref,
                  o_ref,
                  xbuf, a_s, u_s, hs_s, hc_s, la_s, bc_s, st_s, cm_s):
    i = pl.program_id(0)
    nt = GLA_CHUNK

    @pl.when(i == 0)
    def _init():
        xbuf[0:HALO, :] = jnp.zeros((HALO, D_LRU), F32)
        hc_s[...] = jnp.zeros_like(hc_s)
        st_s[...] = jnp.zeros_like(st_s)
        r = lax.broadcasted_iota(jnp.int32, (ROWS, ROWS), 0)
        c = lax.broadcasted_iota(jnp.int32, (ROWS, ROWS), 1)
        same_seq = (r & (BATCH - 1)) == (c & (BATCH - 1))
        cm_s[...] = jnp.where(same_seq & ((c >> 3) <= (r >> 3)), 1.0, 0.0).astype(F32)

    x = x_ref[...]
    y = _rms(x) * gmix_ref[...]
    h = (y * _slab_bcast(sc_ref[...], nt) + _slab_bcast(sh_ref[...], nt)).astype(BF16)

    def proj(c0, c1):
        return _dot(h, win_ref[:, c0:c1])

    xbuf[HALO:HALO + ROWS, :] = proj(C_LX, C_LY)
    cw = cw_ref[...]
    cx = cb_ref[...] + sum(cw[k:k + 1, :] * xbuf[k * BATCH:k * BATCH + ROWS, :]
                           for k in range(CONV_WIDTH))
    xbuf[0:HALO, :] = xbuf[ROWS:ROWS + HALO, :]
    cxb = cx.astype(BF16)
    r_gate = _sigmoid(_dot(cxb, wr_ref[...]) + br_ref[...])
    i_gate = _sigmoid(_dot(cxb, wi_ref[...]) + bi_ref[...])
    log_a = (-LRU_C) * r_gate * _softplus(-lam_ref[...])
    a_s[...] = jnp.exp(log_a)
    th = jnp.tanh(log_a)
    u_s[...] = jnp.sqrt(-2.0 * th / (1.0 - th)) * (i_gate * cx)
    hcur = hc_s[...]
    for t in range(nt):
        sl = slice(t * BATCH, (t + 1) * BATCH)
        hcur = a_s[sl, :] * hcur + u_s[sl, :]
        hs_s[sl, :] = hcur
    hc_s[...] = hcur
    lru_out = hs_s[...] * jax.nn.gelu(proj(C_LY, C_Q), approximate=True)

    gate_lr = proj(C_GL, D_IN_PAD).astype(BF16)
    z = _dot(gate_lr, wa2_ref[...]) + ba_ref[...]
    la_s[...] = -_softplus(-z) * (1.0 / GLA_GATE_NORM)
    bcur = jnp.zeros((BATCH, QK), F32)
    for t in range(nt):
        sl = slice(t * BATCH, (t + 1) * BATCH)
        bcur = bcur + la_s[sl, :]
        bc_s[sl, :] = bcur
    bc = bc_s[...]
    e_last = jnp.exp(bcur)
    q_dec = proj(C_Q, C_K) * (GLA_DK ** -0.5) * jnp.exp(bc)
    kk = proj(C_K, C_V)
    k_dec = kk * jnp.exp(-bc)
    k_last = kk * jnp.exp(_slab_bcast(bcur, nt) - bc)
    vv = proj(C_V, C_G)
    gg = proj(C_G, C_GL)
    causal = cm_s[...] > 0.5

    lane = lax.broadcasted_iota(jnp.int32, (SUBLANES, LANES), 1)
    sub = lax.broadcasted_iota(jnp.int32, (SUBLANES, LANES), 0)
    half = lane >> 6
    seq_sel = [(sub == 2 * j + half).astype(F32) for j in range(BATCH // 2)]

    def expand(m):
        n = m.shape[0] // SUBLANES
        m3 = m.reshape(n, SUBLANES, LANES)
        return jnp.concatenate([(m3 * s[None]).reshape(m.shape) for s in seq_sel], axis=1)

    def both_halves(m, hh):
        keep = (lax.broadcasted_iota(jnp.int32, m.shape, 1) >> 6) == hh
        mh = jnp.where(keep, m, 0.0)
        return mh, mh + pltpu.roll(mh, GLA_DK, axis=1)

    gla_parts = []
    for hd in range(GLA_HEADS):
        p, hh = hd // 2, hd % 2
        pc = slice(p * LANES, (p + 1) * LANES)
        qh, q_both = both_halves(q_dec[:, pc], hh)
        _, k_both = both_halves(k_last[:, pc], hh)
        _, e_both = both_halves(e_last[:, pc], hh)
        v_h = vv[:, hd * GLA_DV:(hd + 1) * GLA_DV]
        v_hb = v_h.astype(BF16)
        scores = _dot_nt(qh.astype(BF16), k_dec[:, pc].astype(BF16))
        scores = jnp.where(causal, scores, 0.0)
        o_h = _dot(scores.astype(BF16), v_hb)
        st = st_s[hd]
        o_h = o_h + _dot_nt(expand(q_both).astype(BF16), st.astype(BF16))
        kv_t = _dot(v_h.T.astype(BF16), expand(k_both).astype(BF16))
        decay = jnp.concatenate(
            [jnp.sum(e_both * s, axis=0, keepdims=True) for s in seq_sel], axis=1)
        st_s[hd] = st * decay + kv_t
        o_n = _rms(o_h) * gn_ref[...]
        g_h = gg[:, hd * GLA_DV:(hd + 1) * GLA_DV]
        gla_parts.append(o_n * (g_h * _sigmoid(g_h)))

    mix_in = jnp.concatenate([lru_out] + gla_parts, axis=1).astype(BF16)
    mix = _dot(mix_in, wout_ref[...])
    o_ref[...] = x + _slab_bcast(gt_ref[...], nt) * mix


def _const_spec(shape):
    nd = len(shape)
    return pl.BlockSpec(shape, lambda i: (0,) * nd)


def _mixer(xt, sc1p, sh1, gt1, g_mix, w_in_p, conv_w, conv_b, wr_d, br, wi_d, bi, lam,
           wa2_p, ba, gn, w_out_b):
    n = xt.shape[0]
    consts = [sc1p, sh1, gt1, g_mix, w_in_p, conv_w, conv_b, wr_d, br, wi_d, bi, lam,
              wa2_p, ba, gn, w_out_b]
    return pl.pallas_call(
        _mixer_kernel,
        grid=(n // ROWS,),
        in_specs=[pl.BlockSpec((ROWS, D_MODEL), lambda i: (i, 0))]
        + [_const_spec(a.shape) for a in consts],
        out_specs=pl.BlockSpec((ROWS, D_MODEL), lambda i: (i, 0)),
        out_shape=jax.ShapeDtypeStruct((n, D_MODEL), F32),
        scratch_shapes=[
            pltpu.VMEM((HALO + ROWS, D_LRU), F32),
            pltpu.VMEM((ROWS, D_LRU), F32),
            pltpu.VMEM((ROWS, D_LRU), F32),
            pltpu.VMEM((ROWS, D_LRU), F32),
            pltpu.VMEM((BATCH, D_LRU), F32),
            pltpu.VMEM((ROWS, QK), F32),
            pltpu.VMEM((ROWS, QK), F32),
            pltpu.VMEM((GLA_HEADS, GLA_DV, BATCH * GLA_DK), F32),
            pltpu.VMEM((ROWS, ROWS), F32),
        ],
        compiler_params=pltpu.CompilerParams(dimension_semantics=("arbitrary",),
                                             vmem_limit_bytes=VMEM_LIMIT_BYTES),
        name="mixer",
    )(xt, *consts)


def _router_kernel(x_ref, sc_ref, sh_ref, gffn_ref, wr_ref, br_ref,
                   h2_ref, meta_ref, metat_ref, cnt_ref, base_s, tri_s):
    i = pl.program_id(0)
    nt = ROUTE_TILE // BATCH

    @pl.when(i == 0)
    def _init():
        base_s[...] = jnp.zeros_like(base_s)
        r = lax.broadcasted_iota(jnp.int32, (ROUTE_TILE, ROUTE_TILE), 0)
        c = lax.broadcasted_iota(jnp.int32, (ROUTE_TILE, ROUTE_TILE), 1)
        tri_s[...] = jnp.where(c < r, 1.0, 0.0).astype(BF16)

    x = x_ref[...]
    h2 = _rms(x) * gffn_ref[...] * _slab_bcast(sc_ref[...], nt) + _slab_bcast(sh_ref[...], nt)
    h2_ref[...] = h2
    logits = _dot(h2.astype(BF16), wr_ref[...]) + br_ref[...]
    lane = lax.broadcasted_iota(jnp.int32, logits.shape, 1)
    neg = jnp.float32(-jnp.inf)

    def first_max(vals):
        m = jnp.max(vals, axis=-1, keepdims=True)
        idx = jnp.min(jnp.where(vals == m, lane, LANES), axis=-1, keepdims=True)
        return m, idx

    cl = jnp.where(lane < N_GROUPS, logits, neg)
    cmax, grp = first_max(cl)
    p_grp = 1.0 / jnp.sum(jnp.exp(cl - cmax), axis=-1, keepdims=True)
    lo = N_GROUPS + EXPERTS_PER_GROUP * grp
    fl = jnp.where((lane >= lo) & (lane < lo + EXPERTS_PER_GROUP), logits, neg)
    f1, i1 = first_max(fl)
    f2, i2 = first_max(jnp.where(lane == i1, neg, fl))
    z = jnp.sum(jnp.exp(fl - f1), axis=-1, keepdims=True)
    tp1 = 1.0 / z
    tp2 = jnp.exp(f2 - f1) / z
    w1 = p_grp * tp1 / (tp1 + tp2)
    w2 = p_grp * tp2 / (tp1 + tp2)

    hit1 = lane == i1
    hit2 = lane == i2
    assign = jnp.where(hit1 | hit2, 1.0, 0.0)
    before = _dot(tri_s[...], assign.astype(BF16)) + base_s[...]
    rank1 = jnp.sum(jnp.where(hit1, before, 0.0), axis=-1, keepdims=True)
    rank2 = jnp.sum(jnp.where(hit2, before, 0.0), axis=-1, keepdims=True)
    base_s[...] = base_s[...] + jnp.sum(assign, axis=0, keepdims=True)
    cnt_ref[...] = jnp.broadcast_to(base_s[...], cnt_ref.shape)

    cols = [(i1 - N_GROUPS).astype(F32), (i2 - N_GROUPS).astype(F32), rank1, rank2, w1, w2]
    meta = jnp.zeros(logits.shape, F32)
    for j, col in enumerate(cols):
        meta = jnp.where(lane == j, col, meta)
    meta_ref[...] = meta
    metat_ref[...] = meta.T[0:SUBLANES, :]


def _router(x1, sc2p, sh2, g_ffn, w_r, b_r):
    n = x1.shape[0]
    consts = [sc2p, sh2, g_ffn, w_r, b_r]
    return pl.pallas_call(
        _router_kernel,
        grid=(n // ROUTE_TILE,),
        in_specs=[pl.BlockSpec((ROUTE_TILE, D_MODEL), lambda i: (i, 0))]
        + [_const_spec(a.shape) for a in consts],
        out_specs=[pl.BlockSpec((ROUTE_TILE, D_MODEL), lambda i: (i, 0)),
                   pl.BlockSpec((ROUTE_TILE, LANES), lambda i: (i, 0)),
                   pl.BlockSpec((SUBLANES, ROUTE_TILE), lambda i: (0, i)),
                   pl.BlockSpec((SUBLANES, LANES), lambda i: (0, 0))],
        out_shape=[jax.ShapeDtypeStruct((n, D_MODEL), F32),
                   jax.ShapeDtypeStruct((n, LANES), F32),
                   jax.ShapeDtypeStruct((SUBLANES, n), F32),
                   jax.ShapeDtypeStruct((SUBLANES, LANES), F32)],
        scratch_shapes=[pltpu.VMEM((1, LANES), F32),
                        pltpu.VMEM((ROUTE_TILE, ROUTE_TILE), BF16)],
        compiler_params=pltpu.CompilerParams(dimension_semantics=("arbitrary",),
                                             vmem_limit_bytes=VMEM_LIMIT_BYTES),
        name="router",
    )(x1, *consts)


def _slots_kernel(offs_ref, mt_ref, dest_ref):
    mt = mt_ref[...]
    e = mt[0:2, :].astype(jnp.int32)
    acc = mt[2:4, :].astype(jnp.int32)
    for j in range(N_EXPERTS):
        acc = acc + jnp.where(e == j, offs_ref[j], 0)
    dest_ref[...] = acc


def _slots(offs, meta_t):
    n = meta_t.shape[1]
    tl = min(n, 8192)
    grid_spec = pltpu.PrefetchScalarGridSpec(
        num_scalar_prefetch=1,
        grid=(n // tl,),
        in_specs=[pl.BlockSpec((SUBLANES, tl), lambda i, offs: (0, i))],
        out_specs=pl.BlockSpec((2, tl), lambda i, offs: (0, i)),
    )
    return pl.pallas_call(
        _slots_kernel,
        grid_spec=grid_spec,
        out_shape=jax.ShapeDtypeStruct((2, n), jnp.int32),
        name="slots",
    )(offs, meta_t)


def _row_copy(src, dst, sem):
    return pltpu.make_async_copy(src, dst, sem)


def _dispatch_kernel(nv_ref, dest_ref, h_ref, xs_ref, zbuf, sem, zsem):
    n_blocks = xs_ref.shape[0] // MOE_BLOCK

    @pl.when(pl.program_id(0) == 0)
    def _zero_partial_blocks():
        zbuf[...] = jnp.zeros_like(zbuf)

        def fill(b):
            return _row_copy(zbuf, xs_ref.at[pl.ds(pl.multiple_of(b * MOE_BLOCK, MOE_BLOCK),
                                                   MOE_BLOCK)], zsem)

        def start(b, carry):
            @pl.when(nv_ref[b] < MOE_BLOCK)
            def _():
                fill(b).start()
            return carry

        def wait(b, carry):
            @pl.when(nv_ref[b] < MOE_BLOCK)
            def _():
                fill(b).wait()
            return carry

        lax.fori_loop(0, n_blocks, start, 0)
        lax.fori_loop(0, n_blocks, wait, 0)

    def issue(r, carry):
        for k in range(2):
            _row_copy(h_ref.at[r], xs_ref.at[dest_ref[k, r]], sem).start(priority=k)
        return carry

    lax.fori_loop(0, TOK_TILE, issue, 0)
    for k in range(2):
        _row_copy(h_ref, xs_ref.at[pl.ds(0, TOK_TILE)], sem).wait()


def _dispatch(n_valid, dest, h2s, cap):
    n = h2s.shape[0]
    grid_spec = pltpu.PrefetchScalarGridSpec(
        num_scalar_prefetch=1,
        grid=(n // TOK_TILE,),
        in_specs=[pl.BlockSpec((2, TOK_TILE), lambda i, nv: (0, i), memory_space=pltpu.SMEM),
                  pl.BlockSpec((TOK_TILE, SLAB, LANES), lambda i, nv: (i, 0, 0))],
        out_specs=pl.BlockSpec(memory_space=pl.ANY),
        scratch_shapes=[pltpu.VMEM((MOE_BLOCK, SLAB, LANES), F32),
                        pltpu.SemaphoreType.DMA(()),
                        pltpu.SemaphoreType.DMA(())],
    )
    return pl.pallas_call(
        _dispatch_kernel,
        grid_spec=grid_spec,
        out_shape=jax.ShapeDtypeStruct((cap, SLAB, LANES), F32),
        compiler_params=pltpu.CompilerParams(dimension_semantics=("arbitrary",)),
        name="dispatch",
    )(n_valid, dest, h2s)


def _slab_rows_to_matrix(ref, rows):
    return jnp.concatenate([ref[pl.ds(s, rows, stride=SLAB), :] for s in range(SLAB)], axis=1)


def _matrix_to_slab_rows(ref, val, rows):
    for s in range(SLAB):
        ref[pl.ds(s, rows, stride=SLAB), :] = val[:, s * LANES:(s + 1) * LANES]


def _expert_kernel(be_ref, nv_ref, xs_ref, wg_ref, wu_ref, wd_ref, ys_ref):
    i = pl.program_id(0)
    nv = nv_ref[i]

    @pl.when(nv > 0)
    def _compute():
        xb = _slab_rows_to_matrix(xs_ref, MOE_BLOCK).astype(BF16)
        g = _dot(xb, wg_ref[0])
        u = _dot(xb, wu_ref[0])
        hid = (g * _sigmoid(g) * u).astype(BF16)
        _matrix_to_slab_rows(ys_ref, _dot(hid, wd_ref[0]), MOE_BLOCK)

    @pl.when(nv == 0)
    def _skip():
        ys_ref[...] = jnp.zeros_like(ys_ref)


def _experts(block_e, n_valid, xs, wg, wu, wd):
    n_blocks = xs.shape[0] // (MOE_BLOCK * SLAB)
    blk = MOE_BLOCK * SLAB
    grid_spec = pltpu.PrefetchScalarGridSpec(
        num_scalar_prefetch=2,
        grid=(n_blocks,),
        in_specs=[pl.BlockSpec((blk, LANES), lambda i, be, nv: (i, 0)),
                  pl.BlockSpec((1, D_MODEL, D_EXPERT), lambda i, be, nv: (be[i], 0, 0)),
                  pl.BlockSpec((1, D_MODEL, D_EXPERT), lambda i, be, nv: (be[i], 0, 0)),
                  pl.BlockSpec((1, D_EXPERT, D_MODEL), lambda i, be, nv: (be[i], 0, 0))],
        out_specs=pl.BlockSpec((blk, LANES), lambda i, be, nv: (i, 0)),
    )
    return pl.pallas_call(
        _expert_kernel,
        grid_spec=grid_spec,
        out_shape=jax.ShapeDtypeStruct(xs.shape, F32),
        compiler_params=pltpu.CompilerParams(dimension_semantics=("arbitrary",),
                                             vmem_limit_bytes=VMEM_LIMIT_BYTES),
        name="experts",
    )(block_e, n_valid, xs, wg, wu, wd)


def _combine_kernel(dest_ref, x_ref, meta_ref, gt_ref, gfin_ref, ys_ref, ys2d_ref, o_ref, buf,
                    sem):
    def slot(k, r):
        return buf.at[k, pl.ds(pl.multiple_of(r * SLAB, SLAB), SLAB)]

    def issue(r, carry):
        for k in range(2):
            _row_copy(ys_ref.at[dest_ref[k, r]], slot(k, r), sem).start(priority=k)
        return carry

    lax.fori_loop(0, TOK_TILE, issue, 0)
    for k in range(2):
        _row_copy(ys2d_ref.at[pl.ds(0, TOK_TILE * SLAB)], buf.at[k], sem).wait()

    meta = meta_ref[...]
    y = (meta[:, 4:5] * _slab_rows_to_matrix(buf.at[0], TOK_TILE)
         + meta[:, 5:6] * _slab_rows_to_matrix(buf.at[1], TOK_TILE))
    x2 = x_ref[...] + _slab_bcast(gt_ref[...], TOK_TILE // BATCH) * y
    o_ref[...] = _rms(x2) * gfin_ref[...]


def _combine(dest, x1, meta, gt2, g_final, ys):
    n = x1.shape[0]
    return pl.pallas_call(
        _combine_kernel,
        grid=(n // TOK_TILE,),
        in_specs=[pl.BlockSpec((2, TOK_TILE), lambda i: (0, i), memory_space=pltpu.SMEM),
                  pl.BlockSpec((TOK_TILE, D_MODEL), lambda i: (i, 0)),
                  pl.BlockSpec((TOK_TILE, LANES), lambda i: (i, 0)),
                  _const_spec(gt2.shape),
                  _const_spec(g_final.shape),
                  pl.BlockSpec(memory_space=pl.ANY),
                  pl.BlockSpec(memory_space=pl.ANY)],
        out_specs=pl.BlockSpec((TOK_TILE, D_MODEL), lambda i: (i, 0)),
        out_shape=jax.ShapeDtypeStruct((n, D_MODEL), F32),
        scratch_shapes=[pltpu.VMEM((2, TOK_TILE * SLAB, LANES), F32),
                        pltpu.SemaphoreType.DMA(())],
        compiler_params=pltpu.CompilerParams(dimension_semantics=("arbitrary",),
                                             vmem_limit_bytes=VMEM_LIMIT_BYTES),
        name="combine",
    )(dest, x1, meta, gt2, g_final, ys.reshape(-1, SLAB, LANES), ys)


def _block_diag(w):
    h, d, _ = w.shape
    eye = jnp.eye(h, dtype=w.dtype)
    return (eye[:, None, :, None] * w[:, :, None, :]).reshape(h * d, h * d)


def kernel(x, c, w_ada, b_ada, g_mix, g_ffn, g_final, w_in, conv_w, conv_b, lru_wr, lru_br,
           lru_wi, lru_bi, lru_lambda, gla_wa2, gla_ba, gla_gnorm, w_out, w_coarse, b_coarse,
           w_fine, b_fine, w_gate, w_up, w_down):
    bsz, seq, d = x.shape
    assert bsz == BATCH and d == D_MODEL and seq % GLA_CHUNK == 0
    assert w_ada.shape[0] == 1, "single-layer problem"
    n = bsz * seq
    row = lambda v: v.reshape(1, -1)

    mod = _ada(c, w_ada[0], b_ada[0])
    sh1, sc1, gt1, sh2, sc2, gt2 = jnp.split(mod, 6, axis=-1)

    xt = x.transpose(1, 0, 2).reshape(n, d)
    w_in_p = jnp.pad(w_in[0], ((0, 0), (0, D_IN_PAD - w_in.shape[2]))).astype(BF16)
    wa2_p = jnp.pad(gla_wa2[0], ((0, LANES - GLA_GATE_RANK), (0, 0))).astype(BF16)
    x1 = _mixer(xt, 1.0 + sc1, sh1, gt1, row(g_mix[0]), w_in_p, conv_w[0], row(conv_b[0]),
                _block_diag(lru_wr[0]).astype(BF16), row(lru_br[0]),
                _block_diag(lru_wi[0]).astype(BF16), row(lru_bi[0]), row(lru_lambda[0]),
                wa2_p, row(gla_ba[0]), row(gla_gnorm[0]), w_out[0].astype(BF16))

    n_route = N_GROUPS + N_EXPERTS
    w_r = jnp.pad(jnp.concatenate([w_coarse[0], w_fine[0]], axis=1),
                  ((0, 0), (0, LANES - n_route))).astype(BF16)
    b_r = jnp.pad(jnp.concatenate([b_coarse[0], b_fine[0]]), (0, LANES - n_route)).reshape(1, LANES)
    h2, meta, meta_t, cnt = _router(x1, 1.0 + sc2, sh2, row(g_ffn[0]), w_r, b_r)

    counts = cnt[0, N_GROUPS:N_GROUPS + N_EXPERTS].astype(jnp.int32)
    padded = (counts + MOE_BLOCK - 1) // MOE_BLOCK * MOE_BLOCK
    pends = jnp.cumsum(padded)
    offs = pends - padded
    cap = (2 * n + MOE_BLOCK - 1) // MOE_BLOCK * MOE_BLOCK + N_EXPERTS * MOE_BLOCK
    n_blocks = cap // MOE_BLOCK
    starts = jnp.arange(n_blocks, dtype=jnp.int32) * MOE_BLOCK
    block_e = jnp.minimum(jnp.sum((pends[None, :] <= starts[:, None]).astype(jnp.int32), axis=1),
                          N_EXPERTS - 1)
    n_valid = jnp.clip(offs[block_e] + counts[block_e] - starts, 0, MOE_BLOCK).astype(jnp.int32)
    dest = _slots(offs.astype(jnp.int32), meta_t)

    xs = _dispatch(n_valid, dest, h2.reshape(n, SLAB, LANES), cap)
    ys = _experts(block_e, n_valid, xs.reshape(cap * SLAB, LANES),
                  w_gate[0].astype(BF16), w_up[0].astype(BF16), w_down[0].astype(BF16))
    out_t = _combine(dest, x1, meta, gt2, row(g_final), ys)
    return out_t.reshape(seq, bsz, d).transpose(1, 0, 2)
```

```python
import functools

import jax
import jax.numpy as jnp
import numpy as np
from jax import lax
from jax.experimental import pallas as pl
from jax.experimental.pallas import tpu as pltpu

F32 = jnp.float32
BF16 = jnp.bfloat16

SUBLANES = 8
LANES = 128
VMEM_LIMIT_BYTES = 56 * 1024 * 1024

D_MODEL = 1024
BATCH = SUBLANES
D_LRU = 512
LRU_HEADS = 8
CONV_WIDTH = 4
LRU_C = 8.0
D_GLA = 512
GLA_HEADS = 4
GLA_DV = 128
GLA_DK = 64
GLA_GATE_RANK = 16
GLA_GATE_NORM = 16.0
GLA_CHUNK = 64
N_GROUPS = 4
EXPERTS_PER_GROUP = 8
N_EXPERTS = 32
D_EXPERT = 512
MOE_BLOCK = 256
EPS = 1e-6

QK = GLA_HEADS * GLA_DK
ROWS = GLA_CHUNK * BATCH
HALO = (CONV_WIDTH - 1) * BATCH
SLAB = D_MODEL // LANES
C_LX, C_LY, C_Q, C_K, C_V, C_G, C_GL = 0, 512, 1024, 1280, 1536, 2048, 2560
D_IN_PAD = 2688
ROUTE_TILE = 512
TOK_TILE = 512
ISSUE_UNROLL = 4


def _dot(a, b):
    return jnp.dot(a, b, preferred_element_type=F32)


def _dot_nt(a, b):
    return lax.dot_general(a, b, (((1,), (1,)), ((), ())), preferred_element_type=F32)


def _softplus(z):
    return jnp.maximum(z, 0.0) + jnp.log1p(jnp.exp(-jnp.abs(z)))


def _sigmoid(z):
    return 1.0 / (1.0 + jnp.exp(-z))


def _rms(x):
    return x * lax.rsqrt(jnp.mean(x * x, axis=-1, keepdims=True) + EPS)


def _slab_bcast(v, n):
    c = v.shape[-1]
    return jnp.broadcast_to(v[None], (n, SUBLANES, c)).reshape(n * SUBLANES, c)


def _ada_kernel(c_ref, w_ref, b_ref, o_ref):
    c = c_ref[...]
    s = c * _sigmoid(c)
    o_ref[...] = jnp.dot(s, w_ref[...], preferred_element_type=F32,
                         precision=lax.Precision.HIGHEST) + b_ref[...]


def _ada(c, w, b):
    n_out = w.shape[1]
    tn = 1024
    return pl.pallas_call(
        _ada_kernel,
        grid=(n_out // tn,),
        in_specs=[pl.BlockSpec((BATCH, D_MODEL), lambda j: (0, 0)),
                  pl.BlockSpec((D_MODEL, tn), lambda j: (0, j)),
                  pl.BlockSpec((1, tn), lambda j: (0, j))],
        out_specs=pl.BlockSpec((BATCH, tn), lambda j: (0, j)),
        out_shape=jax.ShapeDtypeStruct((BATCH, n_out), F32),
        name="ada",
    )(c, w, b.reshape(1, n_out))


def _mixer_kernel(x_ref, sc_ref, sh_ref, gt_ref, gmix_ref, win_ref, cw_ref, cb_ref,
                  wr_ref, br_ref, wi_ref, bi_ref, lam_ref, wa2_ref, ba_ref, gn_ref, wout_ref,
                  o_ref,
                  xbuf, a_s, u_s, hs_s, hc_s, la_s, bc_s, st_s, cm_s):
    i = pl.program_id(0)
    nt = GLA_CHUNK

    @pl.when(i == 0)
    def _init():
        xbuf[0:HALO, :] = jnp.zeros((HALO, D_LRU), F32)
        hc_s[...] = jnp.zeros_like(hc_s)
        st_s[...] = jnp.zeros_like(st_s)
        r = lax.broadcasted_iota(jnp.int32, (ROWS, ROWS), 0)
        c = lax.broadcasted_iota(jnp.int32, (ROWS, ROWS), 1)
        same_seq = (r & (BATCH - 1)) == (c & (BATCH - 1))
        cm_s[...] = jnp.where(same_seq & ((c >> 3) <= (r >> 3)), 1.0, 0.0).astype(F32)

    x = x_ref[...]
    y = _rms(x) * gmix_ref[...]
    h = (y * _slab_bcast(sc_ref[...], nt) + _slab_bcast(sh_ref[...], nt)).astype(BF16)

    def proj(c0, c1):
        return _dot(h, win_ref[:, c0:c1])

    xbuf[HALO:HALO + ROWS, :] = proj(C_LX, C_LY)
    cw = cw_ref[...]
    cx = cb_ref[...] + sum(cw[k:k + 1, :] * xbuf[k * BATCH:k * BATCH + ROWS, :]
                           for k in range(CONV_WIDTH))
    xbuf[0:HALO, :] = xbuf[ROWS:ROWS + HALO, :]
    cxb = cx.astype(BF16)
    r_gate = _sigmoid(_dot(cxb, wr_ref[...]) + br_ref[...])
    i_gate = _sigmoid(_dot(cxb, wi_ref[...]) + bi_ref[...])
    log_a = (-LRU_C) * r_gate * _softplus(-lam_ref[...])
    a_s[...] = jnp.exp(log_a)
    th = jnp.tanh(log_a)
    u_s[...] = jnp.sqrt(-2.0 * th / (1.0 - th)) * (i_gate * cx)
    hcur = hc_s[...]
    for t in range(nt):
        sl = slice(t * BATCH, (t + 1) * BATCH)
        hcur = a_s[sl, :] * hcur + u_s[sl, :]
        hs_s[sl, :] = hcur
    hc_s[...] = hcur
    lru_out = hs_s[...] * jax.nn.gelu(proj(C_LY, C_Q), approximate=True)

    gate_lr = proj(C_GL, D_IN_PAD).astype(BF16)
    z = _dot(gate_lr, wa2_ref[...]) + ba_ref[...]
    la_s[...] = -_softplus(-z) * (1.0 / GLA_GATE_NORM)
    bcur = jnp.zeros((BATCH, QK), F32)
    for t in range(nt):
        sl = slice(t * BATCH, (t + 1) * BATCH)
        bcur = bcur + la_s[sl, :]
        bc_s[sl, :] = bcur
    bc = bc_s[...]
    e_last = jnp.exp(bcur)
    q_dec = proj(C_Q, C_K) * (GLA_DK ** -0.5) * jnp.exp(bc)
    kk = proj(C_K, C_V)
    k_dec = kk * jnp.exp(-bc)
    k_last = kk * jnp.exp(_slab_bcast(bcur, nt) - bc)
    vv = proj(C_V, C_G)
    gg = proj(C_G, C_GL)
    causal = cm_s[...] > 0.5

    lane = lax.broadcasted_iota(jnp.int32, (SUBLANES, LANES), 1)
    sub = lax.broadcasted_iota(jnp.int32, (SUBLANES, LANES), 0)
    half = lane >> 6
    seq_sel = [(sub == 2 * j + half).astype(F32) for j in range(BATCH // 2)]

    def expand(m):
        n = m.shape[0] // SUBLANES
        m3 = m.reshape(n, SUBLANES, LANES)
        return jnp.concatenate([(m3 * s[None]).reshape(m.shape) for s in seq_sel], axis=1)

    def both_halves(m, hh):
        keep = (lax.broadcasted_iota(jnp.int32, m.shape, 1) >> 6) == hh
        mh = jnp.where(keep, m, 0.0)
        return mh, mh + pltpu.roll(mh, GLA_DK, axis=1)

    gla_parts = []
    for hd in range(GLA_HEADS):
        p, hh = hd // 2, hd % 2
        pc = slice(p * LANES, (p + 1) * LANES)
        qh, q_both = both_halves(q_dec[:, pc], hh)
        _, k_both = both_halves(k_last[:, pc], hh)
        _, e_both = both_halves(e_last[:, pc], hh)
        v_h = vv[:, hd * GLA_DV:(hd + 1) * GLA_DV]
        v_hb = v_h.astype(BF16)
        scores = _dot_nt(qh.astype(BF16), k_dec[:, pc].astype(BF16))
        scores = jnp.where(causal, scores, 0.0)
        o_h = _dot(scores.astype(BF16), v_hb)
        st = st_s[hd]
        o_h = o_h + _dot_nt(expand(q_both).astype(BF16), st.astype(BF16))
        kv_t = _dot(v_h.T.astype(BF16), expand(k_both).astype(BF16))
        decay = jnp.concatenate(
            [jnp.sum(e_both * s, axis=0, keepdims=True) for s in seq_sel], axis=1)
        st_s[hd] = st * decay + kv_t
        o_n = _rms(o_h) * gn_ref[...]
        g_h = gg[:, hd * GLA_DV:(hd + 1) * GLA_DV]
        gla_parts.append(o_n * (g_h * _sigmoid(g_h)))

    mix_in = jnp.concatenate([lru_out] + gla_parts, axis=1).astype(BF16)
    mix = _dot(mix_in, wout_ref[...])
    o_ref[...] = x + _slab_bcast(gt_ref[...], nt) * mix


def _const_spec(shape):
    nd = len(shape)
    return pl.BlockSpec(shape, lambda i: (0,) * nd)


def _mixer(xt, sc1p, sh1, gt1, g_mix, w_in_p, conv_w, conv_b, wr_d, br, wi_d, bi, lam,
           wa2_p, ba, gn, w_out_b):
    n = xt.shape[0]
    consts = [sc1p, sh1, gt1, g_mix, w_in_p, conv_w, conv_b, wr_d, br, wi_d, bi, lam,
              wa2_p, ba, gn, w_out_b]
    return pl.pallas_call(
        _mixer_kernel,
        grid=(n // ROWS,),
        in_specs=[pl.BlockSpec((ROWS, D_MODEL), lambda i: (i, 0))]
        + [_const_spec(a.shape) for a in consts],
        out_specs=pl.BlockSpec((ROWS, D_MODEL), lambda i: (i, 0)),
        out_shape=jax.ShapeDtypeStruct((n, D_MODEL), F32),
        scratch_shapes=[
            pltpu.VMEM((HALO + ROWS, D_LRU), F32),
            pltpu.VMEM((ROWS, D_LRU), F32),
            pltpu.VMEM((ROWS, D_LRU), F32),
            pltpu.VMEM((ROWS, D_LRU), F32),
            pltpu.VMEM((BATCH, D_LRU), F32),
            pltpu.VMEM((ROWS, QK), F32),
            pltpu.VMEM((ROWS, QK), F32),
            pltpu.VMEM((GLA_HEADS, GLA_DV, BATCH * GLA_DK), F32),
            pltpu.VMEM((ROWS, ROWS), F32),
        ],
        compiler_params=pltpu.CompilerParams(dimension_semantics=("arbitrary",),
                                             vmem_limit_bytes=VMEM_LIMIT_BYTES),
        name="mixer",
    )(xt, *consts)


def _router_kernel(x_ref, sc_ref, sh_ref, gffn_ref, wr_ref, br_ref,
                   h2_ref, meta_ref, metat_ref, cnt_ref, base_s, tri_s):
    i = pl.program_id(0)
    nt = ROUTE_TILE // BATCH

    @pl.when(i == 0)
    def _init():
        base_s[...] = jnp.zeros_like(base_s)
        r = lax.broadcasted_iota(jnp.int32, (ROUTE_TILE, ROUTE_TILE), 0)
        c = lax.broadcasted_iota(jnp.int32, (ROUTE_TILE, ROUTE_TILE), 1)
        tri_s[...] = jnp.where(c < r, 1.0, 0.0).astype(BF16)

    x = x_ref[...]
    h2 = _rms(x) * gffn_ref[...] * _slab_bcast(sc_ref[...], nt) + _slab_bcast(sh_ref[...], nt)
    h2_ref[...] = h2
    logits = _dot(h2.astype(BF16), wr_ref[...]) + br_ref[...]
    lane = lax.broadcasted_iota(jnp.int32, logits.shape, 1)
    neg = jnp.float32(-jnp.inf)

    def first_max(vals):
        m = jnp.max(vals, axis=-1, keepdims=True)
        idx = jnp.min(jnp.where(vals == m, lane, LANES), axis=-1, keepdims=True)
        return m, idx

    cl = jnp.where(lane < N_GROUPS, logits, neg)
    cmax, grp = first_max(cl)
    p_grp = 1.0 / jnp.sum(jnp.exp(cl - cmax), axis=-1, keepdims=True)
    lo = N_GROUPS + EXPERTS_PER_GROUP * grp
    fl = jnp.where((lane >= lo) & (lane < lo + EXPERTS_PER_GROUP), logits, neg)
    f1, i1 = first_max(fl)
    f2, i2 = first_max(jnp.where(lane == i1, neg, fl))
    z = jnp.sum(jnp.exp(fl - f1), axis=-1, keepdims=True)
    tp1 = 1.0 / z
    tp2 = jnp.exp(f2 - f1) / z
    w1 = p_grp * tp1 / (tp1 + tp2)
    w2 = p_grp * tp2 / (tp1 + tp2)

    hit1 = lane == i1
    hit2 = lane == i2
    assign = jnp.where(hit1 | hit2, 1.0, 0.0)
    before = _dot(tri_s[...], assign.astype(BF16)) + base_s[...]
    rank1 = jnp.sum(jnp.where(hit1, before, 0.0), axis=-1, keepdims=True)
    rank2 = jnp.sum(jnp.where(hit2, before, 0.0), axis=-1, keepdims=True)
    base_s[...] = base_s[...] + jnp.sum(assign, axis=0, keepdims=True)
    cnt_ref[...] = jnp.broadcast_to(base_s[...], cnt_ref.shape)

    cols = [(i1 - N_GROUPS).astype(F32), (i2 - N_GROUPS).astype(F32), rank1, rank2, w1, w2]
    meta = jnp.zeros(logits.shape, F32)
    for j, col in enumerate(cols):
        meta = jnp.where(lane == j, col, meta)
    meta_ref[...] = meta
    metat_ref[...] = meta.T[0:SUBLANES, :]


def _router(x1, sc2p, sh2, g_ffn, w_r, b_r):
    n = x1.shape[0]
    consts = [sc2p, sh2, g_ffn, w_r, b_r]
    return pl.pallas_call(
        _router_kernel,
        grid=(n // ROUTE_TILE,),
        in_specs=[pl.BlockSpec((ROUTE_TILE, D_MODEL), lambda i: (i, 0))]
        + [_const_spec(a.shape) for a in consts],
        out_specs=[pl.BlockSpec((ROUTE_TILE, D_MODEL), lambda i: (i, 0)),
                   pl.BlockSpec((ROUTE_TILE, LANES), lambda i: (i, 0)),
                   pl.BlockSpec((SUBLANES, ROUTE_TILE), lambda i: (0, i)),
                   pl.BlockSpec((SUBLANES, LANES), lambda i: (0, 0))],
        out_shape=[jax.ShapeDtypeStruct((n, D_MODEL), F32),
                   jax.ShapeDtypeStruct((n, LANES), F32),
                   jax.ShapeDtypeStruct((SUBLANES, n), F32),
                   jax.ShapeDtypeStruct((SUBLANES, LANES), F32)],
        scratch_shapes=[pltpu.VMEM((1, LANES), F32),
                        pltpu.VMEM((ROUTE_TILE, ROUTE_TILE), BF16)],
        compiler_params=pltpu.CompilerParams(dimension_semantics=("arbitrary",),
                                             vmem_limit_bytes=VMEM_LIMIT_BYTES),
        name="router",
    )(x1, *consts)


def _slots_kernel(offs_ref, mt_ref, dest_ref):
    mt = mt_ref[...]
    e = mt[0:2, :].astype(jnp.int32)
    acc = mt[2:4, :].astype(jnp.int32)
    for j in range(N_EXPERTS):
        acc = acc + jnp.where(e == j, offs_ref[j], 0)
    dest_ref[...] = acc


def _slots(offs, meta_t):
    n = meta_t.shape[1]
    tl = min(n, 8192)
    grid_spec = pltpu.PrefetchScalarGridSpec(
        num_scalar_prefetch=1,
        grid=(n // tl,),
        in_specs=[pl.BlockSpec((SUBLANES, tl), lambda i, offs: (0, i))],
        out_specs=pl.BlockSpec((2, tl), lambda i, offs: (0, i)),
    )
    return pl.pallas_call(
        _slots_kernel,
        grid_spec=grid_spec,
        out_shape=jax.ShapeDtypeStruct((2, n), jnp.int32),
        name="slots",
    )(offs, meta_t)


def _row_copy(src, dst, sem):
    return pltpu.make_async_copy(src, dst, sem)


def _dispatch_kernel(nv_ref, dest_ref, h_ref, xs_ref, zbuf, hbuf, zsem, lsem, sem):
    n_blocks = xs_ref.shape[0] // MOE_BLOCK

    @pl.when(pl.program_id(0) == 0)
    def _zero_partial_blocks():
        zbuf[...] = jnp.zeros_like(zbuf)

        def fill(b):
            return _row_copy(zbuf, xs_ref.at[pl.ds(pl.multiple_of(b * MOE_BLOCK, MOE_BLOCK),
                                                   MOE_BLOCK)], zsem)

        def start(b, carry):
            @pl.when(nv_ref[b] < MOE_BLOCK)
            def _():
                fill(b).start()
            return carry

        def wait(b, carry):
            @pl.when(nv_ref[b] < MOE_BLOCK)
            def _():
                fill(b).wait()
            return carry

        lax.fori_loop(0, n_blocks, start, 0)
        lax.fori_loop(0, n_blocks, wait, 0)

    i = pl.program_id(0)
    n_steps = pl.num_programs(0)
    cur = lax.rem(i, 2)

    def load(step, slot):
        rows = pl.ds(pl.multiple_of(step * TOK_TILE, TOK_TILE), TOK_TILE)
        return _row_copy(h_ref.at[rows], hbuf.at[slot], lsem.at[slot])

    def wait_rows(slot):
        for k in range(2):
            _row_copy(hbuf.at[slot], xs_ref.at[pl.ds(0, TOK_TILE)], sem.at[slot]).wait()

    @pl.when(i == 0)
    def _first_load():
        load(0, 0).start()

    @pl.when(i > 0)
    def _drain_previous():
        wait_rows(1 - cur)

    @pl.when(i + 1 < n_steps)
    def _prefetch():
        load(i + 1, 1 - cur).start()

    load(i, cur).wait()

    def issue(r4, carry):
        for u in range(ISSUE_UNROLL):
            r = r4 * ISSUE_UNROLL + u
            for k in range(2):
                _row_copy(hbuf.at[cur, r], xs_ref.at[dest_ref[k, r]],
                          sem.at[cur]).start(priority=k)
        return carry

    lax.fori_loop(0, TOK_TILE // ISSUE_UNROLL, issue, 0)

    @pl.when(i == n_steps - 1)
    def _drain_last():
        wait_rows(cur)


def _dispatch(n_valid, dest, h2s, cap):
    n = h2s.shape[0]
    grid_spec = pltpu.PrefetchScalarGridSpec(
        num_scalar_prefetch=1,
        grid=(n // TOK_TILE,),
        in_specs=[pl.BlockSpec((2, TOK_TILE), lambda i, nv: (0, i), memory_space=pltpu.SMEM),
                  pl.BlockSpec(memory_space=pl.ANY)],
        out_specs=pl.BlockSpec(memory_space=pl.ANY),
        scratch_shapes=[pltpu.VMEM((MOE_BLOCK, SLAB, LANES), F32),
                        pltpu.VMEM((2, TOK_TILE, SLAB, LANES), F32),
                        pltpu.SemaphoreType.DMA(()),
                        pltpu.SemaphoreType.DMA((2,)),
                        pltpu.SemaphoreType.DMA((2,))],
    )
    return pl.pallas_call(
        _dispatch_kernel,
        grid_spec=grid_spec,
        out_shape=jax.ShapeDtypeStruct((cap, SLAB, LANES), F32),
        compiler_params=pltpu.CompilerParams(dimension_semantics=("arbitrary",)),
        name="dispatch",
    )(n_valid, dest, h2s)


def _slab_rows_to_matrix(ref, rows):
    return jnp.concatenate([ref[pl.ds(s, rows, stride=SLAB), :] for s in range(SLAB)], axis=1)


def _matrix_to_slab_rows(ref, val, rows):
    for s in range(SLAB):
        ref[pl.ds(s, rows, stride=SLAB), :] = val[:, s * LANES:(s + 1) * LANES]


def _expert_kernel(be_ref, nv_ref, xs_ref, wg_ref, wu_ref, wd_ref, ys_ref):
    i = pl.program_id(0)
    nv = nv_ref[i]

    @pl.when(nv > 0)
    def _compute():
        xb = _slab_rows_to_matrix(xs_ref, MOE_BLOCK).astype(BF16)
        g = _dot(xb, wg_ref[0])
        u = _dot(xb, wu_ref[0])
        hid = (g * _sigmoid(g) * u).astype(BF16)
        _matrix_to_slab_rows(ys_ref, _dot(hid, wd_ref[0]), MOE_BLOCK)

    @pl.when(nv == 0)
    def _skip():
        ys_ref[...] = jnp.zeros_like(ys_ref)


def _experts(block_e, n_valid, xs, wg, wu, wd):
    n_blocks = xs.shape[0] // (MOE_BLOCK * SLAB)
    blk = MOE_BLOCK * SLAB
    grid_spec = pltpu.PrefetchScalarGridSpec(
        num_scalar_prefetch=2,
        grid=(n_blocks,),
        in_specs=[pl.BlockSpec((blk, LANES), lambda i, be, nv: (i, 0)),
                  pl.BlockSpec((1, D_MODEL, D_EXPERT), lambda i, be, nv: (be[i], 0, 0)),
                  pl.BlockSpec((1, D_MODEL, D_EXPERT), lambda i, be, nv: (be[i], 0, 0)),
                  pl.BlockSpec((1, D_EXPERT, D_MODEL), lambda i, be, nv: (be[i], 0, 0))],
        out_specs=pl.BlockSpec((blk, LANES), lambda i, be, nv: (i, 0)),
    )
    return pl.pallas_call(
        _expert_kernel,
        grid_spec=grid_spec,
        out_shape=jax.ShapeDtypeStruct(xs.shape, F32),
        compiler_params=pltpu.CompilerParams(dimension_semantics=("arbitrary",),
                                             vmem_limit_bytes=VMEM_LIMIT_BYTES),
        name="experts",
    )(block_e, n_valid, xs, wg, wu, wd)


def _combine_kernel(dcur_ref, dnxt_ref, x_ref, meta_ref, gt_ref, gfin_ref, ys_ref, ys2d_ref,
                    o_ref, buf, sem):
    i = pl.program_id(0)
    n_steps = pl.num_programs(0)
    cur = lax.rem(i, 2)

    def gather(dest_ref, slot):
        def issue(r4, carry):
            for u in range(ISSUE_UNROLL):
                r = r4 * ISSUE_UNROLL + u
                for k in range(2):
                    dst = buf.at[slot, k, pl.ds(pl.multiple_of(r * SLAB, SLAB), SLAB)]
                    _row_copy(ys_ref.at[dest_ref[k, r]], dst, sem.at[slot]).start(priority=k)
            return carry

        lax.fori_loop(0, TOK_TILE // ISSUE_UNROLL, issue, 0)

    @pl.when(i == 0)
    def _first_gather():
        gather(dcur_ref, 0)

    @pl.when(i + 1 < n_steps)
    def _prefetch():
        gather(dnxt_ref, 1 - cur)

    for k in range(2):
        _row_copy(ys2d_ref.at[pl.ds(0, TOK_TILE * SLAB)], buf.at[cur, k], sem.at[cur]).wait()

    meta = meta_ref[...]
    y = (meta[:, 4:5] * _slab_rows_to_matrix(buf.at[cur, 0], TOK_TILE)
         + meta[:, 5:6] * _slab_rows_to_matrix(buf.at[cur, 1], TOK_TILE))
    x2 = x_ref[...] + _slab_bcast(gt_ref[...], TOK_TILE // BATCH) * y
    o_ref[...] = _rms(x2) * gfin_ref[...]


def _combine(dest, x1, meta, gt2, g_final, ys):
    n = x1.shape[0]
    last = n // TOK_TILE - 1
    return pl.pallas_call(
        _combine_kernel,
        grid=(n // TOK_TILE,),
        in_specs=[pl.BlockSpec((2, TOK_TILE), lambda i: (0, i), memory_space=pltpu.SMEM),
                  pl.BlockSpec((2, TOK_TILE), lambda i: (0, jnp.minimum(i + 1, last)),
                               memory_space=pltpu.SMEM),
                  pl.BlockSpec((TOK_TILE, D_MODEL), lambda i: (i, 0)),
                  pl.BlockSpec((TOK_TILE, LANES), lambda i: (i, 0)),
                  _const_spec(gt2.shape),
                  _const_spec(g_final.shape),
                  pl.BlockSpec(memory_space=pl.ANY),
                  pl.BlockSpec(memory_space=pl.ANY)],
        out_specs=pl.BlockSpec((TOK_TILE, D_MODEL), lambda i: (i, 0)),
        out_shape=jax.ShapeDtypeStruct((n, D_MODEL), F32),
        scratch_shapes=[pltpu.VMEM((2, 2, TOK_TILE * SLAB, LANES), F32),
                        pltpu.SemaphoreType.DMA((2,))],
        compiler_params=pltpu.CompilerParams(dimension_semantics=("arbitrary",),
                                             vmem_limit_bytes=VMEM_LIMIT_BYTES),
        name="combine",
    )(dest, dest, x1, meta, gt2, g_final, ys.reshape(-1, SLAB, LANES), ys)


def _block_diag(w):
    h, d, _ = w.shape
    eye = jnp.eye(h, dtype=w.dtype)
    return (eye[:, None, :, None] * w[:, :, None, :]).reshape(h * d, h * d)


def kernel(x, c, w_ada, b_ada, g_mix, g_ffn, g_final, w_in, conv_w, conv_b, lru_wr, lru_br,
           lru_wi, lru_bi, lru_lambda, gla_wa2, gla_ba, gla_gnorm, w_out, w_coarse, b_coarse,
           w_fine, b_fine, w_gate, w_up, w_down):
    bsz, seq, d = x.shape
    assert bsz == BATCH and d == D_MODEL and seq % GLA_CHUNK == 0
    assert w_ada.shape[0] == 1, "single-layer problem"
    n = bsz * seq
    row = lambda v: v.reshape(1, -1)

    mod = _ada(c, w_ada[0], b_ada[0])
    sh1, sc1, gt1, sh2, sc2, gt2 = jnp.split(mod, 6, axis=-1)

    xt = x.transpose(1, 0, 2).reshape(n, d)
    w_in_p = jnp.pad(w_in[0], ((0, 0), (0, D_IN_PAD - w_in.shape[2]))).astype(BF16)
    wa2_p = jnp.pad(gla_wa2[0], ((0, LANES - GLA_GATE_RANK), (0, 0))).astype(BF16)
    x1 = _mixer(xt, 1.0 + sc1, sh1, gt1, row(g_mix[0]), w_in_p, conv_w[0], row(conv_b[0]),
                _block_diag(lru_wr[0]).astype(BF16), row(lru_br[0]),
                _block_diag(lru_wi[0]).astype(BF16), row(lru_bi[0]), row(lru_lambda[0]),
                wa2_p, row(gla_ba[0]), row(gla_gnorm[0]), w_out[0].astype(BF16))

    n_route = N_GROUPS + N_EXPERTS
    w_r = jnp.pad(jnp.concatenate([w_coarse[0], w_fine[0]], axis=1),
                  ((0, 0), (0, LANES - n_route))).astype(BF16)
    b_r = jnp.pad(jnp.concatenate([b_coarse[0], b_fine[0]]), (0, LANES - n_route)).reshape(1, LANES)
    h2, meta, meta_t, cnt = _router(x1, 1.0 + sc2, sh2, row(g_ffn[0]), w_r, b_r)

    counts = cnt[0, N_GROUPS:N_GROUPS + N_EXPERTS].astype(jnp.int32)
    padded = (counts + MOE_BLOCK - 1) // MOE_BLOCK * MOE_BLOCK
    pends = jnp.cumsum(padded)
    offs = pends - padded
    cap = (2 * n + MOE_BLOCK - 1) // MOE_BLOCK * MOE_BLOCK + N_EXPERTS * MOE_BLOCK
    n_blocks = cap // MOE_BLOCK
    starts = jnp.arange(n_blocks, dtype=jnp.int32) * MOE_BLOCK
    block_e = jnp.minimum(jnp.sum((pends[None, :] <= starts[:, None]).astype(jnp.int32), axis=1),
                          N_EXPERTS - 1)
    n_valid = jnp.clip(offs[block_e] + counts[block_e] - starts, 0, MOE_BLOCK).astype(jnp.int32)
    dest = _slots(offs.astype(jnp.int32), meta_t)

    xs = _dispatch(n_valid, dest, h2.reshape(n, SLAB, LANES), cap)
    ys = _experts(block_e, n_valid, xs.reshape(cap * SLAB, LANES),
                  w_gate[0].astype(BF16), w_up[0].astype(BF16), w_down[0].astype(BF16))
    out_t = _combine(dest, x1, meta, gt2, row(g_final), ys)
    return out_t.reshape(seq, bsz, d).transpose(1, 0, 2)
```

```python
import functools

import jax
import jax.numpy as jnp
import numpy as np
from jax import lax
from jax.experimental import pallas as pl
from jax.experimental.pallas import tpu as pltpu

F32 = jnp.float32
BF16 = jnp.bfloat16

SUBLANES = 8
LANES = 128
VMEM_LIMIT_BYTES = 56 * 1024 * 1024

D_MODEL = 1024
BATCH = SUBLANES
D_LRU = 512
LRU_HEADS = 8
CONV_WIDTH = 4
LRU_C = 8.0
D_GLA = 512
GLA_HEADS = 4
GLA_DV = 128
GLA_DK = 64
GLA_GATE_RANK = 16
GLA_GATE_NORM = 16.0
GLA_CHUNK = 64
N_GROUPS = 4
EXPERTS_PER_GROUP = 8
N_EXPERTS = 32
D_EXPERT = 512
MOE_BLOCK = 512
EPS = 1e-6

QK = GLA_HEADS * GLA_DK
ROWS = GLA_CHUNK * BATCH
HALO = (CONV_WIDTH - 1) * BATCH
SLAB = D_MODEL // LANES
C_LX, C_LY, C_Q, C_K, C_V, C_G, C_GL = 0, 512, 1024, 1280, 1536, 2048, 2560
D_IN_PAD = 2688
ROUTE_TILE = 512
TOK_TILE = 512
ISSUE_UNROLL = 4


def _dot(a, b):
    return jnp.dot(a, b, preferred_element_type=F32)


def _dot_nt(a, b):
    return lax.dot_general(a, b, (((1,), (1,)), ((), ())), preferred_element_type=F32)


def _softplus(z):
    return jnp.maximum(z, 0.0) + jnp.log1p(jnp.exp(-jnp.abs(z)))


def _sigmoid(z):
    return 1.0 / (1.0 + jnp.exp(-z))


def _rms(x):
    return x * lax.rsqrt(jnp.mean(x * x, axis=-1, keepdims=True) + EPS)


def _slab_bcast(v, n):
    c = v.shape[-1]
    return jnp.broadcast_to(v[None], (n, SUBLANES, c)).reshape(n * SUBLANES, c)


def _ada_kernel(c_ref, w_ref, b_ref, o_ref):
    c = c_ref[...]
    s = c * _sigmoid(c)
    o_ref[...] = jnp.dot(s, w_ref[...], preferred_element_type=F32,
                         precision=lax.Precision.HIGHEST) + b_ref[...]


def _ada(c, w, b):
    n_out = w.shape[1]
    tn = 1024
    return pl.pallas_call(
        _ada_kernel,
        grid=(n_out // tn,),
        in_specs=[pl.BlockSpec((BATCH, D_MODEL), lambda j: (0, 0)),
                  pl.BlockSpec((D_MODEL, tn), lambda j: (0, j)),
                  pl.BlockSpec((1, tn), lambda j: (0, j))],
        out_specs=pl.BlockSpec((BATCH, tn), lambda j: (0, j)),
        out_shape=jax.ShapeDtypeStruct((BATCH, n_out), F32),
        name="ada",
    )(c, w, b.reshape(1, n_out))


def _mixer_kernel(x_ref, sc_ref, sh_ref, gt_ref, gmix_ref, win_ref, cw_ref, cb_ref,
                  wr_ref, br_ref, wi_ref, bi_ref, lam_ref, wa2_ref, ba_ref, gn_ref, wout_ref,
                  o_ref,
                  xbuf, a_s, u_s, hs_s, hc_s, la_s, bc_s, st_s, cm_s):
    i = pl.program_id(0)
    nt = GLA_CHUNK

    @pl.when(i == 0)
    def _init():
        xbuf[0:HALO, :] = jnp.zeros((HALO, D_LRU), F32)
        hc_s[...] = jnp.zeros_like(hc_s)
        st_s[...] = jnp.zeros_like(st_s)
        r = lax.broadcasted_iota(jnp.int32, (ROWS, ROWS), 0)
        c = lax.broadcasted_iota(jnp.int32, (ROWS, ROWS), 1)
        same_seq = (r & (BATCH - 1)) == (c & (BATCH - 1))
        cm_s[...] = jnp.where(same_seq & ((c >> 3) <= (r >> 3)), 1.0, 0.0).astype(F32)

    x = x_ref[...]
    y = _rms(x) * gmix_ref[...]
    h = (y * _slab_bcast(sc_ref[...], nt) + _slab_bcast(sh_ref[...], nt)).astype(BF16)

    def proj(c0, c1):
        return _dot(h, win_ref[:, c0:c1])

    xbuf[HALO:HALO + ROWS, :] = proj(C_LX, C_LY)
    cw = cw_ref[...]
    cx = cb_ref[...] + sum(cw[k:k + 1, :] * xbuf[k * BATCH:k * BATCH + ROWS, :]
                           for k in range(CONV_WIDTH))
    xbuf[0:HALO, :] = xbuf[ROWS:ROWS + HALO, :]
    cxb = cx.astype(BF16)
    r_gate = _sigmoid(_dot(cxb, wr_ref[...]) + br_ref[...])
    i_gate = _sigmoid(_dot(cxb, wi_ref[...]) + bi_ref[...])
    log_a = (-LRU_C) * r_gate * _softplus(-lam_ref[...])
    a_s[...] = jnp.exp(log_a)
    th = jnp.tanh(log_a)
    u_s[...] = jnp.sqrt(-2.0 * th / (1.0 - th)) * (i_gate * cx)
    hcur = hc_s[...]
    for t in range(nt):
        sl = slice(t * BATCH, (t + 1) * BATCH)
        hcur = a_s[sl, :] * hcur + u_s[sl, :]
        hs_s[sl, :] = hcur
    hc_s[...] = hcur
    lru_out = hs_s[...] * jax.nn.gelu(proj(C_LY, C_Q), approximate=True)

    gate_lr = proj(C_GL, D_IN_PAD).astype(BF16)
    z = _dot(gate_lr, wa2_ref[...]) + ba_ref[...]
    la_s[...] = -_softplus(-z) * (1.0 / GLA_GATE_NORM)
    bcur = jnp.zeros((BATCH, QK), F32)
    for t in range(nt):
        sl = slice(t * BATCH, (t + 1) * BATCH)
        bcur = bcur + la_s[sl, :]
        bc_s[sl, :] = bcur
    bc = bc_s[...]
    e_last = jnp.exp(bcur)
    q_dec = proj(C_Q, C_K) * (GLA_DK ** -0.5) * jnp.exp(bc)
    kk = proj(C_K, C_V)
    k_dec = kk * jnp.exp(-bc)
    k_last = kk * jnp.exp(_slab_bcast(bcur, nt) - bc)
    vv = proj(C_V, C_G)
    gg = proj(C_G, C_GL)
    causal = cm_s[...] > 0.5

    lane = lax.broadcasted_iota(jnp.int32, (SUBLANES, LANES), 1)
    sub = lax.broadcasted_iota(jnp.int32, (SUBLANES, LANES), 0)
    half = lane >> 6
    seq_sel = [(sub == 2 * j + half).astype(F32) for j in range(BATCH // 2)]

    def expand(m):
        n = m.shape[0] // SUBLANES
        m3 = m.reshape(n, SUBLANES, LANES)
        return jnp.concatenate([(m3 * s[None]).reshape(m.shape) for s in seq_sel], axis=1)

    def both_halves(m, hh):
        keep = (lax.broadcasted_iota(jnp.int32, m.shape, 1) >> 6) == hh
        mh = jnp.where(keep, m, 0.0)
        return mh, mh + pltpu.roll(mh, GLA_DK, axis=1)

    gla_parts = []
    for hd in range(GLA_HEADS):
        p, hh = hd // 2, hd % 2
        pc = slice(p * LANES, (p + 1) * LANES)
        qh, q_both = both_halves(q_dec[:, pc], hh)
        _, k_both = both_halves(k_last[:, pc], hh)
        _, e_both = both_halves(e_last[:, pc], hh)
        v_h = vv[:, hd * GLA_DV:(hd + 1) * GLA_DV]
        v_hb = v_h.astype(BF16)
        scores = _dot_nt(qh.astype(BF16), k_dec[:, pc].astype(BF16))
        scores = jnp.where(causal, scores, 0.0)
        o_h = _dot(scores.astype(BF16), v_hb)
        st = st_s[hd]
        o_h = o_h + _dot_nt(expand(q_both).astype(BF16), st.astype(BF16))
        kv_t = _dot(v_h.T.astype(BF16), expand(k_both).astype(BF16))
        decay = jnp.concatenate(
            [jnp.sum(e_both * s, axis=0, keepdims=True) for s in seq_sel], axis=1)
        st_s[hd] = st * decay + kv_t
        o_n = _rms(o_h) * gn_ref[...]
        g_h = gg[:, hd * GLA_DV:(hd + 1) * GLA_DV]
        gla_parts.append(o_n * (g_h * _sigmoid(g_h)))

    mix_in = jnp.concatenate([lru_out] + gla_parts, axis=1).astype(BF16)
    mix = _dot(mix_in, wout_ref[...])
    o_ref[...] = x + _slab_bcast(gt_ref[...], nt) * mix


def _const_spec(shape):
    nd = len(shape)
    return pl.BlockSpec(shape, lambda i: (0,) * nd)


def _mixer(xt, sc1p, sh1, gt1, g_mix, w_in_p, conv_w, conv_b, wr_d, br, wi_d, bi, lam,
           wa2_p, ba, gn, w_out_b):
    n = xt.shape[0]
    consts = [sc1p, sh1, gt1, g_mix, w_in_p, conv_w, conv_b, wr_d, br, wi_d, bi, lam,
              wa2_p, ba, gn, w_out_b]
    return pl.pallas_call(
        _mixer_kernel,
        grid=(n // ROWS,),
        in_specs=[pl.BlockSpec((ROWS, D_MODEL), lambda i: (i, 0))]
        + [_const_spec(a.shape) for a in consts],
        out_specs=pl.BlockSpec((ROWS, D_MODEL), lambda i: (i, 0)),
        out_shape=jax.ShapeDtypeStruct((n, D_MODEL), F32),
        scratch_shapes=[
            pltpu.VMEM((HALO + ROWS, D_LRU), F32),
            pltpu.VMEM((ROWS, D_LRU), F32),
            pltpu.VMEM((ROWS, D_LRU), F32),
            pltpu.VMEM((ROWS, D_LRU), F32),
            pltpu.VMEM((BATCH, D_LRU), F32),
            pltpu.VMEM((ROWS, QK), F32),
            pltpu.VMEM((ROWS, QK), F32),
            pltpu.VMEM((GLA_HEADS, GLA_DV, BATCH * GLA_DK), F32),
            pltpu.VMEM((ROWS, ROWS), F32),
        ],
        compiler_params=pltpu.CompilerParams(dimension_semantics=("arbitrary",),
                                             vmem_limit_bytes=VMEM_LIMIT_BYTES),
        name="mixer",
    )(xt, *consts)


def _router_kernel(x_ref, sc_ref, sh_ref, gffn_ref, wr_ref, br_ref,
                   h2_ref, meta_ref, metat_ref, cnt_ref, base_s, tri_s):
    i = pl.program_id(0)
    nt = ROUTE_TILE // BATCH

    @pl.when(i == 0)
    def _init():
        base_s[...] = jnp.zeros_like(base_s)
        r = lax.broadcasted_iota(jnp.int32, (ROUTE_TILE, ROUTE_TILE), 0)
        c = lax.broadcasted_iota(jnp.int32, (ROUTE_TILE, ROUTE_TILE), 1)
        tri_s[...] = jnp.where(c < r, 1.0, 0.0).astype(BF16)

    x = x_ref[...]
    h2 = _rms(x) * gffn_ref[...] * _slab_bcast(sc_ref[...], nt) + _slab_bcast(sh_ref[...], nt)
    h2_ref[...] = h2
    logits = _dot(h2.astype(BF16), wr_ref[...]) + br_ref[...]
    lane = lax.broadcasted_iota(jnp.int32, logits.shape, 1)
    neg = jnp.float32(-jnp.inf)

    def first_max(vals):
        m = jnp.max(vals, axis=-1, keepdims=True)
        idx = jnp.min(jnp.where(vals == m, lane, LANES), axis=-1, keepdims=True)
        return m, idx

    cl = jnp.where(lane < N_GROUPS, logits, neg)
    cmax, grp = first_max(cl)
    p_grp = 1.0 / jnp.sum(jnp.exp(cl - cmax), axis=-1, keepdims=True)
    lo = N_GROUPS + EXPERTS_PER_GROUP * grp
    fl = jnp.where((lane >= lo) & (lane < lo + EXPERTS_PER_GROUP), logits, neg)
    f1, i1 = first_max(fl)
    f2, i2 = first_max(jnp.where(lane == i1, neg, fl))
    z = jnp.sum(jnp.exp(fl - f1), axis=-1, keepdims=True)
    tp1 = 1.0 / z
    tp2 = jnp.exp(f2 - f1) / z
    w1 = p_grp * tp1 / (tp1 + tp2)
    w2 = p_grp * tp2 / (tp1 + tp2)

    hit1 = lane == i1
    hit2 = lane == i2
    assign = jnp.where(hit1 | hit2, 1.0, 0.0)
    before = _dot(tri_s[...], assign.astype(BF16)) + base_s[...]
    rank1 = jnp.sum(jnp.where(hit1, before, 0.0), axis=-1, keepdims=True)
    rank2 = jnp.sum(jnp.where(hit2, before, 0.0), axis=-1, keepdims=True)
    base_s[...] = base_s[...] + jnp.sum(assign, axis=0, keepdims=True)
    cnt_ref[...] = jnp.broadcast_to(base_s[...], cnt_ref.shape)

    cols = [(i1 - N_GROUPS).astype(F32), (i2 - N_GROUPS).astype(F32), rank1, rank2, w1, w2]
    meta = jnp.zeros(logits.shape, F32)
    for j, col in enumerate(cols):
        meta = jnp.where(lane == j, col, meta)
    meta_ref[...] = meta
    metat_ref[...] = meta.T[0:SUBLANES, :]


def _router(x1, sc2p, sh2, g_ffn, w_r, b_r):
    n = x1.shape[0]
    consts = [sc2p, sh2, g_ffn, w_r, b_r]
    return pl.pallas_call(
        _router_kernel,
        grid=(n // ROUTE_TILE,),
        in_specs=[pl.BlockSpec((ROUTE_TILE, D_MODEL), lambda i: (i, 0))]
        + [_const_spec(a.shape) for a in consts],
        out_specs=[pl.BlockSpec((ROUTE_TILE, D_MODEL), lambda i: (i, 0)),
                   pl.BlockSpec((ROUTE_TILE, LANES), lambda i: (i, 0)),
                   pl.BlockSpec((SUBLANES, ROUTE_TILE), lambda i: (0, i)),
                   pl.BlockSpec((SUBLANES, LANES), lambda i: (0, 0))],
        out_shape=[jax.ShapeDtypeStruct((n, D_MODEL), F32),
                   jax.ShapeDtypeStruct((n, LANES), F32),
                   jax.ShapeDtypeStruct((SUBLANES, n), F32),
                   jax.ShapeDtypeStruct((SUBLANES, LANES), F32)],
        scratch_shapes=[pltpu.VMEM((1, LANES), F32),
                        pltpu.VMEM((ROUTE_TILE, ROUTE_TILE), BF16)],
        compiler_params=pltpu.CompilerParams(dimension_semantics=("arbitrary",),
                                             vmem_limit_bytes=VMEM_LIMIT_BYTES),
        name="router",
    )(x1, *consts)


def _slots_kernel(offs_ref, mt_ref, dest_ref):
    mt = mt_ref[...]
    e = mt[0:2, :].astype(jnp.int32)
    acc = mt[2:4, :].astype(jnp.int32)
    for j in range(N_EXPERTS):
        acc = acc + jnp.where(e == j, offs_ref[j], 0)
    dest_ref[...] = acc


def _slots(offs, meta_t):
    n = meta_t.shape[1]
    tl = min(n, 8192)
    grid_spec = pltpu.PrefetchScalarGridSpec(
        num_scalar_prefetch=1,
        grid=(n // tl,),
        in_specs=[pl.BlockSpec((SUBLANES, tl), lambda i, offs: (0, i))],
        out_specs=pl.BlockSpec((2, tl), lambda i, offs: (0, i)),
    )
    return pl.pallas_call(
        _slots_kernel,
        grid_spec=grid_spec,
        out_shape=jax.ShapeDtypeStruct((2, n), jnp.int32),
        name="slots",
    )(offs, meta_t)


def _row_copy(src, dst, sem):
    return pltpu.make_async_copy(src, dst, sem)


def _dispatch_kernel(nv_ref, dest_ref, h_ref, xs_ref, zbuf, hbuf, zsem, lsem, sem):
    n_blocks = xs_ref.shape[0] // MOE_BLOCK

    @pl.when(pl.program_id(0) == 0)
    def _zero_partial_blocks():
        zbuf[...] = jnp.zeros_like(zbuf)

        def fill(b):
            return _row_copy(zbuf, xs_ref.at[pl.ds(pl.multiple_of(b * MOE_BLOCK, MOE_BLOCK),
                                                   MOE_BLOCK)], zsem)

        def start(b, carry):
            @pl.when(nv_ref[b] < MOE_BLOCK)
            def _():
                fill(b).start()
            return carry

        def wait(b, carry):
            @pl.when(nv_ref[b] < MOE_BLOCK)
            def _():
                fill(b).wait()
            return carry

        lax.fori_loop(0, n_blocks, start, 0)
        lax.fori_loop(0, n_blocks, wait, 0)

    i = pl.program_id(0)
    n_steps = pl.num_programs(0)
    cur = lax.rem(i, 2)

    def load(step, slot):
        rows = pl.ds(pl.multiple_of(step * TOK_TILE, TOK_TILE), TOK_TILE)
        return _row_copy(h_ref.at[rows], hbuf.at[slot], lsem.at[slot])

    def wait_rows(slot):
        for k in range(2):
            _row_copy(hbuf.at[slot], xs_ref.at[pl.ds(0, TOK_TILE)], sem.at[slot]).wait()

    @pl.when(i == 0)
    def _first_load():
        load(0, 0).start()

    @pl.when(i > 0)
    def _drain_previous():
        wait_rows(1 - cur)

    @pl.when(i + 1 < n_steps)
    def _prefetch():
        load(i + 1, 1 - cur).start()

    load(i, cur).wait()

    def issue(r4, carry):
        for u in range(ISSUE_UNROLL):
            r = r4 * ISSUE_UNROLL + u
            for k in range(2):
                _row_copy(hbuf.at[cur, r], xs_ref.at[dest_ref[k, r]],
                          sem.at[cur]).start(priority=k)
        return carry

    lax.fori_loop(0, TOK_TILE // ISSUE_UNROLL, issue, 0)

    @pl.when(i == n_steps - 1)
    def _drain_last():
        wait_rows(cur)


def _dispatch(n_valid, dest, h2s, cap):
    n = h2s.shape[0]
    grid_spec = pltpu.PrefetchScalarGridSpec(
        num_scalar_prefetch=1,
        grid=(n // TOK_TILE,),
        in_specs=[pl.BlockSpec((2, TOK_TILE), lambda i, nv: (0, i), memory_space=pltpu.SMEM),
                  pl.BlockSpec(memory_space=pl.ANY)],
        out_specs=pl.BlockSpec(memory_space=pl.ANY),
        scratch_shapes=[pltpu.VMEM((MOE_BLOCK, SLAB, LANES), F32),
                        pltpu.VMEM((2, TOK_TILE, SLAB, LANES), F32),
                        pltpu.SemaphoreType.DMA(()),
                        pltpu.SemaphoreType.DMA((2,)),
                        pltpu.SemaphoreType.DMA((2,))],
    )
    return pl.pallas_call(
        _dispatch_kernel,
        grid_spec=grid_spec,
        out_shape=jax.ShapeDtypeStruct((cap, SLAB, LANES), F32),
        compiler_params=pltpu.CompilerParams(dimension_semantics=("arbitrary",)),
        name="dispatch",
    )(n_valid, dest, h2s)


def _slab_rows_to_matrix(ref, rows):
    return jnp.concatenate([ref[pl.ds(s, rows, stride=SLAB), :] for s in range(SLAB)], axis=1)


def _matrix_to_slab_rows(ref, val, rows):
    for s in range(SLAB):
        ref[pl.ds(s, rows, stride=SLAB), :] = val[:, s * LANES:(s + 1) * LANES]


def _expert_kernel(be_ref, nv_ref, xs_ref, wg_ref, wu_ref, wd_ref, ys_ref):
    i = pl.program_id(0)
    nv = nv_ref[i]

    @pl.when(nv > 0)
    def _compute():
        xb = _slab_rows_to_matrix(xs_ref, MOE_BLOCK).astype(BF16)
        g = _dot(xb, wg_ref[0].astype(BF16))
        u = _dot(xb, wu_ref[0].astype(BF16))
        hid = (g * _sigmoid(g) * u).astype(BF16)
        _matrix_to_slab_rows(ys_ref, _dot(hid, wd_ref[0].astype(BF16)), MOE_BLOCK)

    @pl.when(nv == 0)
    def _skip():
        ys_ref[...] = jnp.zeros_like(ys_ref)


def _experts(block_e, n_valid, xs, wg, wu, wd):
    n_blocks = xs.shape[0] // (MOE_BLOCK * SLAB)
    blk = MOE_BLOCK * SLAB
    grid_spec = pltpu.PrefetchScalarGridSpec(
        num_scalar_prefetch=2,
        grid=(n_blocks,),
        in_specs=[pl.BlockSpec((blk, LANES), lambda i, be, nv: (i, 0)),
                  pl.BlockSpec((1, D_MODEL, D_EXPERT), lambda i, be, nv: (be[i], 0, 0)),
                  pl.BlockSpec((1, D_MODEL, D_EXPERT), lambda i, be, nv: (be[i], 0, 0)),
                  pl.BlockSpec((1, D_EXPERT, D_MODEL), lambda i, be, nv: (be[i], 0, 0))],
        out_specs=pl.BlockSpec((blk, LANES), lambda i, be, nv: (i, 0)),
    )
    return pl.pallas_call(
        _expert_kernel,
        grid_spec=grid_spec,
        out_shape=jax.ShapeDtypeStruct(xs.shape, F32),
        compiler_params=pltpu.CompilerParams(dimension_semantics=("arbitrary",),
                                             vmem_limit_bytes=VMEM_LIMIT_BYTES),
        name="experts",
    )(block_e, n_valid, xs, wg, wu, wd)


def _combine_kernel(dcur_ref, dnxt_ref, x_ref, meta_ref, gt_ref, gfin_ref, ys_ref, ys2d_ref,
                    o_ref, buf, sem):
    i = pl.program_id(0)
    n_steps = pl.num_programs(0)
    cur = lax.rem(i, 2)

    def gather(dest_ref, slot):
        def issue(r4, carry):
            for u in range(ISSUE_UNROLL):
                r = r4 * ISSUE_UNROLL + u
                for k in range(2):
                    dst = buf.at[slot, k, pl.ds(pl.multiple_of(r * SLAB, SLAB), SLAB)]
                    _row_copy(ys_ref.at[dest_ref[k, r]], dst, sem.at[slot]).start(priority=k)
            return carry

        lax.fori_loop(0, TOK_TILE // ISSUE_UNROLL, issue, 0)

    @pl.when(i == 0)
    def _first_gather():
        gather(dcur_ref, 0)

    @pl.when(i + 1 < n_steps)
    def _prefetch():
        gather(dnxt_ref, 1 - cur)

    for k in range(2):
        _row_copy(ys2d_ref.at[pl.ds(0, TOK_TILE * SLAB)], buf.at[cur, k], sem.at[cur]).wait()

    meta = meta_ref[...]
    y = (meta[:, 4:5] * _slab_rows_to_matrix(buf.at[cur, 0], TOK_TILE)
         + meta[:, 5:6] * _slab_rows_to_matrix(buf.at[cur, 1], TOK_TILE))
    x2 = x_ref[...] + _slab_bcast(gt_ref[...], TOK_TILE // BATCH) * y
    o_ref[...] = _rms(x2) * gfin_ref[...]


def _combine(dest, x1, meta, gt2, g_final, ys):
    n = x1.shape[0]
    last = n // TOK_TILE - 1
    return pl.pallas_call(
        _combine_kernel,
        grid=(n // TOK_TILE,),
        in_specs=[pl.BlockSpec((2, TOK_TILE), lambda i: (0, i), memory_space=pltpu.SMEM),
                  pl.BlockSpec((2, TOK_TILE), lambda i: (0, jnp.minimum(i + 1, last)),
                               memory_space=pltpu.SMEM),
                  pl.BlockSpec((TOK_TILE, D_MODEL), lambda i: (i, 0)),
                  pl.BlockSpec((TOK_TILE, LANES), lambda i: (i, 0)),
                  _const_spec(gt2.shape),
                  _const_spec(g_final.shape),
                  pl.BlockSpec(memory_space=pl.ANY),
                  pl.BlockSpec(memory_space=pl.ANY)],
        out_specs=pl.BlockSpec((TOK_TILE, D_MODEL), lambda i: (i, 0)),
        out_shape=jax.ShapeDtypeStruct((n, D_MODEL), F32),
        scratch_shapes=[pltpu.VMEM((2, 2, TOK_TILE * SLAB, LANES), F32),
                        pltpu.SemaphoreType.DMA((2,))],
        compiler_params=pltpu.CompilerParams(dimension_semantics=("arbitrary",),
                                             vmem_limit_bytes=VMEM_LIMIT_BYTES),
        name="combine",
    )(dest, dest, x1, meta, gt2, g_final, ys.reshape(-1, SLAB, LANES), ys)


def _block_diag(w):
    h, d, _ = w.shape
    eye = jnp.eye(h, dtype=w.dtype)
    return (eye[:, None, :, None] * w[:, :, None, :]).reshape(h * d, h * d)


def kernel(x, c, w_ada, b_ada, g_mix, g_ffn, g_final, w_in, conv_w, conv_b, lru_wr, lru_br,
           lru_wi, lru_bi, lru_lambda, gla_wa2, gla_ba, gla_gnorm, w_out, w_coarse, b_coarse,
           w_fine, b_fine, w_gate, w_up, w_down):
    bsz, seq, d = x.shape
    assert bsz == BATCH and d == D_MODEL and seq % GLA_CHUNK == 0
    assert w_ada.shape[0] == 1, "single-layer problem"
    n = bsz * seq
    row = lambda v: v.reshape(1, -1)

    mod = _ada(c, w_ada[0], b_ada[0])
    sh1, sc1, gt1, sh2, sc2, gt2 = jnp.split(mod, 6, axis=-1)

    xt = x.transpose(1, 0, 2).reshape(n, d)
    w_in_p = jnp.pad(w_in[0], ((0, 0), (0, D_IN_PAD - w_in.shape[2]))).astype(BF16)
    wa2_p = jnp.pad(gla_wa2[0], ((0, LANES - GLA_GATE_RANK), (0, 0))).astype(BF16)
    x1 = _mixer(xt, 1.0 + sc1, sh1, gt1, row(g_mix[0]), w_in_p, conv_w[0], row(conv_b[0]),
                _block_diag(lru_wr[0]).astype(BF16), row(lru_br[0]),
                _block_diag(lru_wi[0]).astype(BF16), row(lru_bi[0]), row(lru_lambda[0]),
                wa2_p, row(gla_ba[0]), row(gla_gnorm[0]), w_out[0].astype(BF16))

    n_route = N_GROUPS + N_EXPERTS
    w_r = jnp.pad(jnp.concatenate([w_coarse[0], w_fine[0]], axis=1),
                  ((0, 0), (0, LANES - n_route))).astype(BF16)
    b_r = jnp.pad(jnp.concatenate([b_coarse[0], b_fine[0]]), (0, LANES - n_route)).reshape(1, LANES)
    h2, meta, meta_t, cnt = _router(x1, 1.0 + sc2, sh2, row(g_ffn[0]), w_r, b_r)

    counts = cnt[0, N_GROUPS:N_GROUPS + N_EXPERTS].astype(jnp.int32)
    padded = (counts + MOE_BLOCK - 1) // MOE_BLOCK * MOE_BLOCK
    pends = jnp.cumsum(padded)
    offs = pends - padded
    cap = (2 * n + MOE_BLOCK - 1) // MOE_BLOCK * MOE_BLOCK + N_EXPERTS * MOE_BLOCK
    n_blocks = cap // MOE_BLOCK
    starts = jnp.arange(n_blocks, dtype=jnp.int32) * MOE_BLOCK
    block_e = jnp.minimum(jnp.sum((pends[None, :] <= starts[:, None]).astype(jnp.int32), axis=1),
                          N_EXPERTS - 1)
    n_valid = jnp.clip(offs[block_e] + counts[block_e] - starts, 0, MOE_BLOCK).astype(jnp.int32)
    dest = _slots(offs.astype(jnp.int32), meta_t)

    xs = _dispatch(n_valid, dest, h2.reshape(n, SLAB, LANES), cap)
    ys = _experts(block_e, n_valid, xs.reshape(cap * SLAB, LANES),
                  w_gate[0], w_up[0], w_down[0])
    out_t = _combine(dest, x1, meta, gt2, row(g_final), ys)
    return out_t.reshape(seq, bsz, d).transpose(1, 0, 2)
```

```python
import functools

import jax
import jax.numpy as jnp
import numpy as np
from jax import lax
from jax.experimental import pallas as pl
from jax.experimental.pallas import tpu as pltpu

F32 = jnp.float32
BF16 = jnp.bfloat16

SUBLANES = 8
LANES = 128
VMEM_LIMIT_BYTES = 56 * 1024 * 1024

D_MODEL = 1024
BATCH = SUBLANES
D_LRU = 512
LRU_HEADS = 8
CONV_WIDTH = 4
LRU_C = 8.0
D_GLA = 512
GLA_HEADS = 4
GLA_DV = 128
GLA_DK = 64
GLA_GATE_RANK = 16
GLA_GATE_NORM = 16.0
GLA_CHUNK = 64
N_GROUPS = 4
EXPERTS_PER_GROUP = 8
N_EXPERTS = 32
D_EXPERT = 512
MOE_BLOCK = 512
EPS = 1e-6

QK = GLA_HEADS * GLA_DK
ROWS = GLA_CHUNK * BATCH
HALO = (CONV_WIDTH - 1) * BATCH
SLAB = D_MODEL // LANES
PACK = D_MODEL // (2 * LANES)
C_LX, C_LY, C_Q, C_K, C_V, C_G, C_GL = 0, 512, 1024, 1280, 1536, 2048, 2560
D_IN_PAD = 2688
ROUTE_TILE = 512
TOK_TILE = 512
ISSUE_UNROLL = 4
DOWN_CHUNK = 256


def _dot(a, b):
    return jnp.dot(a, b, preferred_element_type=F32)


def _dot_nt(a, b):
    return lax.dot_general(a, b, (((1,), (1,)), ((), ())), preferred_element_type=F32)


def _softplus(z):
    return jnp.maximum(z, 0.0) + jnp.log1p(jnp.exp(-jnp.abs(z)))


def _sigmoid(z):
    return 1.0 / (1.0 + jnp.exp(-z))


def _rms(x):
    return x * lax.rsqrt(jnp.mean(x * x, axis=-1, keepdims=True) + EPS)


def _pack_rows(ref, val, rows):
    for s in range(PACK):
        lo = val[:, (2 * s) * LANES:(2 * s + 1) * LANES].astype(BF16).astype(F32)
        hi = val[:, (2 * s + 1) * LANES:(2 * s + 2) * LANES].astype(BF16).astype(F32)
        word = (lax.bitcast_convert_type(lo, jnp.uint32) >> 16) | lax.bitcast_convert_type(
            hi, jnp.uint32)
        ref[pl.ds(s, rows, stride=PACK), :] = word


def _unpack_rows(ref, rows):
    cols = []
    for s in range(PACK):
        word = ref[pl.ds(s, rows, stride=PACK), :]
        cols.append(lax.bitcast_convert_type(word << 16, F32))
        cols.append(lax.bitcast_convert_type(word & jnp.uint32(0xFFFF0000), F32))
    return jnp.concatenate(cols, axis=1)


def _slab_bcast(v, n):
    c = v.shape[-1]
    return jnp.broadcast_to(v[None], (n, SUBLANES, c)).reshape(n * SUBLANES, c)


def _ada_kernel(c_ref, w_ref, b_ref, o_ref):
    c = c_ref[...]
    s = c * _sigmoid(c)
    o_ref[...] = jnp.dot(s, w_ref[...], preferred_element_type=F32,
                         precision=lax.Precision.HIGHEST) + b_ref[...]


def _ada(c, w, b):
    n_out = w.shape[1]
    tn = 1024
    return pl.pallas_call(
        _ada_kernel,
        grid=(n_out // tn,),
        in_specs=[pl.BlockSpec((BATCH, D_MODEL), lambda j: (0, 0)),
                  pl.BlockSpec((D_MODEL, tn), lambda j: (0, j)),
                  pl.BlockSpec((1, tn), lambda j: (0, j))],
        out_specs=pl.BlockSpec((BATCH, tn), lambda j: (0, j)),
        out_shape=jax.ShapeDtypeStruct((BATCH, n_out), F32),
        name="ada",
    )(c, w, b.reshape(1, n_out))


def _mixer_kernel(x_ref, sc_ref, sh_ref, gt_ref, gmix_ref, win_ref, cw_ref, cb_ref,
                  wr_ref, br_ref, wi_ref, bi_ref, lam_ref, wa2_ref, ba_ref, gn_ref, wout_ref,
                  o_ref,
                  xbuf, a_s, u_s, hs_s, hc_s, la_s, bc_s, st_s, cm_s):
    i = pl.program_id(0)
    nt = GLA_CHUNK

    @pl.when(i == 0)
    def _init():
        xbuf[0:HALO, :] = jnp.zeros((HALO, D_LRU), F32)
        hc_s[...] = jnp.zeros_like(hc_s)
        st_s[...] = jnp.zeros_like(st_s)
        r = lax.broadcasted_iota(jnp.int32, (ROWS, ROWS), 0)
        c = lax.broadcasted_iota(jnp.int32, (ROWS, ROWS), 1)
        same_seq = (r & (BATCH - 1)) == (c & (BATCH - 1))
        cm_s[...] = jnp.where(same_seq & ((c >> 3) <= (r >> 3)), 1.0, 0.0).astype(F32)

    x = x_ref[...]
    y = _rms(x) * gmix_ref[...]
    h = (y * _slab_bcast(sc_ref[...], nt) + _slab_bcast(sh_ref[...], nt)).astype(BF16)

    def proj(c0, c1):
        return _dot(h, win_ref[:, c0:c1])

    xbuf[HALO:HALO + ROWS, :] = proj(C_LX, C_LY)
    cw = cw_ref[...]
    cx = cb_ref[...] + sum(cw[k:k + 1, :] * xbuf[k * BATCH:k * BATCH + ROWS, :]
                           for k in range(CONV_WIDTH))
    xbuf[0:HALO, :] = xbuf[ROWS:ROWS + HALO, :]
    cxb = cx.astype(BF16)
    r_gate = _sigmoid(_dot(cxb, wr_ref[...]) + br_ref[...])
    i_gate = _sigmoid(_dot(cxb, wi_ref[...]) + bi_ref[...])
    log_a = (-LRU_C) * r_gate * _softplus(-lam_ref[...])
    a_s[...] = jnp.exp(log_a)
    th = jnp.tanh(log_a)
    u_s[...] = jnp.sqrt(-2.0 * th / (1.0 - th)) * (i_gate * cx)
    hcur = hc_s[...]
    for t in range(nt):
        sl = slice(t * BATCH, (t + 1) * BATCH)
        hcur = a_s[sl, :] * hcur + u_s[sl, :]
        hs_s[sl, :] = hcur
    hc_s[...] = hcur
    lru_out = hs_s[...] * jax.nn.gelu(proj(C_LY, C_Q), approximate=True)

    gate_lr = proj(C_GL, D_IN_PAD).astype(BF16)
    z = _dot(gate_lr, wa2_ref[...]) + ba_ref[...]
    la_s[...] = -_softplus(-z) * (1.0 / GLA_GATE_NORM)
    bcur = jnp.zeros((BATCH, QK), F32)
    for t in range(nt):
        sl = slice(t * BATCH, (t + 1) * BATCH)
        bcur = bcur + la_s[sl, :]
        bc_s[sl, :] = bcur
    bc = bc_s[...]
    e_last = jnp.exp(bcur)
    q_dec = proj(C_Q, C_K) * (GLA_DK ** -0.5) * jnp.exp(bc)
    kk = proj(C_K, C_V)
    k_dec = kk * jnp.exp(-bc)
    k_last = kk * jnp.exp(_slab_bcast(bcur, nt) - bc)
    vv = proj(C_V, C_G)
    gg = proj(C_G, C_GL)
    causal = cm_s[...] > 0.5

    lane = lax.broadcasted_iota(jnp.int32, (SUBLANES, LANES), 1)
    sub = lax.broadcasted_iota(jnp.int32, (SUBLANES, LANES), 0)
    half = lane >> 6
    seq_sel = [(sub == 2 * j + half).astype(F32) for j in range(BATCH // 2)]

    def expand(m):
        n = m.shape[0] // SUBLANES
        m3 = m.reshape(n, SUBLANES, LANES)
        return jnp.concatenate([(m3 * s[None]).reshape(m.shape) for s in seq_sel], axis=1)

    def both_halves(m, hh):
        keep = (lax.broadcasted_iota(jnp.int32, m.shape, 1) >> 6) == hh
        mh = jnp.where(keep, m, 0.0)
        return mh, mh + pltpu.roll(mh, GLA_DK, axis=1)

    gla_parts = []
    for hd in range(GLA_HEADS):
        p, hh = hd // 2, hd % 2
        pc = slice(p * LANES, (p + 1) * LANES)
        qh, q_both = both_halves(q_dec[:, pc], hh)
        _, k_both = both_halves(k_last[:, pc], hh)
        _, e_both = both_halves(e_last[:, pc], hh)
        v_h = vv[:, hd * GLA_DV:(hd + 1) * GLA_DV]
        v_hb = v_h.astype(BF16)
        scores = _dot_nt(qh.astype(BF16), k_dec[:, pc].astype(BF16))
        scores = jnp.where(causal, scores, 0.0)
        o_h = _dot(scores.astype(BF16), v_hb)
        st = st_s[hd]
        o_h = o_h + _dot_nt(expand(q_both).astype(BF16), st.astype(BF16))
        kv_t = _dot(v_h.T.astype(BF16), expand(k_both).astype(BF16))
        decay = jnp.concatenate(
            [jnp.sum(e_both * s, axis=0, keepdims=True) for s in seq_sel], axis=1)
        st_s[hd] = st * decay + kv_t
        o_n = _rms(o_h) * gn_ref[...]
        g_h = gg[:, hd * GLA_DV:(hd + 1) * GLA_DV]
        gla_parts.append(o_n * (g_h * _sigmoid(g_h)))

    mix_in = jnp.concatenate([lru_out] + gla_parts, axis=1).astype(BF16)
    mix = _dot(mix_in, wout_ref[...])
    o_ref[...] = x + _slab_bcast(gt_ref[...], nt) * mix


def _const_spec(shape):
    nd = len(shape)
    return pl.BlockSpec(shape, lambda i: (0,) * nd)


def _mixer(xt, sc1p, sh1, gt1, g_mix, w_in_p, conv_w, conv_b, wr_d, br, wi_d, bi, lam,
           wa2_p, ba, gn, w_out_b):
    n = xt.shape[0]
    consts = [sc1p, sh1, gt1, g_mix, w_in_p, conv_w, conv_b, wr_d, br, wi_d, bi, lam,
              wa2_p, ba, gn, w_out_b]
    return pl.pallas_call(
        _mixer_kernel,
        grid=(n // ROWS,),
        in_specs=[pl.BlockSpec((ROWS, D_MODEL), lambda i: (i, 0))]
        + [_const_spec(a.shape) for a in consts],
        out_specs=pl.BlockSpec((ROWS, D_MODEL), lambda i: (i, 0)),
        out_shape=jax.ShapeDtypeStruct((n, D_MODEL), F32),
        scratch_shapes=[
            pltpu.VMEM((HALO + ROWS, D_LRU), F32),
            pltpu.VMEM((ROWS, D_LRU), F32),
            pltpu.VMEM((ROWS, D_LRU), F32),
            pltpu.VMEM((ROWS, D_LRU), F32),
            pltpu.VMEM((BATCH, D_LRU), F32),
            pltpu.VMEM((ROWS, QK), F32),
            pltpu.VMEM((ROWS, QK), F32),
            pltpu.VMEM((GLA_HEADS, GLA_DV, BATCH * GLA_DK), F32),
            pltpu.VMEM((ROWS, ROWS), F32),
        ],
        compiler_params=pltpu.CompilerParams(dimension_semantics=("arbitrary",),
                                             vmem_limit_bytes=VMEM_LIMIT_BYTES),
        name="mixer",
    )(xt, *consts)


def _router_kernel(x_ref, sc_ref, sh_ref, gffn_ref, wr_ref, br_ref,
                   h2_ref, meta_ref, metat_ref, cnt_ref, base_s, tri_s):
    i = pl.program_id(0)
    nt = ROUTE_TILE // BATCH

    @pl.when(i == 0)
    def _init():
        base_s[...] = jnp.zeros_like(base_s)
        r = lax.broadcasted_iota(jnp.int32, (ROUTE_TILE, ROUTE_TILE), 0)
        c = lax.broadcasted_iota(jnp.int32, (ROUTE_TILE, ROUTE_TILE), 1)
        tri_s[...] = jnp.where(c < r, 1.0, 0.0).astype(BF16)

    x = x_ref[...]
    h2 = _rms(x) * gffn_ref[...] * _slab_bcast(sc_ref[...], nt) + _slab_bcast(sh_ref[...], nt)
    _pack_rows(h2_ref, h2, ROUTE_TILE)
    logits = _dot(h2.astype(BF16), wr_ref[...]) + br_ref[...]
    lane = lax.broadcasted_iota(jnp.int32, logits.shape, 1)
    neg = jnp.float32(-jnp.inf)

    def first_max(vals):
        m = jnp.max(vals, axis=-1, keepdims=True)
        idx = jnp.min(jnp.where(vals == m, lane, LANES), axis=-1, keepdims=True)
        return m, idx

    cl = jnp.where(lane < N_GROUPS, logits, neg)
    cmax, grp = first_max(cl)
    p_grp = 1.0 / jnp.sum(jnp.exp(cl - cmax), axis=-1, keepdims=True)
    lo = N_GROUPS + EXPERTS_PER_GROUP * grp
    fl = jnp.where((lane >= lo) & (lane < lo + EXPERTS_PER_GROUP), logits, neg)
    f1, i1 = first_max(fl)
    f2, i2 = first_max(jnp.where(lane == i1, neg, fl))
    z = jnp.sum(jnp.exp(fl - f1), axis=-1, keepdims=True)
    tp1 = 1.0 / z
    tp2 = jnp.exp(f2 - f1) / z
    w1 = p_grp * tp1 / (tp1 + tp2)
    w2 = p_grp * tp2 / (tp1 + tp2)

    hit1 = lane == i1
    hit2 = lane == i2
    assign = jnp.where(hit1 | hit2, 1.0, 0.0)
    before = _dot(tri_s[...], assign.astype(BF16)) + base_s[...]
    rank1 = jnp.sum(jnp.where(hit1, before, 0.0), axis=-1, keepdims=True)
    rank2 = jnp.sum(jnp.where(hit2, before, 0.0), axis=-1, keepdims=True)
    base_s[...] = base_s[...] + jnp.sum(assign, axis=0, keepdims=True)
    cnt_ref[...] = jnp.broadcast_to(base_s[...], cnt_ref.shape)

    cols = [(i1 - N_GROUPS).astype(F32), (i2 - N_GROUPS).astype(F32), rank1, rank2, w1, w2]
    meta = jnp.zeros(logits.shape, F32)
    for j, col in enumerate(cols):
        meta = jnp.where(lane == j, col, meta)
    meta_ref[...] = meta
    metat_ref[...] = meta.T[0:SUBLANES, :]


def _router(x1, sc2p, sh2, g_ffn, w_r, b_r):
    n = x1.shape[0]
    consts = [sc2p, sh2, g_ffn, w_r, b_r]
    return pl.pallas_call(
        _router_kernel,
        grid=(n // ROUTE_TILE,),
        in_specs=[pl.BlockSpec((ROUTE_TILE, D_MODEL), lambda i: (i, 0))]
        + [_const_spec(a.shape) for a in consts],
        out_specs=[pl.BlockSpec((ROUTE_TILE * PACK, LANES), lambda i: (i, 0)),
                   pl.BlockSpec((ROUTE_TILE, LANES), lambda i: (i, 0)),
                   pl.BlockSpec((SUBLANES, ROUTE_TILE), lambda i: (0, i)),
                   pl.BlockSpec((SUBLANES, LANES), lambda i: (0, 0))],
        out_shape=[jax.ShapeDtypeStruct((n * PACK, LANES), jnp.uint32),
                   jax.ShapeDtypeStruct((n, LANES), F32),
                   jax.ShapeDtypeStruct((SUBLANES, n), F32),
                   jax.ShapeDtypeStruct((SUBLANES, LANES), F32)],
        scratch_shapes=[pltpu.VMEM((1, LANES), F32),
                        pltpu.VMEM((ROUTE_TILE, ROUTE_TILE), BF16)],
        compiler_params=pltpu.CompilerParams(dimension_semantics=("arbitrary",),
                                             vmem_limit_bytes=VMEM_LIMIT_BYTES),
        name="router",
    )(x1, *consts)


def _slots_kernel(offs_ref, mt_ref, dest_ref):
    mt = mt_ref[...]
    e = mt[0:2, :].astype(jnp.int32)
    acc = mt[2:4, :].astype(jnp.int32)
    for j in range(N_EXPERTS):
        acc = acc + jnp.where(e == j, offs_ref[j], 0)
    dest_ref[...] = acc


def _slots(offs, meta_t):
    n = meta_t.shape[1]
    tl = min(n, 8192)
    grid_spec = pltpu.PrefetchScalarGridSpec(
        num_scalar_prefetch=1,
        grid=(n // tl,),
        in_specs=[pl.BlockSpec((SUBLANES, tl), lambda i, offs: (0, i))],
        out_specs=pl.BlockSpec((2, tl), lambda i, offs: (0, i)),
    )
    return pl.pallas_call(
        _slots_kernel,
        grid_spec=grid_spec,
        out_shape=jax.ShapeDtypeStruct((2, n), jnp.int32),
        name="slots",
    )(offs, meta_t)


def _row_copy(src, dst, sem):
    return pltpu.make_async_copy(src, dst, sem)


def _dispatch_kernel(nv_ref, dest_ref, h_ref, xs_ref, zbuf, hbuf, zsem, lsem, sem):
    n_blocks = xs_ref.shape[0] // MOE_BLOCK

    @pl.when(pl.program_id(0) == 0)
    def _zero_partial_blocks():
        zbuf[...] = jnp.zeros_like(zbuf)

        def fill(b):
            return _row_copy(zbuf, xs_ref.at[pl.ds(pl.multiple_of(b * MOE_BLOCK, MOE_BLOCK),
                                                   MOE_BLOCK)], zsem)

        def start(b, carry):
            @pl.when(nv_ref[b] < MOE_BLOCK)
            def _():
                fill(b).start()
            return carry

        def wait(b, carry):
            @pl.when(nv_ref[b] < MOE_BLOCK)
            def _():
                fill(b).wait()
            return carry

        lax.fori_loop(0, n_blocks, start, 0)
        lax.fori_loop(0, n_blocks, wait, 0)

    i = pl.program_id(0)
    n_steps = pl.num_programs(0)
    cur = lax.rem(i, 2)

    def load(step, slot):
        rows = pl.ds(pl.multiple_of(step * TOK_TILE, TOK_TILE), TOK_TILE)
        return _row_copy(h_ref.at[rows], hbuf.at[slot], lsem.at[slot])

    def wait_rows(slot):
        for k in range(2):
            _row_copy(hbuf.at[slot], xs_ref.at[pl.ds(0, TOK_TILE)], sem.at[slot]).wait()

    @pl.when(i == 0)
    def _first_load():
        load(0, 0).start()

    @pl.when(i > 0)
    def _drain_previous():
        wait_rows(1 - cur)

    @pl.when(i + 1 < n_steps)
    def _prefetch():
        load(i + 1, 1 - cur).start()

    load(i, cur).wait()

    def issue(r4, carry):
        for u in range(ISSUE_UNROLL):
            r = r4 * ISSUE_UNROLL + u
            for k in range(2):
                _row_copy(hbuf.at[cur, r], xs_ref.at[dest_ref[k, r]],
                          sem.at[cur]).start(priority=k)
        return carry

    lax.fori_loop(0, TOK_TILE // ISSUE_UNROLL, issue, 0)

    @pl.when(i == n_steps - 1)
    def _drain_last():
        wait_rows(cur)


def _dispatch(n_valid, dest, h2s, cap):
    n = h2s.shape[0]
    grid_spec = pltpu.PrefetchScalarGridSpec(
        num_scalar_prefetch=1,
        grid=(n // TOK_TILE,),
        in_specs=[pl.BlockSpec((2, TOK_TILE), lambda i, nv: (0, i), memory_space=pltpu.SMEM),
                  pl.BlockSpec(memory_space=pl.ANY)],
        out_specs=pl.BlockSpec(memory_space=pl.ANY),
        scratch_shapes=[pltpu.VMEM((MOE_BLOCK, PACK, LANES), jnp.uint32),
                        pltpu.VMEM((2, TOK_TILE, PACK, LANES), jnp.uint32),
                        pltpu.SemaphoreType.DMA(()),
                        pltpu.SemaphoreType.DMA((2,)),
                        pltpu.SemaphoreType.DMA((2,))],
    )
    return pl.pallas_call(
        _dispatch_kernel,
        grid_spec=grid_spec,
        out_shape=jax.ShapeDtypeStruct((cap, PACK, LANES), jnp.uint32),
        compiler_params=pltpu.CompilerParams(dimension_semantics=("arbitrary",)),
        name="dispatch",
    )(n_valid, dest, h2s)


def _slab_rows_to_matrix(ref, rows):
    return jnp.concatenate([ref[pl.ds(s, rows, stride=SLAB), :] for s in range(SLAB)], axis=1)


def _expert_kernel(be_ref, nv_ref, xs_ref, wg_ref, wu_ref, wd_ref, ys_ref):
    i = pl.program_id(0)
    nv = nv_ref[i]

    @pl.when(nv > 0)
    def _compute():
        xb = _unpack_rows(xs_ref, MOE_BLOCK).astype(BF16)
        g = _dot(xb, wg_ref[0].astype(BF16))
        u = _dot(xb, wu_ref[0].astype(BF16))
        hid = (g * _sigmoid(g) * u).astype(BF16)
        for c in range(D_MODEL // DOWN_CHUNK):
            cols = slice(c * DOWN_CHUNK, (c + 1) * DOWN_CHUNK)
            out = _dot(hid, wd_ref[0, :, cols].astype(BF16))
            for j in range(DOWN_CHUNK // LANES):
                s = c * (DOWN_CHUNK // LANES) + j
                ys_ref[pl.ds(s, MOE_BLOCK, stride=SLAB), :] = out[:, j * LANES:(j + 1) * LANES]

    @pl.when(nv == 0)
    def _skip():
        ys_ref[...] = jnp.zeros_like(ys_ref)


def _experts(block_e, n_valid, xs, wg, wu, wd):
    n_blocks = xs.shape[0] // (MOE_BLOCK * PACK)
    grid_spec = pltpu.PrefetchScalarGridSpec(
        num_scalar_prefetch=2,
        grid=(n_blocks,),
        in_specs=[pl.BlockSpec((MOE_BLOCK * PACK, LANES), lambda i, be, nv: (i, 0)),
                  pl.BlockSpec((1, D_MODEL, D_EXPERT), lambda i, be, nv: (be[i], 0, 0)),
                  pl.BlockSpec((1, D_MODEL, D_EXPERT), lambda i, be, nv: (be[i], 0, 0)),
                  pl.BlockSpec((1, D_EXPERT, D_MODEL), lambda i, be, nv: (be[i], 0, 0))],
        out_specs=pl.BlockSpec((MOE_BLOCK * SLAB, LANES), lambda i, be, nv: (i, 0)),
    )
    return pl.pallas_call(
        _expert_kernel,
        grid_spec=grid_spec,
        out_shape=jax.ShapeDtypeStruct((n_blocks * MOE_BLOCK * SLAB, LANES), F32),
        compiler_params=pltpu.CompilerParams(dimension_semantics=("arbitrary",),
                                             vmem_limit_bytes=VMEM_LIMIT_BYTES),
        name="experts",
    )(block_e, n_valid, xs, wg, wu, wd)


def _combine_kernel(dcur_ref, dnxt_ref, x_ref, meta_ref, gt_ref, gfin_ref, ys_ref, ys2d_ref,
                    o_ref, buf, sem):
    i = pl.program_id(0)
    n_steps = pl.num_programs(0)
    cur = lax.rem(i, 2)

    def gather(dest_ref, slot):
        def issue(r4, carry):
            for u in range(ISSUE_UNROLL):
                r = r4 * ISSUE_UNROLL + u
                for k in range(2):
                    dst = buf.at[slot, k, pl.ds(pl.multiple_of(r * SLAB, SLAB), SLAB)]
                    _row_copy(ys_ref.at[dest_ref[k, r]], dst, sem.at[slot]).start(priority=k)
            return carry

        lax.fori_loop(0, TOK_TILE // ISSUE_UNROLL, issue, 0)

    @pl.when(i == 0)
    def _first_gather():
        gather(dcur_ref, 0)

    @pl.when(i + 1 < n_steps)
    def _prefetch():
        gather(dnxt_ref, 1 - cur)

    for k in range(2):
        _row_copy(ys2d_ref.at[pl.ds(0, TOK_TILE * SLAB)], buf.at[cur, k], sem.at[cur]).wait()

    meta = meta_ref[...]
    y = (meta[:, 4:5] * _slab_rows_to_matrix(buf.at[cur, 0], TOK_TILE)
         + meta[:, 5:6] * _slab_rows_to_matrix(buf.at[cur, 1], TOK_TILE))
    x2 = x_ref[...] + _slab_bcast(gt_ref[...], TOK_TILE // BATCH) * y
    o_ref[...] = _rms(x2) * gfin_ref[...]


def _combine(dest, x1, meta, gt2, g_final, ys):
    n = x1.shape[0]
    last = n // TOK_TILE - 1
    return pl.pallas_call(
        _combine_kernel,
        grid=(n // TOK_TILE,),
        in_specs=[pl.BlockSpec((2, TOK_TILE), lambda i: (0, i), memory_space=pltpu.SMEM),
                  pl.BlockSpec((2, TOK_TILE), lambda i: (0, jnp.minimum(i + 1, last)),
                               memory_space=pltpu.SMEM),
                  pl.BlockSpec((TOK_TILE, D_MODEL), lambda i: (i, 0)),
                  pl.BlockSpec((TOK_TILE, LANES), lambda i: (i, 0)),
                  _const_spec(gt2.shape),
                  _const_spec(g_final.shape),
                  pl.BlockSpec(memory_space=pl.ANY),
                  pl.BlockSpec(memory_space=pl.ANY)],
        out_specs=pl.BlockSpec((TOK_TILE, D_MODEL), lambda i: (i, 0)),
        out_shape=jax.ShapeDtypeStruct((n, D_MODEL), F32),
        scratch_shapes=[pltpu.VMEM((2, 2, TOK_TILE * SLAB, LANES), F32),
                        pltpu.SemaphoreType.DMA((2,))],
        compiler_params=pltpu.CompilerParams(dimension_semantics=("arbitrary",),
                                             vmem_limit_bytes=VMEM_LIMIT_BYTES),
        name="combine",
    )(dest, dest, x1, meta, gt2, g_final, ys.reshape(-1, SLAB, LANES), ys)


def _block_diag(w):
    h, d, _ = w.shape
    eye = jnp.eye(h, dtype=w.dtype)
    return (eye[:, None, :, None] * w[:, :, None, :]).reshape(h * d, h * d)


def kernel(x, c, w_ada, b_ada, g_mix, g_ffn, g_final, w_in, conv_w, conv_b, lru_wr, lru_br,
           lru_wi, lru_bi, lru_lambda, gla_wa2, gla_ba, gla_gnorm, w_out, w_coarse, b_coarse,
           w_fine, b_fine, w_gate, w_up, w_down):
    bsz, seq, d = x.shape
    assert bsz == BATCH and d == D_MODEL and seq % GLA_CHUNK == 0
    assert w_ada.shape[0] == 1, "single-layer problem"
    n = bsz * seq
    row = lambda v: v.reshape(1, -1)

    mod = _ada(c, w_ada[0], b_ada[0])
    sh1, sc1, gt1, sh2, sc2, gt2 = jnp.split(mod, 6, axis=-1)

    xt = x.transpose(1, 0, 2).reshape(n, d)
    w_in_p = jnp.pad(w_in[0], ((0, 0), (0, D_IN_PAD - w_in.shape[2]))).astype(BF16)
    wa2_p = jnp.pad(gla_wa2[0], ((0, LANES - GLA_GATE_RANK), (0, 0))).astype(BF16)
    x1 = _mixer(xt, 1.0 + sc1, sh1, gt1, row(g_mix[0]), w_in_p, conv_w[0], row(conv_b[0]),
                _block_diag(lru_wr[0]).astype(BF16), row(lru_br[0]),
                _block_diag(lru_wi[0]).astype(BF16), row(lru_bi[0]), row(lru_lambda[0]),
                wa2_p, row(gla_ba[0]), row(gla_gnorm[0]), w_out[0].astype(BF16))

    n_route = N_GROUPS + N_EXPERTS
    w_r = jnp.pad(jnp.concatenate([w_coarse[0], w_fine[0]], axis=1),
                  ((0, 0), (0, LANES - n_route))).astype(BF16)
    b_r = jnp.pad(jnp.concatenate([b_coarse[0], b_fine[0]]), (0, LANES - n_route)).reshape(1, LANES)
    h2, meta, meta_t, cnt = _router(x1, 1.0 + sc2, sh2, row(g_ffn[0]), w_r, b_r)

    counts = cnt[0, N_GROUPS:N_GROUPS + N_EXPERTS].astype(jnp.int32)
    padded = (counts + MOE_BLOCK - 1) // MOE_BLOCK * MOE_BLOCK
    pends = jnp.cumsum(padded)
    offs = pends - padded
    cap = (2 * n + MOE_BLOCK - 1) // MOE_BLOCK * MOE_BLOCK + N_EXPERTS * MOE_BLOCK
    n_blocks = cap // MOE_BLOCK
    starts = jnp.arange(n_blocks, dtype=jnp.int32) * MOE_BLOCK
    block_e = jnp.minimum(jnp.sum((pends[None, :] <= starts[:, None]).astype(jnp.int32), axis=1),
                          N_EXPERTS - 1)
    n_valid = jnp.clip(offs[block_e] + counts[block_e] - starts, 0, MOE_BLOCK).astype(jnp.int32)
    dest = _slots(offs.astype(jnp.int32), meta_t)

    xs = _dispatch(n_valid, dest, h2.reshape(n, PACK, LANES), cap)
    ys = _experts(block_e, n_valid, xs.reshape(cap * PACK, LANES),
                  w_gate[0], w_up[0], w_down[0])
    out_t = _combine(dest, x1, meta, gt2, row(g_final), ys)
    return out_t.reshape(seq, bsz, d).transpose(1, 0, 2)
```

```python
import functools

import jax
import jax.numpy as jnp
import numpy as np
from jax import lax
from jax.experimental import pallas as pl
from jax.experimental.pallas import tpu as pltpu

F32 = jnp.float32
BF16 = jnp.bfloat16

SUBLANES = 8
LANES = 128
VMEM_LIMIT_BYTES = 56 * 1024 * 1024

D_MODEL = 1024
BATCH = SUBLANES
D_LRU = 512
LRU_HEADS = 8
CONV_WIDTH = 4
LRU_C = 8.0
D_GLA = 512
GLA_HEADS = 4
GLA_DV = 128
GLA_DK = 64
GLA_GATE_RANK = 16
GLA_GATE_NORM = 16.0
GLA_CHUNK = 64
N_GROUPS = 4
EXPERTS_PER_GROUP = 8
N_EXPERTS = 32
D_EXPERT = 512
MOE_BLOCK = 512
EPS = 1e-6

QK = GLA_HEADS * GLA_DK
ROWS = GLA_CHUNK * BATCH
HALO = (CONV_WIDTH - 1) * BATCH
SLAB = D_MODEL // LANES
PACK = D_MODEL // (2 * LANES)
C_LX, C_LY, C_Q, C_K, C_V, C_G, C_GL = 0, 512, 1024, 1280, 1536, 2048, 2560
D_IN_PAD = 2688
ROUTE_TILE = 512
TOK_TILE = 512
ISSUE_UNROLL = 4
DOWN_CHUNK = 256
XPITCH = 72


def _dot(a, b):
    return jnp.dot(a, b, preferred_element_type=F32)


def _dot_nt(a, b):
    return lax.dot_general(a, b, (((1,), (1,)), ((), ())), preferred_element_type=F32)


def _softplus(z):
    return jnp.maximum(z, 0.0) + jnp.log1p(jnp.exp(-jnp.abs(z)))


def _sigmoid(z):
    return 1.0 / (1.0 + jnp.exp(-z))


def _rms(x):
    return x * lax.rsqrt(jnp.mean(x * x, axis=-1, keepdims=True) + EPS)


def _pack_rows(ref, val, rows):
    for s in range(PACK):
        lo = val[:, (2 * s) * LANES:(2 * s + 1) * LANES].astype(BF16).astype(F32)
        hi = val[:, (2 * s + 1) * LANES:(2 * s + 2) * LANES].astype(BF16).astype(F32)
        word = (lax.bitcast_convert_type(lo, jnp.uint32) >> 16) | lax.bitcast_convert_type(
            hi, jnp.uint32)
        ref[pl.ds(s, rows, stride=PACK), :] = word


def _unpack_rows(ref, rows):
    cols = []
    for s in range(PACK):
        word = ref[pl.ds(s, rows, stride=PACK), :]
        cols.append(lax.bitcast_convert_type(word << 16, F32))
        cols.append(lax.bitcast_convert_type(word & jnp.uint32(0xFFFF0000), F32))
    return jnp.concatenate(cols, axis=1)


def _slab_bcast(v, n):
    c = v.shape[-1]
    return jnp.broadcast_to(v[None], (n, SUBLANES, c)).reshape(n * SUBLANES, c)


def _seq_major_copies(hbm, buf, sem, step, slot, to_hbm):
    copies = []
    t0 = pl.multiple_of(step * GLA_CHUNK, GLA_CHUNK)
    for b in range(BATCH):
        for l in range(SLAB):
            h = hbm.at[b, pl.ds(t0, GLA_CHUNK), pl.ds(l * LANES, LANES)]
            v = buf.at[slot, l, pl.ds(b * XPITCH, GLA_CHUNK)]
            copies.append(pltpu.make_async_copy(v, h, sem.at[slot]) if to_hbm
                          else pltpu.make_async_copy(h, v, sem.at[slot]))
    return copies


def _start_all(copies):
    for c in copies:
        c.start()


def _wait_all(copies):
    for c in copies:
        c.wait()


def _ada_kernel(c_ref, w_ref, b_ref, o_ref):
    c = c_ref[...]
    s = c * _sigmoid(c)
    o_ref[...] = jnp.dot(s, w_ref[...], preferred_element_type=F32,
                         precision=lax.Precision.HIGHEST) + b_ref[...]


def _ada(c, w, b):
    n_out = w.shape[1]
    tn = 1024
    return pl.pallas_call(
        _ada_kernel,
        grid=(n_out // tn,),
        in_specs=[pl.BlockSpec((BATCH, D_MODEL), lambda j: (0, 0)),
                  pl.BlockSpec((D_MODEL, tn), lambda j: (0, j)),
                  pl.BlockSpec((1, tn), lambda j: (0, j))],
        out_specs=pl.BlockSpec((BATCH, tn), lambda j: (0, j)),
        out_shape=jax.ShapeDtypeStruct((BATCH, n_out), F32),
        name="ada",
    )(c, w, b.reshape(1, n_out))


def _mixer_kernel(x_ref, sc_ref, sh_ref, gt_ref, gmix_ref, win_ref, cw_ref, cb_ref,
                  wr_ref, br_ref, wi_ref, bi_ref, lam_ref, wa2_ref, ba_ref, gn_ref, wout_ref,
                  o_ref,
                  xbuf, a_s, u_s, hs_s, hc_s, la_s, bc_s, st_s, cm_s, xin, x_s, xsem):
    i = pl.program_id(0)
    nt = GLA_CHUNK
    cur = lax.rem(i, 2)

    @pl.when(i == 0)
    def _first_fetch():
        _start_all(_seq_major_copies(x_ref, xin, xsem, 0, 0, False))

    @pl.when(i + 1 < pl.num_programs(0))
    def _prefetch():
        _start_all(_seq_major_copies(x_ref, xin, xsem, i + 1, 1 - cur, False))

    @pl.when(i == 0)
    def _init():
        xbuf[0:HALO, :] = jnp.zeros((HALO, D_LRU), F32)
        hc_s[...] = jnp.zeros_like(hc_s)
        st_s[...] = jnp.zeros_like(st_s)
        r = lax.broadcasted_iota(jnp.int32, (ROWS, ROWS), 0)
        c = lax.broadcasted_iota(jnp.int32, (ROWS, ROWS), 1)
        same_seq = (r & (BATCH - 1)) == (c & (BATCH - 1))
        cm_s[...] = jnp.where(same_seq & ((c >> 3) <= (r >> 3)), 1.0, 0.0).astype(F32)

    _wait_all(_seq_major_copies(x_ref, xin, xsem, i, cur, False))
    for t in range(nt):
        for l in range(SLAB):
            x_s[t * BATCH:(t + 1) * BATCH, l * LANES:(l + 1) * LANES] = (
                xin[cur, l, pl.ds(t, BATCH, stride=XPITCH), :])
    x = x_s[...]
    y = _rms(x) * gmix_ref[...]
    h = (y * _slab_bcast(sc_ref[...], nt) + _slab_bcast(sh_ref[...], nt)).astype(BF16)

    def proj(c0, c1):
        return _dot(h, win_ref[:, c0:c1])

    xbuf[HALO:HALO + ROWS, :] = proj(C_LX, C_LY)
    cw = cw_ref[...]
    cx = cb_ref[...] + sum(cw[k:k + 1, :] * xbuf[k * BATCH:k * BATCH + ROWS, :]
                           for k in range(CONV_WIDTH))
    xbuf[0:HALO, :] = xbuf[ROWS:ROWS + HALO, :]
    cxb = cx.astype(BF16)
    r_gate = _sigmoid(_dot(cxb, wr_ref[...]) + br_ref[...])
    i_gate = _sigmoid(_dot(cxb, wi_ref[...]) + bi_ref[...])
    log_a = (-LRU_C) * r_gate * _softplus(-lam_ref[...])
    a_s[...] = jnp.exp(log_a)
    th = jnp.tanh(log_a)
    u_s[...] = jnp.sqrt(-2.0 * th / (1.0 - th)) * (i_gate * cx)
    hcur = hc_s[...]
    for t in range(nt):
        sl = slice(t * BATCH, (t + 1) * BATCH)
        hcur = a_s[sl, :] * hcur + u_s[sl, :]
        hs_s[sl, :] = hcur
    hc_s[...] = hcur
    lru_out = hs_s[...] * jax.nn.gelu(proj(C_LY, C_Q), approximate=True)

    gate_lr = proj(C_GL, D_IN_PAD).astype(BF16)
    z = _dot(gate_lr, wa2_ref[...]) + ba_ref[...]
    la_s[...] = -_softplus(-z) * (1.0 / GLA_GATE_NORM)
    bcur = jnp.zeros((BATCH, QK), F32)
    for t in range(nt):
        sl = slice(t * BATCH, (t + 1) * BATCH)
        bcur = bcur + la_s[sl, :]
        bc_s[sl, :] = bcur
    bc = bc_s[...]
    e_last = jnp.exp(bcur)
    q_dec = proj(C_Q, C_K) * (GLA_DK ** -0.5) * jnp.exp(bc)
    kk = proj(C_K, C_V)
    k_dec = kk * jnp.exp(-bc)
    k_last = kk * jnp.exp(_slab_bcast(bcur, nt) - bc)
    vv = proj(C_V, C_G)
    gg = proj(C_G, C_GL)
    causal = cm_s[...] > 0.5

    lane = lax.broadcasted_iota(jnp.int32, (SUBLANES, LANES), 1)
    sub = lax.broadcasted_iota(jnp.int32, (SUBLANES, LANES), 0)
    half = lane >> 6
    seq_sel = [(sub == 2 * j + half).astype(F32) for j in range(BATCH // 2)]

    def expand(m):
        n = m.shape[0] // SUBLANES
        m3 = m.reshape(n, SUBLANES, LANES)
        return jnp.concatenate([(m3 * s[None]).reshape(m.shape) for s in seq_sel], axis=1)

    def both_halves(m, hh):
        keep = (lax.broadcasted_iota(jnp.int32, m.shape, 1) >> 6) == hh
        mh = jnp.where(keep, m, 0.0)
        return mh, mh + pltpu.roll(mh, GLA_DK, axis=1)

    gla_parts = []
    for hd in range(GLA_HEADS):
        p, hh = hd // 2, hd % 2
        pc = slice(p * LANES, (p + 1) * LANES)
        qh, q_both = both_halves(q_dec[:, pc], hh)
        _, k_both = both_halves(k_last[:, pc], hh)
        _, e_both = both_halves(e_last[:, pc], hh)
        v_h = vv[:, hd * GLA_DV:(hd + 1) * GLA_DV]
        v_hb = v_h.astype(BF16)
        scores = _dot_nt(qh.astype(BF16), k_dec[:, pc].astype(BF16))
        scores = jnp.where(causal, scores, 0.0)
        o_h = _dot(scores.astype(BF16), v_hb)
        st = st_s[hd]
        o_h = o_h + _dot_nt(expand(q_both).astype(BF16), st.astype(BF16))
        kv_t = _dot(v_h.T.astype(BF16), expand(k_both).astype(BF16))
        decay = jnp.concatenate(
            [jnp.sum(e_both * s, axis=0, keepdims=True) for s in seq_sel], axis=1)
        st_s[hd] = st * decay + kv_t
        o_n = _rms(o_h) * gn_ref[...]
        g_h = gg[:, hd * GLA_DV:(hd + 1) * GLA_DV]
        gla_parts.append(o_n * (g_h * _sigmoid(g_h)))

    mix_in = jnp.concatenate([lru_out] + gla_parts, axis=1).astype(BF16)
    mix = _dot(mix_in, wout_ref[...])
    o_ref[...] = x + _slab_bcast(gt_ref[...], nt) * mix


def _const_spec(shape):
    nd = len(shape)
    return pl.BlockSpec(shape, lambda i: (0,) * nd)


def _mixer(x, sc1p, sh1, gt1, g_mix, w_in_p, conv_w, conv_b, wr_d, br, wi_d, bi, lam,
           wa2_p, ba, gn, w_out_b):
    n = x.shape[0] * x.shape[1]
    consts = [sc1p, sh1, gt1, g_mix, w_in_p, conv_w, conv_b, wr_d, br, wi_d, bi, lam,
              wa2_p, ba, gn, w_out_b]
    return pl.pallas_call(
        _mixer_kernel,
        grid=(n // ROWS,),
        in_specs=[pl.BlockSpec(memory_space=pl.ANY)]
        + [_const_spec(a.shape) for a in consts],
        out_specs=pl.BlockSpec((ROWS, D_MODEL), lambda i: (i, 0)),
        out_shape=jax.ShapeDtypeStruct((n, D_MODEL), F32),
        scratch_shapes=[
            pltpu.VMEM((HALO + ROWS, D_LRU), F32),
            pltpu.VMEM((ROWS, D_LRU), F32),
            pltpu.VMEM((ROWS, D_LRU), F32),
            pltpu.VMEM((ROWS, D_LRU), F32),
            pltpu.VMEM((BATCH, D_LRU), F32),
            pltpu.VMEM((ROWS, QK), F32),
            pltpu.VMEM((ROWS, QK), F32),
            pltpu.VMEM((GLA_HEADS, GLA_DV, BATCH * GLA_DK), F32),
            pltpu.VMEM((ROWS, ROWS), F32),
            pltpu.VMEM((2, SLAB, BATCH * XPITCH, LANES), F32),
            pltpu.VMEM((ROWS, D_MODEL), F32),
            pltpu.SemaphoreType.DMA((2,)),
        ],
        compiler_params=pltpu.CompilerParams(dimension_semantics=("arbitrary",),
                                             vmem_limit_bytes=VMEM_LIMIT_BYTES),
        name="mixer",
    )(x, *consts)


def _router_kernel(x_ref, sc_ref, sh_ref, gffn_ref, wr_ref, br_ref,
                   h2_ref, meta_ref, metat_ref, cnt_ref, base_s, tri_s):
    i = pl.program_id(0)
    nt = ROUTE_TILE // BATCH

    @pl.when(i == 0)
    def _init():
        base_s[...] = jnp.zeros_like(base_s)
        r = lax.broadcasted_iota(jnp.int32, (ROUTE_TILE, ROUTE_TILE), 0)
        c = lax.broadcasted_iota(jnp.int32, (ROUTE_TILE, ROUTE_TILE), 1)
        tri_s[...] = jnp.where(c < r, 1.0, 0.0).astype(BF16)

    x = x_ref[...]
    h2 = _rms(x) * gffn_ref[...] * _slab_bcast(sc_ref[...], nt) + _slab_bcast(sh_ref[...], nt)
    _pack_rows(h2_ref, h2, ROUTE_TILE)
    logits = _dot(h2.astype(BF16), wr_ref[...]) + br_ref[...]
    lane = lax.broadcasted_iota(jnp.int32, logits.shape, 1)
    neg = jnp.float32(-jnp.inf)

    def first_max(vals):
        m = jnp.max(vals, axis=-1, keepdims=True)
        idx = jnp.min(jnp.where(vals == m, lane, LANES), axis=-1, keepdims=True)
        return m, idx

    cl = jnp.where(lane < N_GROUPS, logits, neg)
    cmax, grp = first_max(cl)
    p_grp = 1.0 / jnp.sum(jnp.exp(cl - cmax), axis=-1, keepdims=True)
    lo = N_GROUPS + EXPERTS_PER_GROUP * grp
    fl = jnp.where((lane >= lo) & (lane < lo + EXPERTS_PER_GROUP), logits, neg)
    f1, i1 = first_max(fl)
    f2, i2 = first_max(jnp.where(lane == i1, neg, fl))
    z = jnp.sum(jnp.exp(fl - f1), axis=-1, keepdims=True)
    tp1 = 1.0 / z
    tp2 = jnp.exp(f2 - f1) / z
    w1 = p_grp * tp1 / (tp1 + tp2)
    w2 = p_grp * tp2 / (tp1 + tp2)

    hit1 = lane == i1
    hit2 = lane == i2
    assign = jnp.where(hit1 | hit2, 1.0, 0.0)
    before = _dot(tri_s[...], assign.astype(BF16)) + base_s[...]
    rank1 = jnp.sum(jnp.where(hit1, before, 0.0), axis=-1, keepdims=True)
    rank2 = jnp.sum(jnp.where(hit2, before, 0.0), axis=-1, keepdims=True)
    base_s[...] = base_s[...] + jnp.sum(assign, axis=0, keepdims=True)
    cnt_ref[...] = jnp.broadcast_to(base_s[...], cnt_ref.shape)

    cols = [(i1 - N_GROUPS).astype(F32), (i2 - N_GROUPS).astype(F32), rank1, rank2, w1, w2]
    meta = jnp.zeros(logits.shape, F32)
    for j, col in enumerate(cols):
        meta = jnp.where(lane == j, col, meta)
    meta_ref[...] = meta
    metat_ref[...] = meta.T[0:SUBLANES, :]


def _router(x1, sc2p, sh2, g_ffn, w_r, b_r):
    n = x1.shape[0]
    consts = [sc2p, sh2, g_ffn, w_r, b_r]
    return pl.pallas_call(
        _router_kernel,
        grid=(n // ROUTE_TILE,),
        in_specs=[pl.BlockSpec((ROUTE_TILE, D_MODEL), lambda i: (i, 0))]
        + [_const_spec(a.shape) for a in consts],
        out_specs=[pl.BlockSpec((ROUTE_TILE * PACK, LANES), lambda i: (i, 0)),
                   pl.BlockSpec((ROUTE_TILE, LANES), lambda i: (i, 0)),
                   pl.BlockSpec((SUBLANES, ROUTE_TILE), lambda i: (0, i)),
                   pl.BlockSpec((SUBLANES, LANES), lambda i: (0, 0))],
        out_shape=[jax.ShapeDtypeStruct((n * PACK, LANES), jnp.uint32),
                   jax.ShapeDtypeStruct((n, LANES), F32),
                   jax.ShapeDtypeStruct((SUBLANES, n), F32),
                   jax.ShapeDtypeStruct((SUBLANES, LANES), F32)],
        scratch_shapes=[pltpu.VMEM((1, LANES), F32),
                        pltpu.VMEM((ROUTE_TILE, ROUTE_TILE), BF16)],
        compiler_params=pltpu.CompilerParams(dimension_semantics=("arbitrary",),
                                             vmem_limit_bytes=VMEM_LIMIT_BYTES),
        name="router",
    )(x1, *consts)


def _slots_kernel(offs_ref, mt_ref, dest_ref):
    mt = mt_ref[...]
    e = mt[0:2, :].astype(jnp.int32)
    acc = mt[2:4, :].astype(jnp.int32)
    for j in range(N_EXPERTS):
        acc = acc + jnp.where(e == j, offs_ref[j], 0)
    dest_ref[...] = acc


def _slots(offs, meta_t):
    n = meta_t.shape[1]
    tl = min(n, 8192)
    grid_spec = pltpu.PrefetchScalarGridSpec(
        num_scalar_prefetch=1,
        grid=(n // tl,),
        in_specs=[pl.BlockSpec((SUBLANES, tl), lambda i, offs: (0, i))],
        out_specs=pl.BlockSpec((2, tl), lambda i, offs: (0, i)),
    )
    return pl.pallas_call(
        _slots_kernel,
        grid_spec=grid_spec,
        out_shape=jax.ShapeDtypeStruct((2, n), jnp.int32),
        name="slots",
    )(offs, meta_t)


def _row_copy(src, dst, sem):
    return pltpu.make_async_copy(src, dst, sem)


def _dispatch_kernel(nv_ref, dest_ref, h_ref, xs_ref, zbuf, hbuf, zsem, lsem, sem):
    n_blocks = xs_ref.shape[0] // MOE_BLOCK

    @pl.when(pl.program_id(0) == 0)
    def _zero_partial_blocks():
        zbuf[...] = jnp.zeros_like(zbuf)

        def fill(b):
            return _row_copy(zbuf, xs_ref.at[pl.ds(pl.multiple_of(b * MOE_BLOCK, MOE_BLOCK),
                                                   MOE_BLOCK)], zsem)

        def start(b, carry):
            @pl.when(nv_ref[b] < MOE_BLOCK)
            def _():
                fill(b).start()
            return carry

        def wait(b, carry):
            @pl.when(nv_ref[b] < MOE_BLOCK)
            def _():
                fill(b).wait()
            return carry

        lax.fori_loop(0, n_blocks, start, 0)
        lax.fori_loop(0, n_blocks, wait, 0)

    i = pl.program_id(0)
    n_steps = pl.num_programs(0)
    cur = lax.rem(i, 2)

    def load(step, slot):
        rows = pl.ds(pl.multiple_of(step * TOK_TILE, TOK_TILE), TOK_TILE)
        return _row_copy(h_ref.at[rows], hbuf.at[slot], lsem.at[slot])

    def wait_rows(slot):
        for k in range(2):
            _row_copy(hbuf.at[slot], xs_ref.at[pl.ds(0, TOK_TILE)], sem.at[slot]).wait()

    @pl.when(i == 0)
    def _first_load():
        load(0, 0).start()

    @pl.when(i > 0)
    def _drain_previous():
        wait_rows(1 - cur)

    @pl.when(i + 1 < n_steps)
    def _prefetch():
        load(i + 1, 1 - cur).start()

    load(i, cur).wait()

    def issue(r4, carry):
        for u in range(ISSUE_UNROLL):
            r = r4 * ISSUE_UNROLL + u
            for k in range(2):
                _row_copy(hbuf.at[cur, r], xs_ref.at[dest_ref[k, r]],
                          sem.at[cur]).start(priority=k)
        return carry

    lax.fori_loop(0, TOK_TILE // ISSUE_UNROLL, issue, 0)

    @pl.when(i == n_steps - 1)
    def _drain_last():
        wait_rows(cur)


def _dispatch(n_valid, dest, h2s, cap):
    n = h2s.shape[0]
    grid_spec = pltpu.PrefetchScalarGridSpec(
        num_scalar_prefetch=1,
        grid=(n // TOK_TILE,),
        in_specs=[pl.BlockSpec((2, TOK_TILE), lambda i, nv: (0, i), memory_space=pltpu.SMEM),
                  pl.BlockSpec(memory_space=pl.ANY)],
        out_specs=pl.BlockSpec(memory_space=pl.ANY),
        scratch_shapes=[pltpu.VMEM((MOE_BLOCK, PACK, LANES), jnp.uint32),
                        pltpu.VMEM((2, TOK_TILE, PACK, LANES), jnp.uint32),
                        pltpu.SemaphoreType.DMA(()),
                        pltpu.SemaphoreType.DMA((2,)),
                        pltpu.SemaphoreType.DMA((2,))],
    )
    return pl.pallas_call(
        _dispatch_kernel,
        grid_spec=grid_spec,
        out_shape=jax.ShapeDtypeStruct((cap, PACK, LANES), jnp.uint32),
        compiler_params=pltpu.CompilerParams(dimension_semantics=("arbitrary",)),
        name="dispatch",
    )(n_valid, dest, h2s)


def _slab_rows_to_matrix(ref, rows):
    return jnp.concatenate([ref[pl.ds(s, rows, stride=SLAB), :] for s in range(SLAB)], axis=1)


def _expert_kernel(be_ref, nv_ref, xs_ref, wg_ref, wu_ref, wd_ref, ys_ref):
    i = pl.program_id(0)
    nv = nv_ref[i]

    @pl.when(nv > 0)
    def _compute():
        xb = _unpack_rows(xs_ref, MOE_BLOCK).astype(BF16)
        g = _dot(xb, wg_ref[0].astype(BF16))
        u = _dot(xb, wu_ref[0].astype(BF16))
        hid = (g * _sigmoid(g) * u).astype(BF16)
        for c in range(D_MODEL // DOWN_CHUNK):
            cols = slice(c * DOWN_CHUNK, (c + 1) * DOWN_CHUNK)
            out = _dot(hid, wd_ref[0, :, cols].astype(BF16))
            for j in range(DOWN_CHUNK // LANES):
                s = c * (DOWN_CHUNK // LANES) + j
                ys_ref[pl.ds(s, MOE_BLOCK, stride=SLAB), :] = out[:, j * LANES:(j + 1) * LANES]

    @pl.when(nv == 0)
    def _skip():
        ys_ref[...] = jnp.zeros_like(ys_ref)


def _experts(block_e, n_valid, xs, wg, wu, wd):
    n_blocks = xs.shape[0] // (MOE_BLOCK * PACK)
    grid_spec = pltpu.PrefetchScalarGridSpec(
        num_scalar_prefetch=2,
        grid=(n_blocks,),
        in_specs=[pl.BlockSpec((MOE_BLOCK * PACK, LANES), lambda i, be, nv: (i, 0)),
                  pl.BlockSpec((1, D_MODEL, D_EXPERT), lambda i, be, nv: (be[i], 0, 0)),
                  pl.BlockSpec((1, D_MODEL, D_EXPERT), lambda i, be, nv: (be[i], 0, 0)),
                  pl.BlockSpec((1, D_EXPERT, D_MODEL), lambda i, be, nv: (be[i], 0, 0))],
        out_specs=pl.BlockSpec((MOE_BLOCK * SLAB, LANES), lambda i, be, nv: (i, 0)),
    )
    return pl.pallas_call(
        _expert_kernel,
        grid_spec=grid_spec,
        out_shape=jax.ShapeDtypeStruct((n_blocks * MOE_BLOCK * SLAB, LANES), F32),
        compiler_params=pltpu.CompilerParams(dimension_semantics=("arbitrary",),
                                             vmem_limit_bytes=VMEM_LIMIT_BYTES),
        name="experts",
    )(block_e, n_valid, xs, wg, wu, wd)


def _combine_kernel(dcur_ref, dnxt_ref, x_ref, meta_ref, gt_ref, gfin_ref, ys_ref, ys2d_ref,
                    o_ref, buf, sem, obuf, osem):
    i = pl.program_id(0)
    n_steps = pl.num_programs(0)
    cur = lax.rem(i, 2)

    def gather(dest_ref, slot):
        def issue(r4, carry):
            for u in range(ISSUE_UNROLL):
                r = r4 * ISSUE_UNROLL + u
                for k in range(2):
                    dst = buf.at[slot, k, pl.ds(pl.multiple_of(r * SLAB, SLAB), SLAB)]
                    _row_copy(ys_ref.at[dest_ref[k, r]], dst, sem.at[slot]).start(priority=k)
            return carry

        lax.fori_loop(0, TOK_TILE // ISSUE_UNROLL, issue, 0)

    @pl.when(i == 0)
    def _first_gather():
        gather(dcur_ref, 0)

    @pl.when(i + 1 < n_steps)
    def _prefetch():
        gather(dnxt_ref, 1 - cur)

    for k in range(2):
        _row_copy(ys2d_ref.at[pl.ds(0, TOK_TILE * SLAB)], buf.at[cur, k], sem.at[cur]).wait()

    meta = meta_ref[...]
    y = (meta[:, 4:5] * _slab_rows_to_matrix(buf.at[cur, 0], TOK_TILE)
         + meta[:, 5:6] * _slab_rows_to_matrix(buf.at[cur, 1], TOK_TILE))
    x2 = x_ref[...] + _slab_bcast(gt_ref[...], TOK_TILE // BATCH) * y
    out = _rms(x2) * gfin_ref[...]

    @pl.when(i >= 2)
    def _drain_older():
        _wait_all(_seq_major_copies(o_ref, obuf, osem, i - 2, cur, True))

    for t in range(GLA_CHUNK):
        for l in range(SLAB):
            obuf[cur, l, pl.ds(t, BATCH, stride=XPITCH), :] = (
                out[t * BATCH:(t + 1) * BATCH, l * LANES:(l + 1) * LANES])
    _start_all(_seq_major_copies(o_ref, obuf, osem, i, cur, True))

    @pl.when(i == n_steps - 1)
    def _drain_last():
        @pl.when(i >= 1)
        def _():
            _wait_all(_seq_major_copies(o_ref, obuf, osem, i - 1, 1 - cur, True))

        _wait_all(_seq_major_copies(o_ref, obuf, osem, i, cur, True))


def _combine(dest, x1, meta, gt2, g_final, ys):
    n = x1.shape[0]
    assert TOK_TILE == ROWS
    last = n // TOK_TILE - 1
    return pl.pallas_call(
        _combine_kernel,
        grid=(n // TOK_TILE,),
        in_specs=[pl.BlockSpec((2, TOK_TILE), lambda i: (0, i), memory_space=pltpu.SMEM),
                  pl.BlockSpec((2, TOK_TILE), lambda i: (0, jnp.minimum(i + 1, last)),
                               memory_space=pltpu.SMEM),
                  pl.BlockSpec((TOK_TILE, D_MODEL), lambda i: (i, 0)),
                  pl.BlockSpec((TOK_TILE, LANES), lambda i: (i, 0)),
                  _const_spec(gt2.shape),
                  _const_spec(g_final.shape),
                  pl.BlockSpec(memory_space=pl.ANY),
                  pl.BlockSpec(memory_space=pl.ANY)],
        out_specs=pl.BlockSpec(memory_space=pl.ANY),
        out_shape=jax.ShapeDtypeStruct((BATCH, n // BATCH, D_MODEL), F32),
        scratch_shapes=[pltpu.VMEM((2, 2, TOK_TILE * SLAB, LANES), F32),
                        pltpu.SemaphoreType.DMA((2,)),
                        pltpu.VMEM((2, SLAB, BATCH * XPITCH, LANES), F32),
                        pltpu.SemaphoreType.DMA((2,))],
        compiler_params=pltpu.CompilerParams(dimension_semantics=("arbitrary",),
                                             vmem_limit_bytes=VMEM_LIMIT_BYTES),
        name="combine",
    )(dest, dest, x1, meta, gt2, g_final, ys.reshape(-1, SLAB, LANES), ys)


def _block_diag(w):
    h, d, _ = w.shape
    eye = jnp.eye(h, dtype=w.dtype)
    return (eye[:, None, :, None] * w[:, :, None, :]).reshape(h * d, h * d)


def kernel(x, c, w_ada, b_ada, g_mix, g_ffn, g_final, w_in, conv_w, conv_b, lru_wr, lru_br,
           lru_wi, lru_bi, lru_lambda, gla_wa2, gla_ba, gla_gnorm, w_out, w_coarse, b_coarse,
           w_fine, b_fine, w_gate, w_up, w_down):
    bsz, seq, d = x.shape
    assert bsz == BATCH and d == D_MODEL and seq % GLA_CHUNK == 0
    assert w_ada.shape[0] == 1, "single-layer problem"
    n = bsz * seq
    row = lambda v: v.reshape(1, -1)

    mod = _ada(c, w_ada[0], b_ada[0])
    sh1, sc1, gt1, sh2, sc2, gt2 = jnp.split(mod, 6, axis=-1)

    w_in_p = jnp.pad(w_in[0], ((0, 0), (0, D_IN_PAD - w_in.shape[2]))).astype(BF16)
    wa2_p = jnp.pad(gla_wa2[0], ((0, LANES - GLA_GATE_RANK), (0, 0))).astype(BF16)
    x1 = _mixer(x, 1.0 + sc1, sh1, gt1, row(g_mix[0]), w_in_p, conv_w[0], row(conv_b[0]),
                _block_diag(lru_wr[0]).astype(BF16), row(lru_br[0]),
                _block_diag(lru_wi[0]).astype(BF16), row(lru_bi[0]), row(lru_lambda[0]),
                wa2_p, row(gla_ba[0]), row(gla_gnorm[0]), w_out[0].astype(BF16))

    n_route = N_GROUPS + N_EXPERTS
    w_r = jnp.pad(jnp.concatenate([w_coarse[0], w_fine[0]], axis=1),
                  ((0, 0), (0, LANES - n_route))).astype(BF16)
    b_r = jnp.pad(jnp.concatenate([b_coarse[0], b_fine[0]]), (0, LANES - n_route)).reshape(1, LANES)
    h2, meta, meta_t, cnt = _router(x1, 1.0 + sc2, sh2, row(g_ffn[0]), w_r, b_r)

    counts = cnt[0, N_GROUPS:N_GROUPS + N_EXPERTS].astype(jnp.int32)
    padded = (counts + MOE_BLOCK - 1) // MOE_BLOCK * MOE_BLOCK
    pends = jnp.cumsum(padded)
    offs = pends - padded
    cap = (2 * n + MOE_BLOCK - 1) // MOE_BLOCK * MOE_BLOCK + N_EXPERTS * MOE_BLOCK
    n_blocks = cap // MOE_BLOCK
    starts = jnp.arange(n_blocks, dtype=jnp.int32) * MOE_BLOCK
    block_e = jnp.minimum(jnp.sum((pends[None, :] <= starts[:, None]).astype(jnp.int32), axis=1),
                          N_EXPERTS - 1)
    n_valid = jnp.clip(offs[block_e] + counts[block_e] - starts, 0, MOE_BLOCK).astype(jnp.int32)
    dest = _slots(offs.astype(jnp.int32), meta_t)

    xs = _dispatch(n_valid, dest, h2.reshape(n, PACK, LANES), cap)
    ys = _experts(block_e, n_valid, xs.reshape(cap * PACK, LANES),
                  w_gate[0], w_up[0], w_down[0])
    return _combine(dest, x1, meta, gt2, row(g_final), ys)
```

```python
import functools

import jax
import jax.numpy as jnp
import numpy as np
from jax import lax
from jax.experimental import pallas as pl
from jax.experimental.pallas import tpu as pltpu

F32 = jnp.float32
BF16 = jnp.bfloat16

SUBLANES = 8
LANES = 128
VMEM_LIMIT_BYTES = 56 * 1024 * 1024

D_MODEL = 1024
BATCH = SUBLANES
D_LRU = 512
LRU_HEADS = 8
CONV_WIDTH = 4
LRU_C = 8.0
D_GLA = 512
GLA_HEADS = 4
GLA_DV = 128
GLA_DK = 64
GLA_GATE_RANK = 16
GLA_GATE_NORM = 16.0
GLA_CHUNK = 64
N_GROUPS = 4
EXPERTS_PER_GROUP = 8
N_EXPERTS = 32
D_EXPERT = 512
MOE_BLOCK = 512
EPS = 1e-6

QK = GLA_HEADS * GLA_DK
ROWS = GLA_CHUNK * BATCH
HALO = (CONV_WIDTH - 1) * BATCH
SLAB = D_MODEL // LANES
PACK = D_MODEL // (2 * LANES)
C_LX, C_LY, C_Q, C_K, C_V, C_G, C_GL = 0, 512, 1024, 1280, 1536, 2048, 2560
D_IN_PAD = 2688
ROUTE_TILE = 512
ROUTE_ROWS = N_EXPERTS + SUBLANES
TOK_TILE = 512
ISSUE_UNROLL = 4
DOWN_CHUNK = 256
XPITCH = 72


def _dot(a, b):
    return jnp.dot(a, b, preferred_element_type=F32)


def _dot_nt(a, b):
    return lax.dot_general(a, b, (((1,), (1,)), ((), ())), preferred_element_type=F32)


def _softplus(z):
    return jnp.maximum(z, 0.0) + jnp.log1p(jnp.exp(-jnp.abs(z)))


def _sigmoid(z):
    return 1.0 / (1.0 + jnp.exp(-z))


def _rms(x):
    return x * lax.rsqrt(jnp.mean(x * x, axis=-1, keepdims=True) + EPS)


def _pack_rows(ref, val, rows):
    for s in range(PACK):
        lo = val[:, (2 * s) * LANES:(2 * s + 1) * LANES].astype(BF16).astype(F32)
        hi = val[:, (2 * s + 1) * LANES:(2 * s + 2) * LANES].astype(BF16).astype(F32)
        word = (lax.bitcast_convert_type(lo, jnp.uint32) >> 16) | lax.bitcast_convert_type(
            hi, jnp.uint32)
        ref[pl.ds(s, rows, stride=PACK), :] = word


def _unpack_rows(ref, rows):
    cols = []
    for s in range(PACK):
        word = ref[pl.ds(s, rows, stride=PACK), :]
        cols.append(lax.bitcast_convert_type(word << 16, F32))
        cols.append(lax.bitcast_convert_type(word & jnp.uint32(0xFFFF0000), F32))
    return jnp.concatenate(cols, axis=1)


def _slab_bcast(v, n):
    c = v.shape[-1]
    return jnp.broadcast_to(v[None], (n, SUBLANES, c)).reshape(n * SUBLANES, c)


def _seq_major_copies(hbm, buf, sem, step, slot, to_hbm):
    copies = []
    t0 = pl.multiple_of(step * GLA_CHUNK, GLA_CHUNK)
    for b in range(BATCH):
        for l in range(SLAB):
            h = hbm.at[b, pl.ds(t0, GLA_CHUNK), pl.ds(l * LANES, LANES)]
            v = buf.at[slot, l, pl.ds(b * XPITCH, GLA_CHUNK)]
            copies.append(pltpu.make_async_copy(v, h, sem.at[slot]) if to_hbm
                          else pltpu.make_async_copy(h, v, sem.at[slot]))
    return copies


def _start_all(copies):
    for c in copies:
        c.start()


def _wait_all(copies):
    for c in copies:
        c.wait()


def _ada_kernel(c_ref, w_ref, b_ref, o_ref):
    c = c_ref[...]
    s = c * _sigmoid(c)
    o_ref[...] = jnp.dot(s, w_ref[...], preferred_element_type=F32,
                         precision=lax.Precision.HIGHEST) + b_ref[...]


def _ada(c, w, b):
    n_out = w.shape[1]
    tn = 1024
    return pl.pallas_call(
        _ada_kernel,
        grid=(n_out // tn,),
        in_specs=[pl.BlockSpec((BATCH, D_MODEL), lambda j: (0, 0)),
                  pl.BlockSpec((D_MODEL, tn), lambda j: (0, j)),
                  pl.BlockSpec((1, tn), lambda j: (0, j))],
        out_specs=pl.BlockSpec((BATCH, tn), lambda j: (0, j)),
        out_shape=jax.ShapeDtypeStruct((BATCH, n_out), F32),
        name="ada",
    )(c, w, b.reshape(1, n_out))


def _mixer_kernel(x_ref, sc_ref, sh_ref, gt_ref, gmix_ref, win_ref, cw_ref, cb_ref,
                  wr_ref, br_ref, wi_ref, bi_ref, lam_ref, wa2_ref, ba_ref, gn_ref, wout_ref,
                  o_ref,
                  xbuf, a_s, u_s, hs_s, hc_s, la_s, bc_s, st_s, cm_s, xin, x_s, xsem):
    i = pl.program_id(0)
    nt = GLA_CHUNK
    cur = lax.rem(i, 2)

    @pl.when(i == 0)
    def _first_fetch():
        _start_all(_seq_major_copies(x_ref, xin, xsem, 0, 0, False))

    @pl.when(i + 1 < pl.num_programs(0))
    def _prefetch():
        _start_all(_seq_major_copies(x_ref, xin, xsem, i + 1, 1 - cur, False))

    @pl.when(i == 0)
    def _init():
        xbuf[0:HALO, :] = jnp.zeros((HALO, D_LRU), F32)
        hc_s[...] = jnp.zeros_like(hc_s)
        st_s[...] = jnp.zeros_like(st_s)
        r = lax.broadcasted_iota(jnp.int32, (ROWS, ROWS), 0)
        c = lax.broadcasted_iota(jnp.int32, (ROWS, ROWS), 1)
        same_seq = (r & (BATCH - 1)) == (c & (BATCH - 1))
        cm_s[...] = jnp.where(same_seq & ((c >> 3) <= (r >> 3)), 1.0, 0.0).astype(F32)

    _wait_all(_seq_major_copies(x_ref, xin, xsem, i, cur, False))
    for t in range(nt):
        for l in range(SLAB):
            x_s[t * BATCH:(t + 1) * BATCH, l * LANES:(l + 1) * LANES] = (
                xin[cur, l, pl.ds(t, BATCH, stride=XPITCH), :])
    x = x_s[...]
    y = _rms(x) * gmix_ref[...]
    h = (y * _slab_bcast(sc_ref[...], nt) + _slab_bcast(sh_ref[...], nt)).astype(BF16)

    def proj(c0, c1):
        return _dot(h, win_ref[:, c0:c1])

    xbuf[HALO:HALO + ROWS, :] = proj(C_LX, C_LY)
    cw = cw_ref[...]
    cx = cb_ref[...] + sum(cw[k:k + 1, :] * xbuf[k * BATCH:k * BATCH + ROWS, :]
                           for k in range(CONV_WIDTH))
    xbuf[0:HALO, :] = xbuf[ROWS:ROWS + HALO, :]
    cxb = cx.astype(BF16)
    r_gate = _sigmoid(_dot(cxb, wr_ref[...]) + br_ref[...])
    i_gate = _sigmoid(_dot(cxb, wi_ref[...]) + bi_ref[...])
    log_a = (-LRU_C) * r_gate * _softplus(-lam_ref[...])
    a_s[...] = jnp.exp(log_a)
    th = jnp.tanh(log_a)
    u_s[...] = jnp.sqrt(-2.0 * th / (1.0 - th)) * (i_gate * cx)
    hcur = hc_s[...]
    for t in range(nt):
        sl = slice(t * BATCH, (t + 1) * BATCH)
        hcur = a_s[sl, :] * hcur + u_s[sl, :]
        hs_s[sl, :] = hcur
    hc_s[...] = hcur
    lru_out = hs_s[...] * jax.nn.gelu(proj(C_LY, C_Q), approximate=True)

    gate_lr = proj(C_GL, D_IN_PAD).astype(BF16)
    z = _dot(gate_lr, wa2_ref[...]) + ba_ref[...]
    la_s[...] = -_softplus(-z) * (1.0 / GLA_GATE_NORM)
    bcur = jnp.zeros((BATCH, QK), F32)
    for t in range(nt):
        sl = slice(t * BATCH, (t + 1) * BATCH)
        bcur = bcur + la_s[sl, :]
        bc_s[sl, :] = bcur
    bc = bc_s[...]
    e_last = jnp.exp(bcur)
    q_dec = proj(C_Q, C_K) * (GLA_DK ** -0.5) * jnp.exp(bc)
    kk = proj(C_K, C_V)
    k_dec = kk * jnp.exp(-bc)
    k_last = kk * jnp.exp(_slab_bcast(bcur, nt) - bc)
    vv = proj(C_V, C_G)
    gg = proj(C_G, C_GL)
    causal = cm_s[...] > 0.5

    lane = lax.broadcasted_iota(jnp.int32, (SUBLANES, LANES), 1)
    sub = lax.broadcasted_iota(jnp.int32, (SUBLANES, LANES), 0)
    half = lane >> 6
    seq_sel = [(sub == 2 * j + half).astype(F32) for j in range(BATCH // 2)]

    def expand(m):
        n = m.shape[0] // SUBLANES
        m3 = m.reshape(n, SUBLANES, LANES)
        return jnp.concatenate([(m3 * s[None]).reshape(m.shape) for s in seq_sel], axis=1)

    def both_halves(m, hh):
        keep = (lax.broadcasted_iota(jnp.int32, m.shape, 1) >> 6) == hh
        mh = jnp.where(keep, m, 0.0)
        return mh, mh + pltpu.roll(mh, GLA_DK, axis=1)

    gla_parts = []
    for hd in range(GLA_HEADS):
        p, hh = hd // 2, hd % 2
        pc = slice(p * LANES, (p + 1) * LANES)
        qh, q_both = both_halves(q_dec[:, pc], hh)
        _, k_both = both_halves(k_last[:, pc], hh)
        _, e_both = both_halves(e_last[:, pc], hh)
        v_h = vv[:, hd * GLA_DV:(hd + 1) * GLA_DV]
        v_hb = v_h.astype(BF16)
        scores = _dot_nt(qh.astype(BF16), k_dec[:, pc].astype(BF16))
        scores = jnp.where(causal, scores, 0.0)
        o_h = _dot(scores.astype(BF16), v_hb)
        st = st_s[hd]
        o_h = o_h + _dot_nt(expand(q_both).astype(BF16), st.astype(BF16))
        kv_t = _dot(v_h.T.astype(BF16), expand(k_both).astype(BF16))
        decay = jnp.concatenate(
            [jnp.sum(e_both * s, axis=0, keepdims=True) for s in seq_sel], axis=1)
        st_s[hd] = st * decay + kv_t
        o_n = _rms(o_h) * gn_ref[...]
        g_h = gg[:, hd * GLA_DV:(hd + 1) * GLA_DV]
        gla_parts.append(o_n * (g_h * _sigmoid(g_h)))

    mix_in = jnp.concatenate([lru_out] + gla_parts, axis=1).astype(BF16)
    mix = _dot(mix_in, wout_ref[...])
    o_ref[...] = x + _slab_bcast(gt_ref[...], nt) * mix


def _const_spec(shape):
    nd = len(shape)
    return pl.BlockSpec(shape, lambda i: (0,) * nd)


def _mixer(x, sc1p, sh1, gt1, g_mix, w_in_p, conv_w, conv_b, wr_d, br, wi_d, bi, lam,
           wa2_p, ba, gn, w_out_b):
    n = x.shape[0] * x.shape[1]
    consts = [sc1p, sh1, gt1, g_mix, w_in_p, conv_w, conv_b, wr_d, br, wi_d, bi, lam,
              wa2_p, ba, gn, w_out_b]
    return pl.pallas_call(
        _mixer_kernel,
        grid=(n // ROWS,),
        in_specs=[pl.BlockSpec(memory_space=pl.ANY)]
        + [_const_spec(a.shape) for a in consts],
        out_specs=pl.BlockSpec((ROWS, D_MODEL), lambda i: (i, 0)),
        out_shape=jax.ShapeDtypeStruct((n, D_MODEL), F32),
        scratch_shapes=[
            pltpu.VMEM((HALO + ROWS, D_LRU), F32),
            pltpu.VMEM((ROWS, D_LRU), F32),
            pltpu.VMEM((ROWS, D_LRU), F32),
            pltpu.VMEM((ROWS, D_LRU), F32),
            pltpu.VMEM((BATCH, D_LRU), F32),
            pltpu.VMEM((ROWS, QK), F32),
            pltpu.VMEM((ROWS, QK), F32),
            pltpu.VMEM((GLA_HEADS, GLA_DV, BATCH * GLA_DK), F32),
            pltpu.VMEM((ROWS, ROWS), F32),
            pltpu.VMEM((2, SLAB, BATCH * XPITCH, LANES), F32),
            pltpu.VMEM((ROWS, D_MODEL), F32),
            pltpu.SemaphoreType.DMA((2,)),
        ],
        compiler_params=pltpu.CompilerParams(dimension_semantics=("arbitrary",),
                                             vmem_limit_bytes=VMEM_LIMIT_BYTES),
        name="mixer",
    )(x, *consts)


def _router_kernel(x_ref, sc_ref, sh_ref, gffn_ref, wrt_ref, brt_ref,
                   h2_ref, meta_ref, metat_ref, cnt_ref, base_s, tri_s):
    i = pl.program_id(0)
    nt = ROUTE_TILE // BATCH

    @pl.when(i == 0)
    def _init():
        base_s[...] = jnp.zeros_like(base_s)
        r = lax.broadcasted_iota(jnp.int32, (ROUTE_TILE, ROUTE_TILE), 0)
        c = lax.broadcasted_iota(jnp.int32, (ROUTE_TILE, ROUTE_TILE), 1)
        tri_s[...] = jnp.where(r < c, 1.0, 0.0).astype(BF16)

    x = x_ref[...]
    h2 = _rms(x) * gffn_ref[...] * _slab_bcast(sc_ref[...], nt) + _slab_bcast(sh_ref[...], nt)
    _pack_rows(h2_ref, h2, ROUTE_TILE)
    logits = _dot_nt(wrt_ref[...], h2.astype(BF16)) + brt_ref[...]
    fl_all = logits[0:N_EXPERTS, :]
    cl_all = logits[N_EXPERTS:ROUTE_ROWS, :]
    neg = jnp.float32(-jnp.inf)

    def first_max(vals, rows):
        m = jnp.max(vals, axis=0, keepdims=True)
        idx = jnp.min(jnp.where(vals == m, rows, vals.shape[0]), axis=0, keepdims=True)
        return m, idx

    crow = lax.broadcasted_iota(jnp.int32, cl_all.shape, 0)
    cl = jnp.where(crow < N_GROUPS, cl_all, neg)
    cmax, grp = first_max(cl, crow)
    p_grp = 1.0 / jnp.sum(jnp.exp(cl - cmax), axis=0, keepdims=True)
    frow = lax.broadcasted_iota(jnp.int32, fl_all.shape, 0)
    fl = jnp.where((frow >> 3) == grp, fl_all, neg)
    f1, i1 = first_max(fl, frow)
    f2, i2 = first_max(jnp.where(frow == i1, neg, fl), frow)
    z = jnp.sum(jnp.exp(fl - f1), axis=0, keepdims=True)
    tp1 = 1.0 / z
    tp2 = jnp.exp(f2 - f1) / z
    w1 = p_grp * tp1 / (tp1 + tp2)
    w2 = p_grp * tp2 / (tp1 + tp2)

    hit1 = frow == i1
    hit2 = frow == i2
    assign = jnp.where(hit1 | hit2, 1.0, 0.0)
    before = _dot(assign.astype(BF16), tri_s[...]) + base_s[:, 0:1]
    rank1 = jnp.sum(jnp.where(hit1, before, 0.0), axis=0, keepdims=True)
    rank2 = jnp.sum(jnp.where(hit2, before, 0.0), axis=0, keepdims=True)
    base_s[...] = base_s[...] + jnp.sum(assign, axis=1, keepdims=True)
    cnt_ref[...] = base_s[...]

    rows = [i1.astype(F32), i2.astype(F32), rank1, rank2, w1, w2]
    meta_t = jnp.concatenate(rows + [jnp.zeros((LANES - len(rows), ROUTE_TILE), F32)], axis=0)
    metat_ref[...] = meta_t[0:SUBLANES, :]
    meta_ref[...] = meta_t.T


def _router(x1, sc2p, sh2, g_ffn, w_rt, b_rt):
    n = x1.shape[0]
    consts = [sc2p, sh2, g_ffn, w_rt, b_rt]
    return pl.pallas_call(
        _router_kernel,
        grid=(n // ROUTE_TILE,),
        in_specs=[pl.BlockSpec((ROUTE_TILE, D_MODEL), lambda i: (i, 0))]
        + [_const_spec(a.shape) for a in consts],
        out_specs=[pl.BlockSpec((ROUTE_TILE * PACK, LANES), lambda i: (i, 0)),
                   pl.BlockSpec((ROUTE_TILE, LANES), lambda i: (i, 0)),
                   pl.BlockSpec((SUBLANES, ROUTE_TILE), lambda i: (0, i)),
                   pl.BlockSpec((N_EXPERTS, LANES), lambda i: (0, 0))],
        out_shape=[jax.ShapeDtypeStruct((n * PACK, LANES), jnp.uint32),
                   jax.ShapeDtypeStruct((n, LANES), F32),
                   jax.ShapeDtypeStruct((SUBLANES, n), F32),
                   jax.ShapeDtypeStruct((N_EXPERTS, LANES), F32)],
        scratch_shapes=[pltpu.VMEM((N_EXPERTS, LANES), F32),
                        pltpu.VMEM((ROUTE_TILE, ROUTE_TILE), BF16)],
        compiler_params=pltpu.CompilerParams(dimension_semantics=("arbitrary",),
                                             vmem_limit_bytes=VMEM_LIMIT_BYTES),
        name="router",
    )(x1, *consts)


def _slots_kernel(offs_ref, mt_ref, dest_ref):
    mt = mt_ref[...]
    e = mt[0:2, :].astype(jnp.int32)
    acc = mt[2:4, :].astype(jnp.int32)
    for j in range(N_EXPERTS):
        acc = acc + jnp.where(e == j, offs_ref[j], 0)
    dest_ref[...] = acc


def _slots(offs, meta_t):
    n = meta_t.shape[1]
    tl = min(n, 8192)
    grid_spec = pltpu.PrefetchScalarGridSpec(
        num_scalar_prefetch=1,
        grid=(n // tl,),
        in_specs=[pl.BlockSpec((SUBLANES, tl), lambda i, offs: (0, i))],
        out_specs=pl.BlockSpec((2, tl), lambda i, offs: (0, i)),
    )
    return pl.pallas_call(
        _slots_kernel,
        grid_spec=grid_spec,
        out_shape=jax.ShapeDtypeStruct((2, n), jnp.int32),
        name="slots",
    )(offs, meta_t)


def _row_copy(src, dst, sem):
    return pltpu.make_async_copy(src, dst, sem)


def _dispatch_kernel(nv_ref, dest_ref, h_ref, xs_ref, zbuf, hbuf, zsem, lsem, sem):
    n_blocks = xs_ref.shape[0] // MOE_BLOCK

    @pl.when(pl.program_id(0) == 0)
    def _zero_partial_blocks():
        zbuf[...] = jnp.zeros_like(zbuf)

        def fill(b):
            return _row_copy(zbuf, xs_ref.at[pl.ds(pl.multiple_of(b * MOE_BLOCK, MOE_BLOCK),
                                                   MOE_BLOCK)], zsem)

        def start(b, carry):
            @pl.when(nv_ref[b] < MOE_BLOCK)
            def _():
                fill(b).start()
            return carry

        def wait(b, carry):
            @pl.when(nv_ref[b] < MOE_BLOCK)
            def _():
                fill(b).wait()
            return carry

        lax.fori_loop(0, n_blocks, start, 0)
        lax.fori_loop(0, n_blocks, wait, 0)

    i = pl.program_id(0)
    n_steps = pl.num_programs(0)
    cur = lax.rem(i, 2)

    def load(step, slot):
        rows = pl.ds(pl.multiple_of(step * TOK_TILE, TOK_TILE), TOK_TILE)
        return _row_copy(h_ref.at[rows], hbuf.at[slot], lsem.at[slot])

    def wait_rows(slot):
        for k in range(2):
            _row_copy(hbuf.at[slot], xs_ref.at[pl.ds(0, TOK_TILE)], sem.at[slot]).wait()

    @pl.when(i == 0)
    def _first_load():
        load(0, 0).start()

    @pl.when(i > 0)
    def _drain_previous():
        wait_rows(1 - cur)

    @pl.when(i + 1 < n_steps)
    def _prefetch():
        load(i + 1, 1 - cur).start()

    load(i, cur).wait()

    def issue(r4, carry):
        for u in range(ISSUE_UNROLL):
            r = r4 * ISSUE_UNROLL + u
            for k in range(2):
                _row_copy(hbuf.at[cur, r], xs_ref.at[dest_ref[k, r]],
                          sem.at[cur]).start(priority=k)
        return carry

    lax.fori_loop(0, TOK_TILE // ISSUE_UNROLL, issue, 0)

    @pl.when(i == n_steps - 1)
    def _drain_last():
        wait_rows(cur)


def _dispatch(n_valid, dest, h2s, cap):
    n = h2s.shape[0]
    grid_spec = pltpu.PrefetchScalarGridSpec(
        num_scalar_prefetch=1,
        grid=(n // TOK_TILE,),
        in_specs=[pl.BlockSpec((2, TOK_TILE), lambda i, nv: (0, i), memory_space=pltpu.SMEM),
                  pl.BlockSpec(memory_space=pl.ANY)],
        out_specs=pl.BlockSpec(memory_space=pl.ANY),
        scratch_shapes=[pltpu.VMEM((MOE_BLOCK, PACK, LANES), jnp.uint32),
                        pltpu.VMEM((2, TOK_TILE, PACK, LANES), jnp.uint32),
                        pltpu.SemaphoreType.DMA(()),
                        pltpu.SemaphoreType.DMA((2,)),
                        pltpu.SemaphoreType.DMA((2,))],
    )
    return pl.pallas_call(
        _dispatch_kernel,
        grid_spec=grid_spec,
        out_shape=jax.ShapeDtypeStruct((cap, PACK, LANES), jnp.uint32),
        compiler_params=pltpu.CompilerParams(dimension_semantics=("arbitrary",)),
        name="dispatch",
    )(n_valid, dest, h2s)


def _slab_rows_to_matrix(ref, rows):
    return jnp.concatenate([ref[pl.ds(s, rows, stride=SLAB), :] for s in range(SLAB)], axis=1)


def _expert_kernel(be_ref, nv_ref, xs_ref, wg_ref, wu_ref, wd_ref, ys_ref):
    i = pl.program_id(0)
    nv = nv_ref[i]

    @pl.when(nv > 0)
    def _compute():
        xb = _unpack_rows(xs_ref, MOE_BLOCK).astype(BF16)
        g = _dot(xb, wg_ref[0].astype(BF16))
        u = _dot(xb, wu_ref[0].astype(BF16))
        hid = (g * _sigmoid(g) * u).astype(BF16)
        for c in range(D_MODEL // DOWN_CHUNK):
            cols = slice(c * DOWN_CHUNK, (c + 1) * DOWN_CHUNK)
            out = _dot(hid, wd_ref[0, :, cols].astype(BF16))
            for j in range(DOWN_CHUNK // LANES):
                s = c * (DOWN_CHUNK // LANES) + j
                ys_ref[pl.ds(s, MOE_BLOCK, stride=SLAB), :] = out[:, j * LANES:(j + 1) * LANES]

    @pl.when(nv == 0)
    def _skip():
        ys_ref[...] = jnp.zeros_like(ys_ref)


def _experts(block_e, n_valid, xs, wg, wu, wd):
    n_blocks = xs.shape[0] // (MOE_BLOCK * PACK)
    grid_spec = pltpu.PrefetchScalarGridSpec(
        num_scalar_prefetch=2,
        grid=(n_blocks,),
        in_specs=[pl.BlockSpec((MOE_BLOCK * PACK, LANES), lambda i, be, nv: (i, 0)),
                  pl.BlockSpec((1, D_MODEL, D_EXPERT), lambda i, be, nv: (be[i], 0, 0)),
                  pl.BlockSpec((1, D_MODEL, D_EXPERT), lambda i, be, nv: (be[i], 0, 0)),
                  pl.BlockSpec((1, D_EXPERT, D_MODEL), lambda i, be, nv: (be[i], 0, 0))],
        out_specs=pl.BlockSpec((MOE_BLOCK * SLAB, LANES), lambda i, be, nv: (i, 0)),
    )
    return pl.pallas_call(
        _expert_kernel,
        grid_spec=grid_spec,
        out_shape=jax.ShapeDtypeStruct((n_blocks * MOE_BLOCK * SLAB, LANES), F32),
        compiler_params=pltpu.CompilerParams(dimension_semantics=("arbitrary",),
                                             vmem_limit_bytes=VMEM_LIMIT_BYTES),
        name="experts",
    )(block_e, n_valid, xs, wg, wu, wd)


def _combine_kernel(dcur_ref, dnxt_ref, x_ref, meta_ref, gt_ref, gfin_ref, ys_ref, ys2d_ref,
                    o_ref, buf, sem, obuf, osem):
    i = pl.program_id(0)
    n_steps = pl.num_programs(0)
    cur = lax.rem(i, 2)

    def gather(dest_ref, slot):
        def issue(r4, carry):
            for u in range(ISSUE_UNROLL):
                r = r4 * ISSUE_UNROLL + u
                for k in range(2):
                    dst = buf.at[slot, k, pl.ds(pl.multiple_of(r * SLAB, SLAB), SLAB)]
                    _row_copy(ys_ref.at[dest_ref[k, r]], dst, sem.at[slot]).start(priority=k)
            return carry

        lax.fori_loop(0, TOK_TILE // ISSUE_UNROLL, issue, 0)

    @pl.when(i == 0)
    def _first_gather():
        gather(dcur_ref, 0)

    @pl.when(i + 1 < n_steps)
    def _prefetch():
        gather(dnxt_ref, 1 - cur)

    for k in range(2):
        _row_copy(ys2d_ref.at[pl.ds(0, TOK_TILE * SLAB)], buf.at[cur, k], sem.at[cur]).wait()

    meta = meta_ref[...]
    y = (meta[:, 4:5] * _slab_rows_to_matrix(buf.at[cur, 0], TOK_TILE)
         + meta[:, 5:6] * _slab_rows_to_matrix(buf.at[cur, 1], TOK_TILE))
    x2 = x_ref[...] + _slab_bcast(gt_ref[...], TOK_TILE // BATCH) * y
    out = _rms(x2) * gfin_ref[...]

    @pl.when(i >= 2)
    def _drain_older():
        _wait_all(_seq_major_copies(o_ref, obuf, osem, i - 2, cur, True))

    for t in range(GLA_CHUNK):
        for l in range(SLAB):
            obuf[cur, l, pl.ds(t, BATCH, stride=XPITCH), :] = (
                out[t * BATCH:(t + 1) * BATCH, l * LANES:(l + 1) * LANES])
    _start_all(_seq_major_copies(o_ref, obuf, osem, i, cur, True))

    @pl.when(i == n_steps - 1)
    def _drain_last():
        @pl.when(i >= 1)
        def _():
            _wait_all(_seq_major_copies(o_ref, obuf, osem, i - 1, 1 - cur, True))

        _wait_all(_seq_major_copies(o_ref, obuf, osem, i, cur, True))


def _combine(dest, x1, meta, gt2, g_final, ys):
    n = x1.shape[0]
    assert TOK_TILE == ROWS
    last = n // TOK_TILE - 1
    return pl.pallas_call(
        _combine_kernel,
        grid=(n // TOK_TILE,),
        in_specs=[pl.BlockSpec((2, TOK_TILE), lambda i: (0, i), memory_space=pltpu.SMEM),
                  pl.BlockSpec((2, TOK_TILE), lambda i: (0, jnp.minimum(i + 1, last)),
                               memory_space=pltpu.SMEM),
                  pl.BlockSpec((TOK_TILE, D_MODEL), lambda i: (i, 0)),
                  pl.BlockSpec((TOK_TILE, LANES), lambda i: (i, 0)),
                  _const_spec(gt2.shape),
                  _const_spec(g_final.shape),
                  pl.BlockSpec(memory_space=pl.ANY),
                  pl.BlockSpec(memory_space=pl.ANY)],
        out_specs=pl.BlockSpec(memory_space=pl.ANY),
        out_shape=jax.ShapeDtypeStruct((BATCH, n // BATCH, D_MODEL), F32),
        scratch_shapes=[pltpu.VMEM((2, 2, TOK_TILE * SLAB, LANES), F32),
                        pltpu.SemaphoreType.DMA((2,)),
                        pltpu.VMEM((2, SLAB, BATCH * XPITCH, LANES), F32),
                        pltpu.SemaphoreType.DMA((2,))],
        compiler_params=pltpu.CompilerParams(dimension_semantics=("arbitrary",),
                                             vmem_limit_bytes=VMEM_LIMIT_BYTES),
        name="combine",
    )(dest, dest, x1, meta, gt2, g_final, ys.reshape(-1, SLAB, LANES), ys)


def _block_diag(w):
    h, d, _ = w.shape
    eye = jnp.eye(h, dtype=w.dtype)
    return (eye[:, None, :, None] * w[:, :, None, :]).reshape(h * d, h * d)


def kernel(x, c, w_ada, b_ada, g_mix, g_ffn, g_final, w_in, conv_w, conv_b, lru_wr, lru_br,
           lru_wi, lru_bi, lru_lambda, gla_wa2, gla_ba, gla_gnorm, w_out, w_coarse, b_coarse,
           w_fine, b_fine, w_gate, w_up, w_down):
    bsz, seq, d = x.shape
    assert bsz == BATCH and d == D_MODEL and seq % GLA_CHUNK == 0
    assert w_ada.shape[0] == 1, "single-layer problem"
    n = bsz * seq
    row = lambda v: v.reshape(1, -1)

    mod = _ada(c, w_ada[0], b_ada[0])
    sh1, sc1, gt1, sh2, sc2, gt2 = jnp.split(mod, 6, axis=-1)

    w_in_p = jnp.pad(w_in[0], ((0, 0), (0, D_IN_PAD - w_in.shape[2]))).astype(BF16)
    wa2_p = jnp.pad(gla_wa2[0], ((0, LANES - GLA_GATE_RANK), (0, 0))).astype(BF16)
    x1 = _mixer(x, 1.0 + sc1, sh1, gt1, row(g_mix[0]), w_in_p, conv_w[0], row(conv_b[0]),
                _block_diag(lru_wr[0]).astype(BF16), row(lru_br[0]),
                _block_diag(lru_wi[0]).astype(BF16), row(lru_bi[0]), row(lru_lambda[0]),
                wa2_p, row(gla_ba[0]), row(gla_gnorm[0]), w_out[0].astype(BF16))

    pad_rows = ROUTE_ROWS - N_EXPERTS - N_GROUPS
    w_rt = jnp.pad(jnp.concatenate([w_fine[0], w_coarse[0]], axis=1).T,
                   ((0, pad_rows), (0, 0))).astype(BF16)
    b_rt = jnp.pad(jnp.concatenate([b_fine[0], b_coarse[0]]), (0, pad_rows)).reshape(-1, 1)
    h2, meta, meta_t, cnt = _router(x1, 1.0 + sc2, sh2, row(g_ffn[0]), w_rt, b_rt)

    counts = cnt[:, 0].astype(jnp.int32)
    padded = (counts + MOE_BLOCK - 1) // MOE_BLOCK * MOE_BLOCK
    pends = jnp.cumsum(padded)
    offs = pends - padded
    cap = (2 * n + MOE_BLOCK - 1) // MOE_BLOCK * MOE_BLOCK + N_EXPERTS * MOE_BLOCK
    n_blocks = cap // MOE_BLOCK
    starts = jnp.arange(n_blocks, dtype=jnp.int32) * MOE_BLOCK
    block_e = jnp.minimum(jnp.sum((pends[None, :] <= starts[:, None]).astype(jnp.int32), axis=1),
                          N_EXPERTS - 1)
    n_valid = jnp.clip(offs[block_e] + counts[block_e] - starts, 0, MOE_BLOCK).astype(jnp.int32)
    dest = _slots(offs.astype(jnp.int32), meta_t)

    xs = _dispatch(n_valid, dest, h2.reshape(n, PACK, LANES), cap)
    ys = _experts(block_e, n_valid, xs.reshape(cap * PACK, LANES),
                  w_gate[0], w_up[0], w_down[0])
    return _combine(dest, x1, meta, gt2, row(g_final), ys)
```

```python
import functools

import jax
import jax.numpy as jnp
import numpy as np
from jax import lax
from jax.experimental import pallas as pl
from jax.experimental.pallas import tpu as pltpu

F32 = jnp.float32
BF16 = jnp.bfloat16

SUBLANES = 8
LANES = 128
VMEM_LIMIT_BYTES = 56 * 1024 * 1024

D_MODEL = 1024
BATCH = SUBLANES
D_LRU = 512
LRU_HEADS = 8
CONV_WIDTH = 4
LRU_C = 8.0
D_GLA = 512
GLA_HEADS = 4
GLA_DV = 128
GLA_DK = 64
GLA_GATE_RANK = 16
GLA_GATE_NORM = 16.0
GLA_CHUNK = 64
N_GROUPS = 4
EXPERTS_PER_GROUP = 8
N_EXPERTS = 32
D_EXPERT = 512
MOE_BLOCK = 512
EPS = 1e-6

QK = GLA_HEADS * GLA_DK
ROWS = GLA_CHUNK * BATCH
HALO = (CONV_WIDTH - 1) * BATCH
SLAB = D_MODEL // LANES
PACK = D_MODEL // (2 * LANES)
C_LX, C_LY, C_Q, C_K, C_V, C_G, C_GL = 0, 512, 1024, 1280, 1536, 2048, 2560
D_IN_PAD = 2688
ROUTE_TILE = 512
ROUTE_ROWS = N_EXPERTS + SUBLANES
TOK_TILE = 512
ISSUE_UNROLL = 4
DOWN_CHUNK = 256
PROJ_CHUNK = 256
XPITCH = 72


def _dot(a, b):
    return jnp.dot(a, b, preferred_element_type=F32)


def _dot_nt(a, b):
    return lax.dot_general(a, b, (((1,), (1,)), ((), ())), preferred_element_type=F32)


def _softplus(z):
    return jnp.maximum(z, 0.0) + jnp.log1p(jnp.exp(-jnp.abs(z)))


def _sigmoid(z):
    return 1.0 / (1.0 + jnp.exp(-z))


def _rms(x):
    return x * lax.rsqrt(jnp.mean(x * x, axis=-1, keepdims=True) + EPS)


def _pack_rows(ref, val, rows):
    for s in range(PACK):
        lo = val[:, (2 * s) * LANES:(2 * s + 1) * LANES].astype(BF16).astype(F32)
        hi = val[:, (2 * s + 1) * LANES:(2 * s + 2) * LANES].astype(BF16).astype(F32)
        word = (lax.bitcast_convert_type(lo, jnp.uint32) >> 16) | lax.bitcast_convert_type(
            hi, jnp.uint32)
        ref[pl.ds(s, rows, stride=PACK), :] = word


def _unpack_rows(ref, rows):
    cols = []
    for s in range(PACK):
        word = ref[pl.ds(s, rows, stride=PACK), :]
        cols.append(lax.bitcast_convert_type(word << 16, F32))
        cols.append(lax.bitcast_convert_type(word & jnp.uint32(0xFFFF0000), F32))
    return jnp.concatenate(cols, axis=1)


def _slab_bcast(v, n):
    c = v.shape[-1]
    return jnp.broadcast_to(v[None], (n, SUBLANES, c)).reshape(n * SUBLANES, c)


def _seq_major_copies(hbm, buf, sem, step, to_hbm):
    copies = []
    t0 = pl.multiple_of(step * GLA_CHUNK, GLA_CHUNK)
    for b in range(BATCH):
        for l in range(SLAB):
            h = hbm.at[b, pl.ds(t0, GLA_CHUNK), pl.ds(l * LANES, LANES)]
            v = buf.at[l, pl.ds(b * XPITCH, GLA_CHUNK)]
            copies.append(pltpu.make_async_copy(v, h, sem) if to_hbm
                          else pltpu.make_async_copy(h, v, sem))
    return copies


def _start_all(copies):
    for c in copies:
        c.start()


def _wait_all(copies):
    for c in copies:
        c.wait()


def _ada_kernel(c_ref, w_ref, b_ref, o_ref):
    c = c_ref[...]
    s = c * _sigmoid(c)
    o_ref[...] = jnp.dot(s, w_ref[...], preferred_element_type=F32,
                         precision=lax.Precision.HIGHEST) + b_ref[...]


def _ada(c, w, b):
    n_out = w.shape[1]
    tn = 1024
    return pl.pallas_call(
        _ada_kernel,
        grid=(n_out // tn,),
        in_specs=[pl.BlockSpec((BATCH, D_MODEL), lambda j: (0, 0)),
                  pl.BlockSpec((D_MODEL, tn), lambda j: (0, j)),
                  pl.BlockSpec((1, tn), lambda j: (0, j))],
        out_specs=pl.BlockSpec((BATCH, tn), lambda j: (0, j)),
        out_shape=jax.ShapeDtypeStruct((BATCH, n_out), F32),
        name="ada",
    )(c, w, b.reshape(1, n_out))


def _mixer_kernel(x_ref, sc_ref, sh_ref, gt_ref, gmix_ref, win_ref, cw_ref, cb_ref,
                  wr_ref, br_ref, wi_ref, bi_ref, lam_ref, wa2_ref, ba_ref, gn_ref, wout_ref,
                  o_ref,
                  xbuf, a_s, u_s, hs_s, hc_s, la_s, bc_s, st_s, cm_s, sel_s, hb_s,
                  xin_a, xin_b, x_a, x_b, p_a, p_b, xsem):
    j = pl.program_id(0)
    n_tiles = 2 * pl.num_programs(0)
    nt = GLA_CHUNK
    side = ((xin_a, x_a, p_a), (xin_b, x_b, p_b))

    def copies(tile, sd):
        return _seq_major_copies(x_ref, side[sd][0], xsem.at[sd], tile, False)

    def normalise(sd):
        xin, x_s, _ = side[sd]
        for t in range(nt):
            for l in range(SLAB):
                x_s[t * BATCH:(t + 1) * BATCH, l * LANES:(l + 1) * LANES] = (
                    xin[l, pl.ds(t, BATCH, stride=XPITCH), :])
        y = _rms(x_s[...]) * gmix_ref[...]
        hb_s[...] = (y * _slab_bcast(sc_ref[...], nt) + _slab_bcast(sh_ref[...], nt)).astype(BF16)

    def project(sd, c0, c1):
        side[sd][2][:, c0:c1] = _dot(hb_s[...], win_ref[:, c0:c1])

    def input_half(sd):
        pieces = [lambda: normalise(sd)]
        for c0 in range(0, D_IN_PAD, PROJ_CHUNK):
            pieces.append(functools.partial(project, sd, c0, min(c0 + PROJ_CHUNK, D_IN_PAD)))
        return pieces

    def recurrent_half(sd, out_rows, fill):
        _, x_s, p_s = side[sd]
        fill = list(fill)

        def next_piece():
            if fill:
                fill.pop(0)()

        def proj(c0, c1):
            return p_s[:, c0:c1]

        next_piece()
        xbuf[HALO:HALO + ROWS, :] = proj(C_LX, C_LY)
        cw = cw_ref[...]
        cx = cb_ref[...] + sum(cw[k:k + 1, :] * xbuf[k * BATCH:k * BATCH + ROWS, :]
                               for k in range(CONV_WIDTH))
        xbuf[0:HALO, :] = xbuf[ROWS:ROWS + HALO, :]
        cxb = cx.astype(BF16)
        next_piece()

        def gate(w_ref, b_ref):
            hw = D_LRU // 2
            pre = jnp.concatenate([_dot(cxb[:, 0:hw], w_ref[0:hw, 0:hw]),
                                   _dot(cxb[:, hw:], w_ref[hw:, hw:])], axis=1)
            return _sigmoid(pre + b_ref[...])

        r_gate = gate(wr_ref, br_ref)
        next_piece()
        i_gate = gate(wi_ref, bi_ref)
        next_piece()
        decay_rate = (-LRU_C) * _softplus(-lam_ref[...])
        for rows in (slice(0, ROWS // 2), slice(ROWS // 2, ROWS)):
            log_a = r_gate[rows] * decay_rate
            a_s[rows, :] = jnp.exp(log_a)
            th = jnp.tanh(log_a)
            u_s[rows, :] = jnp.sqrt(-2.0 * th / (1.0 - th)) * (i_gate[rows] * cx[rows])
            next_piece()
        hcur = hc_s[...]
        for t in range(nt):
            sl = slice(t * BATCH, (t + 1) * BATCH)
            hcur = a_s[sl, :] * hcur + u_s[sl, :]
            hs_s[sl, :] = hcur
        hc_s[...] = hcur
        next_piece()
        lru_out = hs_s[...] * jax.nn.gelu(proj(C_LY, C_Q), approximate=True)
        next_piece()

        gate_lr = proj(C_GL, D_IN_PAD).astype(BF16)
        z = _dot(gate_lr, wa2_ref[...]) + ba_ref[...]
        la_s[...] = -_softplus(-z) * (1.0 / GLA_GATE_NORM)
        next_piece()
        bcur = jnp.zeros((BATCH, QK), F32)
        for t in range(nt):
            sl = slice(t * BATCH, (t + 1) * BATCH)
            bcur = bcur + la_s[sl, :]
            bc_s[sl, :] = bcur
        next_piece()
        bc = bc_s[...]
        e_last = jnp.exp(bcur)
        q_dec = proj(C_Q, C_K) * (GLA_DK ** -0.5) * jnp.exp(bc)
        kk = proj(C_K, C_V)
        k_dec = kk * jnp.exp(-bc)
        k_last = kk * jnp.exp(_slab_bcast(bcur, nt) - bc)
        next_piece()
        vv = proj(C_V, C_G)
        gg = proj(C_G, C_GL)
        next_piece()
        hr = ROWS // 2
        causal_top = cm_s[0:hr, 0:hr] > 0.5
        causal_bot = cm_s[hr:ROWS, :] > 0.5

        lane = lax.broadcasted_iota(jnp.int32, (SUBLANES, LANES), 1)
        sub = lax.broadcasted_iota(jnp.int32, (SUBLANES, LANES), 0)
        half = lane >> 6
        seq_sel = [(sub == 2 * jj + half).astype(F32) for jj in range(BATCH // 2)]

        def expand(m):
            mb = m.astype(BF16)
            return jnp.concatenate([mb * sel_s[jj] for jj in range(BATCH // 2)], axis=1)

        def both_halves(m, hh):
            keep = (lax.broadcasted_iota(jnp.int32, m.shape, 1) >> 6) == hh
            mh = jnp.where(keep, m, 0.0)
            return mh, mh + pltpu.roll(mh, GLA_DK, axis=1)

        gla_parts = []
        for hd in range(GLA_HEADS):
            p, hh = hd // 2, hd % 2
            pc = slice(p * LANES, (p + 1) * LANES)
            qh, q_both = both_halves(q_dec[:, pc], hh)
            _, k_both = both_halves(k_last[:, pc], hh)
            _, e_both = both_halves(e_last[:, pc], hh)
            v_h = vv[:, hd * GLA_DV:(hd + 1) * GLA_DV]
            v_hb = v_h.astype(BF16)
            qb = qh.astype(BF16)
            kb = k_dec[:, pc].astype(BF16)
            s_top = jnp.where(causal_top, _dot_nt(qb[0:hr], kb[0:hr]), 0.0)
            s_bot = jnp.where(causal_bot, _dot_nt(qb[hr:ROWS], kb), 0.0)
            o_h = jnp.concatenate([_dot(s_top.astype(BF16), v_hb[0:hr]),
                                   _dot(s_bot.astype(BF16), v_hb)], axis=0)
            st = st_s[hd]
            o_h = o_h + _dot_nt(expand(q_both), st.astype(BF16))
            kv_t = _dot(v_h.T.astype(BF16), expand(k_both))
            decay = jnp.concatenate(
                [jnp.sum(e_both * sq, axis=0, keepdims=True) for sq in seq_sel], axis=1)
            st_s[hd] = st * decay + kv_t
            o_n = _rms(o_h) * gn_ref[...]
            g_h = gg[:, hd * GLA_DV:(hd + 1) * GLA_DV]
            gla_parts.append(o_n * (g_h * _sigmoid(g_h)))

        mix_in = jnp.concatenate([lru_out] + gla_parts, axis=1).astype(BF16)
        mix = _dot(mix_in, wout_ref[...])
        o_ref[out_rows, :] = x_s[...] + _slab_bcast(gt_ref[...], nt) * mix
        assert not fill

    @pl.when(j == 0)
    def _init():
        _start_all(copies(0, 0))
        _start_all(copies(1, 1))
        xbuf[0:HALO, :] = jnp.zeros((HALO, D_LRU), F32)
        hc_s[...] = jnp.zeros_like(hc_s)
        st_s[...] = jnp.zeros_like(st_s)
        r = lax.broadcasted_iota(jnp.int32, (ROWS, ROWS), 0)
        c = lax.broadcasted_iota(jnp.int32, (ROWS, ROWS), 1)
        same_seq = (r & (BATCH - 1)) == (c & (BATCH - 1))
        cm_s[...] = jnp.where(same_seq & ((c >> 3) <= (r >> 3)), 1.0, 0.0).astype(F32)
        rr = lax.broadcasted_iota(jnp.int32, (ROWS, LANES), 0) & (BATCH - 1)
        ll = lax.broadcasted_iota(jnp.int32, (ROWS, LANES), 1) >> 6
        for jj in range(BATCH // 2):
            sel_s[jj] = jnp.where(rr == 2 * jj + ll, 1.0, 0.0).astype(BF16)
        _wait_all(copies(0, 0))
        for piece in input_half(0):
            piece()

    @pl.when(2 * j + 2 < n_tiles)
    def _fetch_next_even():
        _start_all(copies(2 * j + 2, 0))

    _wait_all(copies(2 * j + 1, 1))
    recurrent_half(0, slice(0, ROWS), input_half(1))

    @pl.when(2 * j + 2 < n_tiles)
    def _next_even_arrived():
        _wait_all(copies(2 * j + 2, 0))

    @pl.when(2 * j + 3 < n_tiles)
    def _next_odd():
        _start_all(copies(2 * j + 3, 1))

    recurrent_half(1, slice(ROWS, 2 * ROWS), input_half(0))


def _const_spec(shape):
    nd = len(shape)
    return pl.BlockSpec(shape, lambda i: (0,) * nd)


def _mixer(x, sc1p, sh1, gt1, g_mix, w_in_p, conv_w, conv_b, wr_d, br, wi_d, bi, lam,
           wa2_p, ba, gn, w_out_b):
    n = x.shape[0] * x.shape[1]
    consts = [sc1p, sh1, gt1, g_mix, w_in_p, conv_w, conv_b, wr_d, br, wi_d, bi, lam,
              wa2_p, ba, gn, w_out_b]
    seq_buf = pltpu.VMEM((SLAB, BATCH * XPITCH, LANES), F32)
    return pl.pallas_call(
        _mixer_kernel,
        grid=(n // (2 * ROWS),),
        in_specs=[pl.BlockSpec(memory_space=pl.ANY)]
        + [_const_spec(a.shape) for a in consts],
        out_specs=pl.BlockSpec((2 * ROWS, D_MODEL), lambda i: (i, 0)),
        out_shape=jax.ShapeDtypeStruct((n, D_MODEL), F32),
        scratch_shapes=[
            pltpu.VMEM((HALO + ROWS, D_LRU), F32),
            pltpu.VMEM((ROWS, D_LRU), F32),
            pltpu.VMEM((ROWS, D_LRU), F32),
            pltpu.VMEM((ROWS, D_LRU), F32),
            pltpu.VMEM((BATCH, D_LRU), F32),
            pltpu.VMEM((ROWS, QK), F32),
            pltpu.VMEM((ROWS, QK), F32),
            pltpu.VMEM((GLA_HEADS, GLA_DV, BATCH * GLA_DK), F32),
            pltpu.VMEM((ROWS, ROWS), F32),
            pltpu.VMEM((BATCH // 2, ROWS, LANES), BF16),
            pltpu.VMEM((ROWS, D_MODEL), BF16),
            seq_buf, seq_buf,
            pltpu.VMEM((ROWS, D_MODEL), F32),
            pltpu.VMEM((ROWS, D_MODEL), F32),
            pltpu.VMEM((ROWS, D_IN_PAD), F32),
            pltpu.VMEM((ROWS, D_IN_PAD), F32),
            pltpu.SemaphoreType.DMA((2,)),
        ],
        compiler_params=pltpu.CompilerParams(dimension_semantics=("arbitrary",),
                                             vmem_limit_bytes=VMEM_LIMIT_BYTES),
        name="mixer",
    )(x, *consts)


def _router_kernel(x_ref, sc_ref, sh_ref, gffn_ref, wrt_ref, brt_ref,
                   h2_ref, meta_ref, metat_ref, cnt_ref, base_s, tri_s):
    i = pl.program_id(0)
    nt = ROUTE_TILE // BATCH

    @pl.when(i == 0)
    def _init():
        base_s[...] = jnp.zeros_like(base_s)
        r = lax.broadcasted_iota(jnp.int32, (ROUTE_TILE, ROUTE_TILE), 0)
        c = lax.broadcasted_iota(jnp.int32, (ROUTE_TILE, ROUTE_TILE), 1)
        tri_s[...] = jnp.where(r < c, 1.0, 0.0).astype(BF16)

    x = x_ref[...]
    h2 = _rms(x) * gffn_ref[...] * _slab_bcast(sc_ref[...], nt) + _slab_bcast(sh_ref[...], nt)
    _pack_rows(h2_ref, h2, ROUTE_TILE)
    logits = _dot_nt(wrt_ref[...], h2.astype(BF16)) + brt_ref[...]
    fl_all = logits[0:N_EXPERTS, :]
    cl_all = logits[N_EXPERTS:ROUTE_ROWS, :]
    neg = jnp.float32(-jnp.inf)

    def first_max(vals, rows):
        m = jnp.max(vals, axis=0, keepdims=True)
        idx = jnp.min(jnp.where(vals == m, rows, vals.shape[0]), axis=0, keepdims=True)
        return m, idx

    crow = lax.broadcasted_iota(jnp.int32, cl_all.shape, 0)
    cl = jnp.where(crow < N_GROUPS, cl_all, neg)
    cmax, grp = first_max(cl, crow)
    p_grp = 1.0 / jnp.sum(jnp.exp(cl - cmax), axis=0, keepdims=True)
    frow = lax.broadcasted_iota(jnp.int32, fl_all.shape, 0)
    fl = jnp.where((frow >> 3) == grp, fl_all, neg)
    f1, i1 = first_max(fl, frow)
    f2, i2 = first_max(jnp.where(frow == i1, neg, fl), frow)
    z = jnp.sum(jnp.exp(fl - f1), axis=0, keepdims=True)
    tp1 = 1.0 / z
    tp2 = jnp.exp(f2 - f1) / z
    w1 = p_grp * tp1 / (tp1 + tp2)
    w2 = p_grp * tp2 / (tp1 + tp2)

    hit1 = frow == i1
    hit2 = frow == i2
    assign = jnp.where(hit1 | hit2, 1.0, 0.0)
    before = _dot(assign.astype(BF16), tri_s[...]) + base_s[:, 0:1]
    rank1 = jnp.sum(jnp.where(hit1, before, 0.0), axis=0, keepdims=True)
    rank2 = jnp.sum(jnp.where(hit2, before, 0.0), axis=0, keepdims=True)
    base_s[...] = base_s[...] + jnp.sum(assign, axis=1, keepdims=True)
    cnt_ref[...] = base_s[...]

    rows = [i1.astype(F32), i2.astype(F32), rank1, rank2, w1, w2]
    meta_t = jnp.concatenate(rows + [jnp.zeros((LANES - len(rows), ROUTE_TILE), F32)], axis=0)
    metat_ref[...] = meta_t[0:SUBLANES, :]
    meta_ref[...] = meta_t.T


def _router(x1, sc2p, sh2, g_ffn, w_rt, b_rt):
    n = x1.shape[0]
    consts = [sc2p, sh2, g_ffn, w_rt, b_rt]
    return pl.pallas_call(
        _router_kernel,
        grid=(n // ROUTE_TILE,),
        in_specs=[pl.BlockSpec((ROUTE_TILE, D_MODEL), lambda i: (i, 0))]
        + [_const_spec(a.shape) for a in consts],
        out_specs=[pl.BlockSpec((ROUTE_TILE * PACK, LANES), lambda i: (i, 0)),
                   pl.BlockSpec((ROUTE_TILE, LANES), lambda i: (i, 0)),
                   pl.BlockSpec((SUBLANES, ROUTE_TILE), lambda i: (0, i)),
                   pl.BlockSpec((N_EXPERTS, LANES), lambda i: (0, 0))],
        out_shape=[jax.ShapeDtypeStruct((n * PACK, LANES), jnp.uint32),
                   jax.ShapeDtypeStruct((n, LANES), F32),
                   jax.ShapeDtypeStruct((SUBLANES, n), F32),
                   jax.ShapeDtypeStruct((N_EXPERTS, LANES), F32)],
        scratch_shapes=[pltpu.VMEM((N_EXPERTS, LANES), F32),
                        pltpu.VMEM((ROUTE_TILE, ROUTE_TILE), BF16)],
        compiler_params=pltpu.CompilerParams(dimension_semantics=("arbitrary",),
                                             vmem_limit_bytes=VMEM_LIMIT_BYTES),
        name="router",
    )(x1, *consts)


def _slots_kernel(offs_ref, mt_ref, dest_ref):
    mt = mt_ref[...]
    e = mt[0:2, :].astype(jnp.int32)
    acc = mt[2:4, :].astype(jnp.int32)
    for j in range(N_EXPERTS):
        acc = acc + jnp.where(e == j, offs_ref[j], 0)
    dest_ref[...] = acc


def _slots(offs, meta_t):
    n = meta_t.shape[1]
    tl = min(n, 8192)
    grid_spec = pltpu.PrefetchScalarGridSpec(
        num_scalar_prefetch=1,
        grid=(n // tl,),
        in_specs=[pl.BlockSpec((SUBLANES, tl), lambda i, offs: (0, i))],
        out_specs=pl.BlockSpec((2, tl), lambda i, offs: (0, i)),
    )
    return pl.pallas_call(
        _slots_kernel,
        grid_spec=grid_spec,
        out_shape=jax.ShapeDtypeStruct((2, n), jnp.int32),
        name="slots",
    )(offs, meta_t)


def _row_copy(src, dst, sem):
    return pltpu.make_async_copy(src, dst, sem)


def _dispatch_kernel(nv_ref, dest_ref, h_ref, xs_ref, zbuf, hbuf, zsem, lsem, sem):
    n_blocks = xs_ref.shape[0] // MOE_BLOCK

    @pl.when(pl.program_id(0) == 0)
    def _zero_partial_blocks():
        zbuf[...] = jnp.zeros_like(zbuf)

        def fill(b):
            return _row_copy(zbuf, xs_ref.at[pl.ds(pl.multiple_of(b * MOE_BLOCK, MOE_BLOCK),
                                                   MOE_BLOCK)], zsem)

        def start(b, carry):
            @pl.when(nv_ref[b] < MOE_BLOCK)
            def _():
                fill(b).start()
            return carry

        def wait(b, carry):
            @pl.when(nv_ref[b] < MOE_BLOCK)
            def _():
                fill(b).wait()
            return carry

        lax.fori_loop(0, n_blocks, start, 0)
        lax.fori_loop(0, n_blocks, wait, 0)

    i = pl.program_id(0)
    n_steps = pl.num_programs(0)
    cur = lax.rem(i, 2)

    def load(step, slot):
        rows = pl.ds(pl.multiple_of(step * TOK_TILE, TOK_TILE), TOK_TILE)
        return _row_copy(h_ref.at[rows], hbuf.at[slot], lsem.at[slot])

    def wait_rows(slot):
        for k in range(2):
            _row_copy(hbuf.at[slot], xs_ref.at[pl.ds(0, TOK_TILE)], sem.at[slot]).wait()

    @pl.when(i == 0)
    def _first_load():
        load(0, 0).start()

    @pl.when(i > 0)
    def _drain_previous():
        wait_rows(1 - cur)

    @pl.when(i + 1 < n_steps)
    def _prefetch():
        load(i + 1, 1 - cur).start()

    load(i, cur).wait()

    def issue(r4, carry):
        for u in range(ISSUE_UNROLL):
            r = r4 * ISSUE_UNROLL + u
            for k in range(2):
                _row_copy(hbuf.at[cur, r], xs_ref.at[dest_ref[k, r]],
                          sem.at[cur]).start(priority=k)
        return carry

    lax.fori_loop(0, TOK_TILE // ISSUE_UNROLL, issue, 0)

    @pl.when(i == n_steps - 1)
    def _drain_last():
        wait_rows(cur)


def _dispatch(n_valid, dest, h2s, cap):
    n = h2s.shape[0]
    grid_spec = pltpu.PrefetchScalarGridSpec(
        num_scalar_prefetch=1,
        grid=(n // TOK_TILE,),
        in_specs=[pl.BlockSpec((2, TOK_TILE), lambda i, nv: (0, i), memory_space=pltpu.SMEM),
                  pl.BlockSpec(memory_space=pl.ANY)],
        out_specs=pl.BlockSpec(memory_space=pl.ANY),
        scratch_shapes=[pltpu.VMEM((MOE_BLOCK, PACK, LANES), jnp.uint32),
                        pltpu.VMEM((2, TOK_TILE, PACK, LANES), jnp.uint32),
                        pltpu.SemaphoreType.DMA(()),
                        pltpu.SemaphoreType.DMA((2,)),
                        pltpu.SemaphoreType.DMA((2,))],
    )
    return pl.pallas_call(
        _dispatch_kernel,
        grid_spec=grid_spec,
        out_shape=jax.ShapeDtypeStruct((cap, PACK, LANES), jnp.uint32),
        compiler_params=pltpu.CompilerParams(dimension_semantics=("arbitrary",)),
        name="dispatch",
    )(n_valid, dest, h2s)


def _slab_rows_to_matrix(ref, rows):
    return jnp.concatenate([ref[pl.ds(s, rows, stride=SLAB), :] for s in range(SLAB)], axis=1)


def _expert_kernel(be_ref, nv_ref, xs_ref, wg_ref, wu_ref, wd_ref, ys_ref):
    i = pl.program_id(0)
    nv = nv_ref[i]

    @pl.when(nv > 0)
    def _compute():
        xb = _unpack_rows(xs_ref, MOE_BLOCK).astype(BF16)
        g = _dot(xb, wg_ref[0].astype(BF16))
        u = _dot(xb, wu_ref[0].astype(BF16))
        hid = (g * _sigmoid(g) * u).astype(BF16)
        for c in range(D_MODEL // DOWN_CHUNK):
            cols = slice(c * DOWN_CHUNK, (c + 1) * DOWN_CHUNK)
            out = _dot(hid, wd_ref[0, :, cols].astype(BF16))
            for j in range(DOWN_CHUNK // LANES):
                s = c * (DOWN_CHUNK // LANES) + j
                ys_ref[pl.ds(s, MOE_BLOCK, stride=SLAB), :] = out[:, j * LANES:(j + 1) * LANES]

    @pl.when(nv == 0)
    def _skip():
        ys_ref[...] = jnp.zeros_like(ys_ref)


def _experts(block_e, n_valid, xs, wg, wu, wd):
    n_blocks = xs.shape[0] // (MOE_BLOCK * PACK)
    grid_spec = pltpu.PrefetchScalarGridSpec(
        num_scalar_prefetch=2,
        grid=(n_blocks,),
        in_specs=[pl.BlockSpec((MOE_BLOCK * PACK, LANES), lambda i, be, nv: (i, 0)),
                  pl.BlockSpec((1, D_MODEL, D_EXPERT), lambda i, be, nv: (be[i], 0, 0)),
                  pl.BlockSpec((1, D_MODEL, D_EXPERT), lambda i, be, nv: (be[i], 0, 0)),
                  pl.BlockSpec((1, D_EXPERT, D_MODEL), lambda i, be, nv: (be[i], 0, 0))],
        out_specs=pl.BlockSpec((MOE_BLOCK * SLAB, LANES), lambda i, be, nv: (i, 0)),
    )
    return pl.pallas_call(
        _expert_kernel,
        grid_spec=grid_spec,
        out_shape=jax.ShapeDtypeStruct((n_blocks * MOE_BLOCK * SLAB, LANES), F32),
        compiler_params=pltpu.CompilerParams(dimension_semantics=("arbitrary",),
                                             vmem_limit_bytes=VMEM_LIMIT_BYTES),
        name="experts",
    )(block_e, n_valid, xs, wg, wu, wd)


def _combine_kernel(dcur_ref, dnxt_ref, x_ref, meta_ref, gt_ref, gfin_ref, ys_ref, ys2d_ref,
                    o_ref, buf, sem, obuf, osem):
    i = pl.program_id(0)
    n_steps = pl.num_programs(0)
    cur = lax.rem(i, 2)

    def gather(dest_ref, slot):
        def issue(r4, carry):
            for u in range(ISSUE_UNROLL):
                r = r4 * ISSUE_UNROLL + u
                for k in range(2):
                    dst = buf.at[slot, k, pl.ds(pl.multiple_of(r * SLAB, SLAB), SLAB)]
                    _row_copy(ys_ref.at[dest_ref[k, r]], dst, sem.at[slot]).start(priority=k)
            return carry

        lax.fori_loop(0, TOK_TILE // ISSUE_UNROLL, issue, 0)

    @pl.when(i == 0)
    def _first_gather():
        gather(dcur_ref, 0)

    @pl.when(i + 1 < n_steps)
    def _prefetch():
        gather(dnxt_ref, 1 - cur)

    for k in range(2):
        _row_copy(ys2d_ref.at[pl.ds(0, TOK_TILE * SLAB)], buf.at[cur, k], sem.at[cur]).wait()

    meta = meta_ref[...]
    y = (meta[:, 4:5] * _slab_rows_to_matrix(buf.at[cur, 0], TOK_TILE)
         + meta[:, 5:6] * _slab_rows_to_matrix(buf.at[cur, 1], TOK_TILE))
    x2 = x_ref[...] + _slab_bcast(gt_ref[...], TOK_TILE // BATCH) * y
    out = _rms(x2) * gfin_ref[...]

    @pl.when(i >= 2)
    def _drain_older():
        _wait_all(_seq_major_copies(o_ref, obuf.at[cur], osem.at[cur], i - 2, True))

    for t in range(GLA_CHUNK):
        for l in range(SLAB):
            obuf[cur, l, pl.ds(t, BATCH, stride=XPITCH), :] = (
                out[t * BATCH:(t + 1) * BATCH, l * LANES:(l + 1) * LANES])
    _start_all(_seq_major_copies(o_ref, obuf.at[cur], osem.at[cur], i, True))

    @pl.when(i == n_steps - 1)
    def _drain_last():
        @pl.when(i >= 1)
        def _():
            _wait_all(_seq_major_copies(o_ref, obuf.at[1 - cur], osem.at[1 - cur], i - 1, True))

        _wait_all(_seq_major_copies(o_ref, obuf.at[cur], osem.at[cur], i, True))


def _combine(dest, x1, meta, gt2, g_final, ys):
    n = x1.shape[0]
    assert TOK_TILE == ROWS
    last = n // TOK_TILE - 1
    return pl.pallas_call(
        _combine_kernel,
        grid=(n // TOK_TILE,),
        in_specs=[pl.BlockSpec((2, TOK_TILE), lambda i: (0, i), memory_space=pltpu.SMEM),
                  pl.BlockSpec((2, TOK_TILE), lambda i: (0, jnp.minimum(i + 1, last)),
                               memory_space=pltpu.SMEM),
                  pl.BlockSpec((TOK_TILE, D_MODEL), lambda i: (i, 0)),
                  pl.BlockSpec((TOK_TILE, LANES), lambda i: (i, 0)),
                  _const_spec(gt2.shape),
                  _const_spec(g_final.shape),
                  pl.BlockSpec(memory_space=pl.ANY),
                  pl.BlockSpec(memory_space=pl.ANY)],
        out_specs=pl.BlockSpec(memory_space=pl.ANY),
        out_shape=jax.ShapeDtypeStruct((BATCH, n // BATCH, D_MODEL), F32),
        scratch_shapes=[pltpu.VMEM((2, 2, TOK_TILE * SLAB, LANES), F32),
                        pltpu.SemaphoreType.DMA((2,)),
                        pltpu.VMEM((2, SLAB, BATCH * XPITCH, LANES), F32),
                        pltpu.SemaphoreType.DMA((2,))],
        compiler_params=pltpu.CompilerParams(dimension_semantics=("arbitrary",),
                                             vmem_limit_bytes=VMEM_LIMIT_BYTES),
        name="combine",
    )(dest, dest, x1, meta, gt2, g_final, ys.reshape(-1, SLAB, LANES), ys)


def _block_diag(w):
    h, d, _ = w.shape
    eye = jnp.eye(h, dtype=w.dtype)
    return (eye[:, None, :, None] * w[:, :, None, :]).reshape(h * d, h * d)


def kernel(x, c, w_ada, b_ada, g_mix, g_ffn, g_final, w_in, conv_w, conv_b, lru_wr, lru_br,
           lru_wi, lru_bi, lru_lambda, gla_wa2, gla_ba, gla_gnorm, w_out, w_coarse, b_coarse,
           w_fine, b_fine, w_gate, w_up, w_down):
    bsz, seq, d = x.shape
    assert bsz == BATCH and d == D_MODEL and seq % (2 * GLA_CHUNK) == 0
    assert w_ada.shape[0] == 1, "single-layer problem"
    n = bsz * seq
    row = lambda v: v.reshape(1, -1)

    mod = _ada(c, w_ada[0], b_ada[0])
    sh1, sc1, gt1, sh2, sc2, gt2 = jnp.split(mod, 6, axis=-1)

    w_in_p = jnp.pad(w_in[0], ((0, 0), (0, D_IN_PAD - w_in.shape[2]))).astype(BF16)
    wa2_p = jnp.pad(gla_wa2[0], ((0, LANES - GLA_GATE_RANK), (0, 0))).astype(BF16)
    x1 = _mixer(x, 1.0 + sc1, sh1, gt1, row(g_mix[0]), w_in_p, conv_w[0], row(conv_b[0]),
                _block_diag(lru_wr[0]).astype(BF16), row(lru_br[0]),
                _block_diag(lru_wi[0]).astype(BF16), row(lru_bi[0]), row(lru_lambda[0]),
                wa2_p, row(gla_ba[0]), row(gla_gnorm[0]), w_out[0].astype(BF16))

    pad_rows = ROUTE_ROWS - N_EXPERTS - N_GROUPS
    w_rt = jnp.pad(jnp.concatenate([w_fine[0], w_coarse[0]], axis=1).T,
                   ((0, pad_rows), (0, 0))).astype(BF16)
    b_rt = jnp.pad(jnp.concatenate([b_fine[0], b_coarse[0]]), (0, pad_rows)).reshape(-1, 1)
    h2, meta, meta_t, cnt = _router(x1, 1.0 + sc2, sh2, row(g_ffn[0]), w_rt, b_rt)

    counts = cnt[:, 0].astype(jnp.int32)
    padded = (counts + MOE_BLOCK - 1) // MOE_BLOCK * MOE_BLOCK
    pends = jnp.cumsum(padded)
    offs = pends - padded
    cap = (2 * n + MOE_BLOCK - 1) // MOE_BLOCK * MOE_BLOCK + N_EXPERTS * MOE_BLOCK
    n_blocks = cap // MOE_BLOCK
    starts = jnp.arange(n_blocks, dtype=jnp.int32) * MOE_BLOCK
    block_e = jnp.minimum(jnp.sum((pends[None, :] <= starts[:, None]).astype(jnp.int32), axis=1),
                          N_EXPERTS - 1)
    n_valid = jnp.clip(offs[block_e] + counts[block_e] - starts, 0, MOE_BLOCK).astype(jnp.int32)
    dest = _slots(offs.astype(jnp.int32), meta_t)

    xs = _dispatch(n_valid, dest, h2.reshape(n, PACK, LANES), cap)
    ys = _experts(block_e, n_valid, xs.reshape(cap * PACK, LANES),
                  w_gate[0], w_up[0], w_down[0])
    return _combine(dest, x1, meta, gt2, row(g_final), ys)
```

```python
import functools

import jax
import jax.numpy as jnp
import numpy as np
from jax import lax
from jax.experimental import pallas as pl
from jax.experimental.pallas import tpu as pltpu

F32 = jnp.float32
BF16 = jnp.bfloat16

SUBLANES = 8
LANES = 128
VMEM_LIMIT_BYTES = 56 * 1024 * 1024

D_MODEL = 1024
BATCH = SUBLANES
D_LRU = 512
LRU_HEADS = 8
CONV_WIDTH = 4
LRU_C = 8.0
D_GLA = 512
GLA_HEADS = 4
GLA_DV = 128
GLA_DK = 64
GLA_GATE_RANK = 16
GLA_GATE_NORM = 16.0
GLA_CHUNK = 64
N_GROUPS = 4
EXPERTS_PER_GROUP = 8
N_EXPERTS = 32
D_EXPERT = 512
MOE_BLOCK = 512
EPS = 1e-6

QK = GLA_HEADS * GLA_DK
ROWS = GLA_CHUNK * BATCH
HALO = (CONV_WIDTH - 1) * BATCH
SLAB = D_MODEL // LANES
PACK = D_MODEL // (2 * LANES)
C_LX, C_LY, C_Q, C_K, C_V, C_G, C_GL = 0, 512, 1024, 1280, 1536, 2048, 2560
D_IN_PAD = 2688
ROUTE_TILE = 512
ROUTE_ROWS = N_EXPERTS + SUBLANES
TOK_TILE = 512
ISSUE_UNROLL = 8
DOWN_CHUNK = 256
PROJ_CHUNK = 256
XPITCH = 72


def _dot(a, b):
    return jnp.dot(a, b, preferred_element_type=F32)


def _dot_nt(a, b):
    return lax.dot_general(a, b, (((1,), (1,)), ((), ())), preferred_element_type=F32)


def _softplus(z):
    return jnp.maximum(z, 0.0) + jnp.log1p(jnp.exp(-jnp.abs(z)))


def _sigmoid(z):
    return 0.5 * jnp.tanh(0.5 * z) + 0.5


def _rms(x):
    return x * lax.rsqrt(jnp.mean(x * x, axis=-1, keepdims=True) + EPS)


def _pack_rows(ref, val, rows):
    for s in range(PACK):
        lo = val[:, (2 * s) * LANES:(2 * s + 1) * LANES].astype(BF16).astype(F32)
        hi = val[:, (2 * s + 1) * LANES:(2 * s + 2) * LANES].astype(BF16).astype(F32)
        word = (lax.bitcast_convert_type(lo, jnp.uint32) >> 16) | lax.bitcast_convert_type(
            hi, jnp.uint32)
        ref[pl.ds(s, rows, stride=PACK), :] = word


def _unpack_rows(ref, rows):
    cols = []
    for s in range(PACK):
        word = ref[pl.ds(s, rows, stride=PACK), :]
        cols.append(lax.bitcast_convert_type(word << 16, F32))
        cols.append(lax.bitcast_convert_type(word & jnp.uint32(0xFFFF0000), F32))
    return jnp.concatenate(cols, axis=1)


def _slab_bcast(v, n):
    c = v.shape[-1]
    return jnp.broadcast_to(v[None], (n, SUBLANES, c)).reshape(n * SUBLANES, c)


def _seq_major_copies(hbm, buf, sem, step, to_hbm):
    copies = []
    t0 = pl.multiple_of(step * GLA_CHUNK, GLA_CHUNK)
    for b in range(BATCH):
        for l in range(SLAB):
            h = hbm.at[b, pl.ds(t0, GLA_CHUNK), pl.ds(l * LANES, LANES)]
            v = buf.at[l, pl.ds(b * XPITCH, GLA_CHUNK)]
            copies.append(pltpu.make_async_copy(v, h, sem) if to_hbm
                          else pltpu.make_async_copy(h, v, sem))
    return copies


def _start_all(copies):
    for c in copies:
        c.start()


def _wait_all(copies):
    for c in copies:
        c.wait()


def _ada_kernel(c_ref, w_ref, b_ref, o_ref):
    c = c_ref[...]
    s = c * _sigmoid(c)
    o_ref[...] = jnp.dot(s, w_ref[...], preferred_element_type=F32,
                         precision=lax.Precision.HIGHEST) + b_ref[...]


def _ada(c, w, b):
    n_out = w.shape[1]
    tn = 1024
    return pl.pallas_call(
        _ada_kernel,
        grid=(n_out // tn,),
        in_specs=[pl.BlockSpec((BATCH, D_MODEL), lambda j: (0, 0)),
                  pl.BlockSpec((D_MODEL, tn), lambda j: (0, j)),
                  pl.BlockSpec((1, tn), lambda j: (0, j))],
        out_specs=pl.BlockSpec((BATCH, tn), lambda j: (0, j)),
        out_shape=jax.ShapeDtypeStruct((BATCH, n_out), F32),
        name="ada",
    )(c, w, b.reshape(1, n_out))


def _mixer_kernel(x_ref, sc_ref, sh_ref, gt_ref, gmix_ref, win_ref, cw_ref, cb_ref,
                  wr_ref, br_ref, wi_ref, bi_ref, lam_ref, wa2_ref, ba_ref, gn_ref, wout_ref,
                  o_ref,
                  xbuf, a_s, u_s, hs_s, hc_s, la_s, bc_s, st_s, cm_s, sel_s, hb_s,
                  xin_a, xin_b, x_a, x_b, p_a, p_b, xsem):
    j = pl.program_id(0)
    n_tiles = 2 * pl.num_programs(0)
    nt = GLA_CHUNK
    side = ((xin_a, x_a, p_a), (xin_b, x_b, p_b))

    def copies(tile, sd):
        return _seq_major_copies(x_ref, side[sd][0], xsem.at[sd], tile, False)

    def normalise(sd):
        xin, x_s, _ = side[sd]
        for t in range(nt):
            for l in range(SLAB):
                x_s[t * BATCH:(t + 1) * BATCH, l * LANES:(l + 1) * LANES] = (
                    xin[l, pl.ds(t, BATCH, stride=XPITCH), :])
        y = _rms(x_s[...]) * gmix_ref[...]
        hb_s[...] = (y * _slab_bcast(sc_ref[...], nt) + _slab_bcast(sh_ref[...], nt)).astype(BF16)

    def project(sd, c0, c1):
        side[sd][2][:, c0:c1] = _dot(hb_s[...], win_ref[:, c0:c1])

    def input_half(sd):
        pieces = [lambda: normalise(sd)]
        for c0 in range(0, D_IN_PAD, PROJ_CHUNK):
            pieces.append(functools.partial(project, sd, c0, min(c0 + PROJ_CHUNK, D_IN_PAD)))
        return pieces

    def recurrent_half(sd, out_rows, fill):
        _, x_s, p_s = side[sd]
        fill = list(fill)

        def next_piece():
            if fill:
                fill.pop(0)()

        def proj(c0, c1):
            return p_s[:, c0:c1]

        next_piece()
        xbuf[HALO:HALO + ROWS, :] = proj(C_LX, C_LY)
        cw = cw_ref[...]
        cx = cb_ref[...] + sum(cw[k:k + 1, :] * xbuf[k * BATCH:k * BATCH + ROWS, :]
                               for k in range(CONV_WIDTH))
        xbuf[0:HALO, :] = xbuf[ROWS:ROWS + HALO, :]
        cxb = cx.astype(BF16)
        next_piece()

        def gate(w_ref, b_ref):
            hw = D_LRU // 2
            pre = jnp.concatenate([_dot(cxb[:, 0:hw], w_ref[0:hw, 0:hw]),
                                   _dot(cxb[:, hw:], w_ref[hw:, hw:])], axis=1)
            return _sigmoid(pre + b_ref[...])

        r_gate = gate(wr_ref, br_ref)
        next_piece()
        i_gate = gate(wi_ref, bi_ref)
        next_piece()
        decay_rate = (-LRU_C) * _softplus(-lam_ref[...])
        for rows in (slice(0, ROWS // 2), slice(ROWS // 2, ROWS)):
            log_a = r_gate[rows] * decay_rate
            a_s[rows, :] = jnp.exp(log_a)
            th = jnp.tanh(log_a)
            u_s[rows, :] = jnp.sqrt(-2.0 * th / (1.0 - th)) * (i_gate[rows] * cx[rows])
            next_piece()
        hcur = hc_s[...]
        for t in range(nt):
            sl = slice(t * BATCH, (t + 1) * BATCH)
            hcur = a_s[sl, :] * hcur + u_s[sl, :]
            hs_s[sl, :] = hcur
        hc_s[...] = hcur
        next_piece()
        lru_out = hs_s[...] * jax.nn.gelu(proj(C_LY, C_Q), approximate=True)
        next_piece()

        gate_lr = proj(C_GL, D_IN_PAD).astype(BF16)
        z = _dot(gate_lr, wa2_ref[...]) + ba_ref[...]
        la_s[...] = -_softplus(-z) * (1.0 / GLA_GATE_NORM)
        next_piece()
        bcur = jnp.zeros((BATCH, QK), F32)
        for t in range(nt):
            sl = slice(t * BATCH, (t + 1) * BATCH)
            bcur = bcur + la_s[sl, :]
            bc_s[sl, :] = bcur
        next_piece()
        bc = bc_s[...]
        e_last = jnp.exp(bcur)
        q_dec = proj(C_Q, C_K) * (GLA_DK ** -0.5) * jnp.exp(bc)
        kk = proj(C_K, C_V)
        k_dec = kk * jnp.exp(-bc)
        k_last = kk * jnp.exp(_slab_bcast(bcur, nt) - bc)
        next_piece()
        vv = proj(C_V, C_G)
        gg = proj(C_G, C_GL)
        next_piece()
        hr = ROWS // 2
        causal_top = cm_s[0:hr, 0:hr] > 0.5
        causal_bot = cm_s[hr:ROWS, :] > 0.5

        lane = lax.broadcasted_iota(jnp.int32, (SUBLANES, LANES), 1)
        sub = lax.broadcasted_iota(jnp.int32, (SUBLANES, LANES), 0)
        half = lane >> 6
        seq_sel = [(sub == 2 * jj + half).astype(F32) for jj in range(BATCH // 2)]

        def expand(m):
            mb = m.astype(BF16)
            return jnp.concatenate([mb * sel_s[jj] for jj in range(BATCH // 2)], axis=1)

        def both_halves(m, hh):
            keep = (lax.broadcasted_iota(jnp.int32, m.shape, 1) >> 6) == hh
            mh = jnp.where(keep, m, 0.0)
            return mh, mh + pltpu.roll(mh, GLA_DK, axis=1)

        gla_parts = []
        for hd in range(GLA_HEADS):
            p, hh = hd // 2, hd % 2
            pc = slice(p * LANES, (p + 1) * LANES)
            qh, q_both = both_halves(q_dec[:, pc], hh)
            _, k_both = both_halves(k_last[:, pc], hh)
            _, e_both = both_halves(e_last[:, pc], hh)
            v_h = vv[:, hd * GLA_DV:(hd + 1) * GLA_DV]
            v_hb = v_h.astype(BF16)
            qb = qh.astype(BF16)
            kb = k_dec[:, pc].astype(BF16)
            s_top = jnp.where(causal_top, _dot_nt(qb[0:hr], kb[0:hr]), 0.0)
            s_bot = jnp.where(causal_bot, _dot_nt(qb[hr:ROWS], kb), 0.0)
            o_h = jnp.concatenate([_dot(s_top.astype(BF16), v_hb[0:hr]),
                                   _dot(s_bot.astype(BF16), v_hb)], axis=0)
            st = st_s[hd]
            o_h = o_h + _dot_nt(expand(q_both), st.astype(BF16))
            kv_t = _dot(v_h.T.astype(BF16), expand(k_both))
            decay = jnp.concatenate(
                [jnp.sum(e_both * sq, axis=0, keepdims=True) for sq in seq_sel], axis=1)
            st_s[hd] = st * decay + kv_t
            o_n = _rms(o_h) * gn_ref[...]
            g_h = gg[:, hd * GLA_DV:(hd + 1) * GLA_DV]
            gla_parts.append(o_n * (g_h * _sigmoid(g_h)))

        mix_in = jnp.concatenate([lru_out] + gla_parts, axis=1).astype(BF16)
        mix = _dot(mix_in, wout_ref[...])
        o_ref[out_rows, :] = x_s[...] + _slab_bcast(gt_ref[...], nt) * mix
        assert not fill

    @pl.when(j == 0)
    def _init():
        _start_all(copies(0, 0))
        _start_all(copies(1, 1))
        xbuf[0:HALO, :] = jnp.zeros((HALO, D_LRU), F32)
        hc_s[...] = jnp.zeros_like(hc_s)
        st_s[...] = jnp.zeros_like(st_s)
        r = lax.broadcasted_iota(jnp.int32, (ROWS, ROWS), 0)
        c = lax.broadcasted_iota(jnp.int32, (ROWS, ROWS), 1)
        same_seq = (r & (BATCH - 1)) == (c & (BATCH - 1))
        cm_s[...] = jnp.where(same_seq & ((c >> 3) <= (r >> 3)), 1.0, 0.0).astype(F32)
        rr = lax.broadcasted_iota(jnp.int32, (ROWS, LANES), 0) & (BATCH - 1)
        ll = lax.broadcasted_iota(jnp.int32, (ROWS, LANES), 1) >> 6
        for jj in range(BATCH // 2):
            sel_s[jj] = jnp.where(rr == 2 * jj + ll, 1.0, 0.0).astype(BF16)
        _wait_all(copies(0, 0))
        for piece in input_half(0):
            piece()

    @pl.when(2 * j + 2 < n_tiles)
    def _fetch_next_even():
        _start_all(copies(2 * j + 2, 0))

    _wait_all(copies(2 * j + 1, 1))
    recurrent_half(0, slice(0, ROWS), input_half(1))

    @pl.when(2 * j + 2 < n_tiles)
    def _next_even_arrived():
        _wait_all(copies(2 * j + 2, 0))

    @pl.when(2 * j + 3 < n_tiles)
    def _next_odd():
        _start_all(copies(2 * j + 3, 1))

    recurrent_half(1, slice(ROWS, 2 * ROWS), input_half(0))


def _const_spec(shape):
    nd = len(shape)
    return pl.BlockSpec(shape, lambda i: (0,) * nd)


def _mixer(x, sc1p, sh1, gt1, g_mix, w_in_p, conv_w, conv_b, wr_d, br, wi_d, bi, lam,
           wa2_p, ba, gn, w_out_b):
    n = x.shape[0] * x.shape[1]
    consts = [sc1p, sh1, gt1, g_mix, w_in_p, conv_w, conv_b, wr_d, br, wi_d, bi, lam,
              wa2_p, ba, gn, w_out_b]
    seq_buf = pltpu.VMEM((SLAB, BATCH * XPITCH, LANES), F32)
    return pl.pallas_call(
        _mixer_kernel,
        grid=(n // (2 * ROWS),),
        in_specs=[pl.BlockSpec(memory_space=pl.ANY)]
        + [_const_spec(a.shape) for a in consts],
        out_specs=pl.BlockSpec((2 * ROWS, D_MODEL), lambda i: (i, 0)),
        out_shape=jax.ShapeDtypeStruct((n, D_MODEL), F32),
        scratch_shapes=[
            pltpu.VMEM((HALO + ROWS, D_LRU), F32),
            pltpu.VMEM((ROWS, D_LRU), F32),
            pltpu.VMEM((ROWS, D_LRU), F32),
            pltpu.VMEM((ROWS, D_LRU), F32),
            pltpu.VMEM((BATCH, D_LRU), F32),
            pltpu.VMEM((ROWS, QK), F32),
            pltpu.VMEM((ROWS, QK), F32),
            pltpu.VMEM((GLA_HEADS, GLA_DV, BATCH * GLA_DK), F32),
            pltpu.VMEM((ROWS, ROWS), F32),
            pltpu.VMEM((BATCH // 2, ROWS, LANES), BF16),
            pltpu.VMEM((ROWS, D_MODEL), BF16),
            seq_buf, seq_buf,
            pltpu.VMEM((ROWS, D_MODEL), F32),
            pltpu.VMEM((ROWS, D_MODEL), F32),
            pltpu.VMEM((ROWS, D_IN_PAD), F32),
            pltpu.VMEM((ROWS, D_IN_PAD), F32),
            pltpu.SemaphoreType.DMA((2,)),
        ],
        compiler_params=pltpu.CompilerParams(dimension_semantics=("arbitrary",),
                                             vmem_limit_bytes=VMEM_LIMIT_BYTES),
        name="mixer",
    )(x, *consts)


def _router_kernel(x_ref, sc_ref, sh_ref, gffn_ref, wrt_ref, brt_ref,
                   h2_ref, meta_ref, metat_ref, cnt_ref, base_s, tri_s):
    i = pl.program_id(0)
    nt = ROUTE_TILE // BATCH

    @pl.when(i == 0)
    def _init():
        base_s[...] = jnp.zeros_like(base_s)
        r = lax.broadcasted_iota(jnp.int32, (ROUTE_TILE, ROUTE_TILE), 0)
        c = lax.broadcasted_iota(jnp.int32, (ROUTE_TILE, ROUTE_TILE), 1)
        tri_s[...] = jnp.where(r < c, 1.0, 0.0).astype(BF16)

    x = x_ref[...]
    h2 = _rms(x) * gffn_ref[...] * _slab_bcast(sc_ref[...], nt) + _slab_bcast(sh_ref[...], nt)
    _pack_rows(h2_ref, h2, ROUTE_TILE)
    logits = _dot_nt(wrt_ref[...], h2.astype(BF16)) + brt_ref[...]
    fl_all = logits[0:N_EXPERTS, :]
    cl_all = logits[N_EXPERTS:ROUTE_ROWS, :]
    neg = jnp.float32(-jnp.inf)

    def first_max(vals, rows):
        m = jnp.max(vals, axis=0, keepdims=True)
        idx = jnp.min(jnp.where(vals == m, rows, vals.shape[0]), axis=0, keepdims=True)
        return m, idx

    crow = lax.broadcasted_iota(jnp.int32, cl_all.shape, 0)
    cl = jnp.where(crow < N_GROUPS, cl_all, neg)
    cmax, grp = first_max(cl, crow)
    p_grp = 1.0 / jnp.sum(jnp.exp(cl - cmax), axis=0, keepdims=True)
    frow = lax.broadcasted_iota(jnp.int32, fl_all.shape, 0)
    fl = jnp.where((frow >> 3) == grp, fl_all, neg)
    f1, i1 = first_max(fl, frow)
    f2, i2 = first_max(jnp.where(frow == i1, neg, fl), frow)
    z = jnp.sum(jnp.exp(fl - f1), axis=0, keepdims=True)
    tp1 = 1.0 / z
    tp2 = jnp.exp(f2 - f1) / z
    w1 = p_grp * tp1 / (tp1 + tp2)
    w2 = p_grp * tp2 / (tp1 + tp2)

    hit1 = frow == i1
    hit2 = frow == i2
    assign = jnp.where(hit1 | hit2, 1.0, 0.0)
    before = _dot(assign.astype(BF16), tri_s[...]) + base_s[:, 0:1]
    rank1 = jnp.sum(jnp.where(hit1, before, 0.0), axis=0, keepdims=True)
    rank2 = jnp.sum(jnp.where(hit2, before, 0.0), axis=0, keepdims=True)
    base_s[...] = base_s[...] + jnp.sum(assign, axis=1, keepdims=True)
    cnt_ref[...] = base_s[...]

    rows = [i1.astype(F32), i2.astype(F32), rank1, rank2, w1, w2]
    meta_t = jnp.concatenate(rows + [jnp.zeros((LANES - len(rows), ROUTE_TILE), F32)], axis=0)
    metat_ref[...] = meta_t[0:SUBLANES, :]
    meta_ref[...] = meta_t.T


def _router(x1, sc2p, sh2, g_ffn, w_rt, b_rt):
    n = x1.shape[0]
    consts = [sc2p, sh2, g_ffn, w_rt, b_rt]
    return pl.pallas_call(
        _router_kernel,
        grid=(n // ROUTE_TILE,),
        in_specs=[pl.BlockSpec((ROUTE_TILE, D_MODEL), lambda i: (i, 0))]
        + [_const_spec(a.shape) for a in consts],
        out_specs=[pl.BlockSpec((ROUTE_TILE * PACK, LANES), lambda i: (i, 0)),
                   pl.BlockSpec((ROUTE_TILE, LANES), lambda i: (i, 0)),
                   pl.BlockSpec((SUBLANES, ROUTE_TILE), lambda i: (0, i)),
                   pl.BlockSpec((N_EXPERTS, LANES), lambda i: (0, 0))],
        out_shape=[jax.ShapeDtypeStruct((n * PACK, LANES), jnp.uint32),
                   jax.ShapeDtypeStruct((n, LANES), F32),
                   jax.ShapeDtypeStruct((SUBLANES, n), F32),
                   jax.ShapeDtypeStruct((N_EXPERTS, LANES), F32)],
        scratch_shapes=[pltpu.VMEM((N_EXPERTS, LANES), F32),
                        pltpu.VMEM((ROUTE_TILE, ROUTE_TILE), BF16)],
        compiler_params=pltpu.CompilerParams(dimension_semantics=("arbitrary",),
                                             vmem_limit_bytes=VMEM_LIMIT_BYTES),
        name="router",
    )(x1, *consts)


def _slots_kernel(offs_ref, mt_ref, dest_ref):
    mt = mt_ref[...]
    e = mt[0:2, :].astype(jnp.int32)
    acc = mt[2:4, :].astype(jnp.int32)
    for j in range(N_EXPERTS):
        acc = acc + jnp.where(e == j, offs_ref[j], 0)
    dest_ref[...] = acc


def _slots(offs, meta_t):
    n = meta_t.shape[1]
    tl = min(n, 8192)
    grid_spec = pltpu.PrefetchScalarGridSpec(
        num_scalar_prefetch=1,
        grid=(n // tl,),
        in_specs=[pl.BlockSpec((SUBLANES, tl), lambda i, offs: (0, i))],
        out_specs=pl.BlockSpec((2, tl), lambda i, offs: (0, i)),
    )
    return pl.pallas_call(
        _slots_kernel,
        grid_spec=grid_spec,
        out_shape=jax.ShapeDtypeStruct((2, n), jnp.int32),
        name="slots",
    )(offs, meta_t)


def _row_copy(src, dst, sem):
    return pltpu.make_async_copy(src, dst, sem)


def _dispatch_kernel(nv_ref, dest_ref, h_ref, xs_ref, zbuf, hbuf, zsem, lsem, sem):
    n_blocks = xs_ref.shape[0] // MOE_BLOCK

    @pl.when(pl.program_id(0) == 0)
    def _zero_partial_blocks():
        zbuf[...] = jnp.zeros_like(zbuf)

        def fill(b):
            return _row_copy(zbuf, xs_ref.at[pl.ds(pl.multiple_of(b * MOE_BLOCK, MOE_BLOCK),
                                                   MOE_BLOCK)], zsem)

        def start(b, carry):
            @pl.when(nv_ref[b] < MOE_BLOCK)
            def _():
                fill(b).start()
            return carry

        def wait(b, carry):
            @pl.when(nv_ref[b] < MOE_BLOCK)
            def _():
                fill(b).wait()
            return carry

        lax.fori_loop(0, n_blocks, start, 0)
        lax.fori_loop(0, n_blocks, wait, 0)

    i = pl.program_id(0)
    n_steps = pl.num_programs(0)
    cur = lax.rem(i, 2)

    def load(step, slot):
        rows = pl.ds(pl.multiple_of(step * TOK_TILE, TOK_TILE), TOK_TILE)
        return _row_copy(h_ref.at[rows], hbuf.at[slot], lsem.at[slot])

    def wait_rows(slot):
        for k in range(2):
            _row_copy(hbuf.at[slot], xs_ref.at[pl.ds(0, TOK_TILE)], sem.at[slot]).wait()

    @pl.when(i == 0)
    def _first_load():
        load(0, 0).start()

    @pl.when(i > 0)
    def _drain_previous():
        wait_rows(1 - cur)

    @pl.when(i + 1 < n_steps)
    def _prefetch():
        load(i + 1, 1 - cur).start()

    load(i, cur).wait()

    def issue(r4, carry):
        for u in range(ISSUE_UNROLL):
            r = r4 * ISSUE_UNROLL + u
            for k in range(2):
                _row_copy(hbuf.at[cur, r], xs_ref.at[dest_ref[k, r]],
                          sem.at[cur]).start(priority=k)
        return carry

    lax.fori_loop(0, TOK_TILE // ISSUE_UNROLL, issue, 0)

    @pl.when(i == n_steps - 1)
    def _drain_last():
        wait_rows(cur)


def _dispatch(n_valid, dest, h2s, cap):
    n = h2s.shape[0]
    grid_spec = pltpu.PrefetchScalarGridSpec(
        num_scalar_prefetch=1,
        grid=(n // TOK_TILE,),
        in_specs=[pl.BlockSpec((2, TOK_TILE), lambda i, nv: (0, i), memory_space=pltpu.SMEM),
                  pl.BlockSpec(memory_space=pl.ANY)],
        out_specs=pl.BlockSpec(memory_space=pl.ANY),
        scratch_shapes=[pltpu.VMEM((MOE_BLOCK, PACK, LANES), jnp.uint32),
                        pltpu.VMEM((2, TOK_TILE, PACK, LANES), jnp.uint32),
                        pltpu.SemaphoreType.DMA(()),
                        pltpu.SemaphoreType.DMA((2,)),
                        pltpu.SemaphoreType.DMA((2,))],
    )
    return pl.pallas_call(
        _dispatch_kernel,
        grid_spec=grid_spec,
        out_shape=jax.ShapeDtypeStruct((cap, PACK, LANES), jnp.uint32),
        compiler_params=pltpu.CompilerParams(dimension_semantics=("arbitrary",)),
        name="dispatch",
    )(n_valid, dest, h2s)


def _slab_rows_to_matrix(ref, rows):
    return jnp.concatenate([ref[pl.ds(s, rows, stride=SLAB), :] for s in range(SLAB)], axis=1)


def _expert_kernel(be_ref, nv_ref, xs_ref, wg_ref, wu_ref, wd_ref, ys_ref, wg_b, wu_b, wd_b):
    i = pl.program_id(0)
    nv = nv_ref[i]

    @pl.when((i == 0) | (be_ref[i] != be_ref[jnp.maximum(i - 1, 0)]))
    def _cast_weights():
        wg_b[...] = wg_ref[0].astype(BF16)
        wu_b[...] = wu_ref[0].astype(BF16)
        wd_b[...] = wd_ref[0].astype(BF16)

    @pl.when(nv > 0)
    def _compute():
        xb = _unpack_rows(xs_ref, MOE_BLOCK).astype(BF16)
        g = _dot(xb, wg_b[...])
        u = _dot(xb, wu_b[...])
        hid = (g * _sigmoid(g) * u).astype(BF16)
        for c in range(D_MODEL // DOWN_CHUNK):
            cols = slice(c * DOWN_CHUNK, (c + 1) * DOWN_CHUNK)
            out = _dot(hid, wd_b[:, cols])
            for j in range(DOWN_CHUNK // LANES):
                s = c * (DOWN_CHUNK // LANES) + j
                ys_ref[pl.ds(s, MOE_BLOCK, stride=SLAB), :] = out[:, j * LANES:(j + 1) * LANES]

    @pl.when(nv == 0)
    def _skip():
        ys_ref[...] = jnp.zeros_like(ys_ref)


def _experts(block_e, n_valid, xs, wg, wu, wd):
    n_blocks = xs.shape[0] // (MOE_BLOCK * PACK)
    grid_spec = pltpu.PrefetchScalarGridSpec(
        num_scalar_prefetch=2,
        grid=(n_blocks,),
        in_specs=[pl.BlockSpec((MOE_BLOCK * PACK, LANES), lambda i, be, nv: (i, 0)),
                  pl.BlockSpec((1, D_MODEL, D_EXPERT), lambda i, be, nv: (be[i], 0, 0)),
                  pl.BlockSpec((1, D_MODEL, D_EXPERT), lambda i, be, nv: (be[i], 0, 0)),
                  pl.BlockSpec((1, D_EXPERT, D_MODEL), lambda i, be, nv: (be[i], 0, 0))],
        out_specs=pl.BlockSpec((MOE_BLOCK * SLAB, LANES), lambda i, be, nv: (i, 0)),
        scratch_shapes=[pltpu.VMEM((D_MODEL, D_EXPERT), BF16),
                        pltpu.VMEM((D_MODEL, D_EXPERT), BF16),
                        pltpu.VMEM((D_EXPERT, D_MODEL), BF16)],
    )
    return pl.pallas_call(
        _expert_kernel,
        grid_spec=grid_spec,
        out_shape=jax.ShapeDtypeStruct((n_blocks * MOE_BLOCK * SLAB, LANES), F32),
        compiler_params=pltpu.CompilerParams(dimension_semantics=("arbitrary",),
                                             vmem_limit_bytes=VMEM_LIMIT_BYTES),
        name="experts",
    )(block_e, n_valid, xs, wg, wu, wd)


def _combine_kernel(dcur_ref, dnxt_ref, x_ref, meta_ref, gt_ref, gfin_ref, ys_ref, ys2d_ref,
                    o_ref, buf, sem, obuf, osem):
    i = pl.program_id(0)
    n_steps = pl.num_programs(0)
    cur = lax.rem(i, 2)

    def gather(dest_ref, slot):
        def issue(r4, carry):
            for u in range(ISSUE_UNROLL):
                r = r4 * ISSUE_UNROLL + u
                for k in range(2):
                    dst = buf.at[slot, k, pl.ds(pl.multiple_of(r * SLAB, SLAB), SLAB)]
                    _row_copy(ys_ref.at[dest_ref[k, r]], dst, sem.at[slot]).start(priority=k)
            return carry

        lax.fori_loop(0, TOK_TILE // ISSUE_UNROLL, issue, 0)

    @pl.when(i == 0)
    def _first_gather():
        gather(dcur_ref, 0)

    @pl.when(i + 1 < n_steps)
    def _prefetch():
        gather(dnxt_ref, 1 - cur)

    for k in range(2):
        _row_copy(ys2d_ref.at[pl.ds(0, TOK_TILE * SLAB)], buf.at[cur, k], sem.at[cur]).wait()

    meta = meta_ref[...]
    y = (meta[:, 4:5] * _slab_rows_to_matrix(buf.at[cur, 0], TOK_TILE)
         + meta[:, 5:6] * _slab_rows_to_matrix(buf.at[cur, 1], TOK_TILE))
    x2 = x_ref[...] + _slab_bcast(gt_ref[...], TOK_TILE // BATCH) * y
    out = _rms(x2) * gfin_ref[...]

    @pl.when(i >= 2)
    def _drain_older():
        _wait_all(_seq_major_copies(o_ref, obuf.at[cur], osem.at[cur], i - 2, True))

    for t in range(GLA_CHUNK):
        for l in range(SLAB):
            obuf[cur, l, pl.ds(t, BATCH, stride=XPITCH), :] = (
                out[t * BATCH:(t + 1) * BATCH, l * LANES:(l + 1) * LANES])
    _start_all(_seq_major_copies(o_ref, obuf.at[cur], osem.at[cur], i, True))

    @pl.when(i == n_steps - 1)
    def _drain_last():
        @pl.when(i >= 1)
        def _():
            _wait_all(_seq_major_copies(o_ref, obuf.at[1 - cur], osem.at[1 - cur], i - 1, True))

        _wait_all(_seq_major_copies(o_ref, obuf.at[cur], osem.at[cur], i, True))


def _combine(dest, x1, meta, gt2, g_final, ys):
    n = x1.shape[0]
    assert TOK_TILE == ROWS
    last = n // TOK_TILE - 1
    return pl.pallas_call(
        _combine_kernel,
        grid=(n // TOK_TILE,),
        in_specs=[pl.BlockSpec((2, TOK_TILE), lambda i: (0, i), memory_space=pltpu.SMEM),
                  pl.BlockSpec((2, TOK_TILE), lambda i: (0, jnp.minimum(i + 1, last)),
                               memory_space=pltpu.SMEM),
                  pl.BlockSpec((TOK_TILE, D_MODEL), lambda i: (i, 0)),
                  pl.BlockSpec((TOK_TILE, LANES), lambda i: (i, 0)),
                  _const_spec(gt2.shape),
                  _const_spec(g_final.shape),
                  pl.BlockSpec(memory_space=pl.ANY),
                  pl.BlockSpec(memory_space=pl.ANY)],
        out_specs=pl.BlockSpec(memory_space=pl.ANY),
        out_shape=jax.ShapeDtypeStruct((BATCH, n // BATCH, D_MODEL), F32),
        scratch_shapes=[pltpu.VMEM((2, 2, TOK_TILE * SLAB, LANES), F32),
                        pltpu.SemaphoreType.DMA((2,)),
                        pltpu.VMEM((2, SLAB, BATCH * XPITCH, LANES), F32),
                        pltpu.SemaphoreType.DMA((2,))],
        compiler_params=pltpu.CompilerParams(dimension_semantics=("arbitrary",),
                                             vmem_limit_bytes=VMEM_LIMIT_BYTES),
        name="combine",
    )(dest, dest, x1, meta, gt2, g_final, ys.reshape(-1, SLAB, LANES), ys)


def _block_diag(w):
    h, d, _ = w.shape
    eye = jnp.eye(h, dtype=w.dtype)
    return (eye[:, None, :, None] * w[:, :, None, :]).reshape(h * d, h * d)


def kernel(x, c, w_ada, b_ada, g_mix, g_ffn, g_final, w_in, conv_w, conv_b, lru_wr, lru_br,
           lru_wi, lru_bi, lru_lambda, gla_wa2, gla_ba, gla_gnorm, w_out, w_coarse, b_coarse,
           w_fine, b_fine, w_gate, w_up, w_down):
    bsz, seq, d = x.shape
    assert bsz == BATCH and d == D_MODEL and seq % (2 * GLA_CHUNK) == 0
    assert w_ada.shape[0] == 1, "single-layer problem"
    n = bsz * seq
    row = lambda v: v.reshape(1, -1)

    mod = _ada(c, w_ada[0], b_ada[0])
    sh1, sc1, gt1, sh2, sc2, gt2 = jnp.split(mod, 6, axis=-1)

    w_in_p = jnp.pad(w_in[0], ((0, 0), (0, D_IN_PAD - w_in.shape[2]))).astype(BF16)
    wa2_p = jnp.pad(gla_wa2[0], ((0, LANES - GLA_GATE_RANK), (0, 0))).astype(BF16)
    x1 = _mixer(x, 1.0 + sc1, sh1, gt1, row(g_mix[0]), w_in_p, conv_w[0], row(conv_b[0]),
                _block_diag(lru_wr[0]).astype(BF16), row(lru_br[0]),
                _block_diag(lru_wi[0]).astype(BF16), row(lru_bi[0]), row(lru_lambda[0]),
                wa2_p, row(gla_ba[0]), row(gla_gnorm[0]), w_out[0].astype(BF16))

    pad_rows = ROUTE_ROWS - N_EXPERTS - N_GROUPS
    w_rt = jnp.pad(jnp.concatenate([w_fine[0], w_coarse[0]], axis=1).T,
                   ((0, pad_rows), (0, 0))).astype(BF16)
    b_rt = jnp.pad(jnp.concatenate([b_fine[0], b_coarse[0]]), (0, pad_rows)).reshape(-1, 1)
    h2, meta, meta_t, cnt = _router(x1, 1.0 + sc2, sh2, row(g_ffn[0]), w_rt, b_rt)

    counts = cnt[:, 0].astype(jnp.int32)
    padded = (counts + MOE_BLOCK - 1) // MOE_BLOCK * MOE_BLOCK
    pends = jnp.cumsum(padded)
    offs = pends - padded
    cap = (2 * n + MOE_BLOCK - 1) // MOE_BLOCK * MOE_BLOCK + N_EXPERTS * MOE_BLOCK
    n_blocks = cap // MOE_BLOCK
    starts = jnp.arange(n_blocks, dtype=jnp.int32) * MOE_BLOCK
    block_e = jnp.minimum(jnp.sum((pends[None, :] <= starts[:, None]).astype(jnp.int32), axis=1),
                          N_EXPERTS - 1)
    own = block_e[:, None] == jnp.arange(N_EXPERTS, dtype=jnp.int32)[None, :]
    seg_end = jnp.sum(jnp.where(own, (offs + counts)[None, :], 0), axis=1)
    n_valid = jnp.clip(seg_end - starts, 0, MOE_BLOCK).astype(jnp.int32)
    dest = _slots(offs.astype(jnp.int32), meta_t)

    xs = _dispatch(n_valid, dest, h2.reshape(n, PACK, LANES), cap)
    ys = _experts(block_e, n_valid, xs.reshape(cap * PACK, LANES),
                  w_gate[0], w_up[0], w_down[0])
    return _combine(dest, x1, meta, gt2, row(g_final), ys)
```

```python
import functools

import jax
import jax.numpy as jnp
import numpy as np
from jax import lax
from jax.experimental import pallas as pl
from jax.experimental.pallas import tpu as pltpu

F32 = jnp.float32
BF16 = jnp.bfloat16

SUBLANES = 8
LANES = 128
VMEM_LIMIT_BYTES = 56 * 1024 * 1024

D_MODEL = 1024
BATCH = SUBLANES
D_LRU = 512
LRU_HEADS = 8
CONV_WIDTH = 4
LRU_C = 8.0
D_GLA = 512
GLA_HEADS = 4
GLA_DV = 128
GLA_DK = 64
GLA_GATE_RANK = 16
GLA_GATE_NORM = 16.0
GLA_CHUNK = 64
N_GROUPS = 4
EXPERTS_PER_GROUP = 8
N_EXPERTS = 32
D_EXPERT = 512
MOE_BLOCK = 512
EPS = 1e-6

QK = GLA_HEADS * GLA_DK
ROWS = GLA_CHUNK * BATCH
HALO = (CONV_WIDTH - 1) * BATCH
SLAB = D_MODEL // LANES
PACK = D_MODEL // (2 * LANES)
C_LX, C_LY, C_Q, C_K, C_V, C_G, C_GL = 0, 512, 1024, 1280, 1536, 2048, 2560
D_IN_PAD = 2688
ROUTE_TILE = 512
ROUTE_ROWS = N_EXPERTS + SUBLANES
TOK_TILE = 512
DOWN_CHUNK = 256
PROJ_CHUNK = 256
COMBINE_CHUNKS = 8
XPITCH = 72


def _dot(a, b):
    return jnp.dot(a, b, preferred_element_type=F32)


def _dot_nt(a, b):
    return lax.dot_general(a, b, (((1,), (1,)), ((), ())), preferred_element_type=F32)


def _softplus(z):
    return jnp.maximum(z, 0.0) + jnp.log1p(jnp.exp(-jnp.abs(z)))


def _sigmoid(z):
    return 0.5 * jnp.tanh(0.5 * z) + 0.5


def _rms(x):
    return x * lax.rsqrt(jnp.mean(x * x, axis=-1, keepdims=True) + EPS)


def _pack_rows(ref, val, rows):
    for s in range(PACK):
        lo = val[:, (2 * s) * LANES:(2 * s + 1) * LANES].astype(BF16).astype(F32)
        hi = val[:, (2 * s + 1) * LANES:(2 * s + 2) * LANES].astype(BF16).astype(F32)
        word = (lax.bitcast_convert_type(lo, jnp.uint32) >> 16) | lax.bitcast_convert_type(
            hi, jnp.uint32)
        ref[pl.ds(s, rows, stride=PACK), :] = word


def _unpack_rows(ref, rows):
    cols = []
    for s in range(PACK):
        word = ref[pl.ds(s, rows, stride=PACK), :]
        cols.append(lax.bitcast_convert_type(word << 16, F32))
        cols.append(lax.bitcast_convert_type(word & jnp.uint32(0xFFFF0000), F32))
    return jnp.concatenate(cols, axis=1)


def _slab_bcast(v, n):
    c = v.shape[-1]
    return jnp.broadcast_to(v[None], (n, SUBLANES, c)).reshape(n * SUBLANES, c)


def _seq_major_copies(hbm, buf, sem, step, to_hbm):
    copies = []
    t0 = pl.multiple_of(step * GLA_CHUNK, GLA_CHUNK)
    for b in range(BATCH):
        for l in range(SLAB):
            h = hbm.at[b, pl.ds(t0, GLA_CHUNK), pl.ds(l * LANES, LANES)]
            v = buf.at[l, pl.ds(b * XPITCH, GLA_CHUNK)]
            copies.append(pltpu.make_async_copy(v, h, sem) if to_hbm
                          else pltpu.make_async_copy(h, v, sem))
    return copies


def _start_all(copies):
    for c in copies:
        c.start()


def _wait_all(copies):
    for c in copies:
        c.wait()


def _ada_kernel(c_ref, w_ref, b_ref, o_ref):
    c = c_ref[...]
    s = c * _sigmoid(c)
    o_ref[...] = jnp.dot(s, w_ref[...], preferred_element_type=F32,
                         precision=lax.Precision.HIGHEST) + b_ref[...]


def _ada(c, w, b):
    n_out = w.shape[1]
    tn = 1024
    return pl.pallas_call(
        _ada_kernel,
        grid=(n_out // tn,),
        in_specs=[pl.BlockSpec((BATCH, D_MODEL), lambda j: (0, 0)),
                  pl.BlockSpec((D_MODEL, tn), lambda j: (0, j)),
                  pl.BlockSpec((1, tn), lambda j: (0, j))],
        out_specs=pl.BlockSpec((BATCH, tn), lambda j: (0, j)),
        out_shape=jax.ShapeDtypeStruct((BATCH, n_out), F32),
        name="ada",
    )(c, w, b.reshape(1, n_out))


def _mixer_kernel(x_ref, sc_ref, sh_ref, gt_ref, gmix_ref, win_ref, cw_ref, cb_ref,
                  wr_ref, br_ref, wi_ref, bi_ref, lam_ref, wa2_ref, ba_ref, gn_ref, wout_ref,
                  o_ref,
                  xbuf, a_s, u_s, hs_s, hc_s, la_s, bc_s, st_s, cm_s, sel_s, hb_s,
                  xin_a, xin_b, x_a, x_b, p_a, p_b, xsem):
    j = pl.program_id(0)
    n_tiles = 2 * pl.num_programs(0)
    nt = GLA_CHUNK
    side = ((xin_a, x_a, p_a), (xin_b, x_b, p_b))

    def copies(tile, sd):
        return _seq_major_copies(x_ref, side[sd][0], xsem.at[sd], tile, False)

    def normalise(sd):
        xin, x_s, _ = side[sd]
        for t in range(nt):
            for l in range(SLAB):
                x_s[t * BATCH:(t + 1) * BATCH, l * LANES:(l + 1) * LANES] = (
                    xin[l, pl.ds(t, BATCH, stride=XPITCH), :])
        y = _rms(x_s[...]) * gmix_ref[...]
        hb_s[...] = (y * _slab_bcast(sc_ref[...], nt) + _slab_bcast(sh_ref[...], nt)).astype(BF16)

    def project(sd, c0, c1):
        side[sd][2][:, c0:c1] = _dot(hb_s[...], win_ref[:, c0:c1])

    def input_half(sd):
        pieces = [lambda: normalise(sd)]
        for c0 in range(0, D_IN_PAD, PROJ_CHUNK):
            pieces.append(functools.partial(project, sd, c0, min(c0 + PROJ_CHUNK, D_IN_PAD)))
        return pieces

    def recurrent_half(sd, out_rows, fill):
        _, x_s, p_s = side[sd]
        fill = list(fill)

        def next_piece():
            if fill:
                fill.pop(0)()

        def proj(c0, c1):
            return p_s[:, c0:c1]

        next_piece()
        xbuf[HALO:HALO + ROWS, :] = proj(C_LX, C_LY)
        cw = cw_ref[...]
        cx = cb_ref[...] + sum(cw[k:k + 1, :] * xbuf[k * BATCH:k * BATCH + ROWS, :]
                               for k in range(CONV_WIDTH))
        xbuf[0:HALO, :] = xbuf[ROWS:ROWS + HALO, :]
        cxb = cx.astype(BF16)
        next_piece()

        def gate(w_ref, b_ref):
            hw = D_LRU // 2
            pre = jnp.concatenate([_dot(cxb[:, 0:hw], w_ref[0:hw, 0:hw]),
                                   _dot(cxb[:, hw:], w_ref[hw:, hw:])], axis=1)
            return _sigmoid(pre + b_ref[...])

        r_gate = gate(wr_ref, br_ref)
        next_piece()
        i_gate = gate(wi_ref, bi_ref)
        next_piece()
        decay_rate = (-LRU_C) * _softplus(-lam_ref[...])
        for rows in (slice(0, ROWS // 2), slice(ROWS // 2, ROWS)):
            log_a = r_gate[rows] * decay_rate
            a_s[rows, :] = jnp.exp(log_a)
            th = jnp.tanh(log_a)
            u_s[rows, :] = jnp.sqrt(-2.0 * th / (1.0 - th)) * (i_gate[rows] * cx[rows])
            next_piece()
        hcur = hc_s[...]
        for t in range(nt):
            sl = slice(t * BATCH, (t + 1) * BATCH)
            hcur = a_s[sl, :] * hcur + u_s[sl, :]
            hs_s[sl, :] = hcur
        hc_s[...] = hcur
        next_piece()
        lru_out = hs_s[...] * jax.nn.gelu(proj(C_LY, C_Q), approximate=True)
        next_piece()

        gate_lr = proj(C_GL, D_IN_PAD).astype(BF16)
        z = _dot(gate_lr, wa2_ref[...]) + ba_ref[...]
        la_s[...] = -_softplus(-z) * (1.0 / GLA_GATE_NORM)
        next_piece()
        bcur = jnp.zeros((BATCH, QK), F32)
        for t in range(nt):
            sl = slice(t * BATCH, (t + 1) * BATCH)
            bcur = bcur + la_s[sl, :]
            bc_s[sl, :] = bcur
        next_piece()
        bc = bc_s[...]
        e_last = jnp.exp(bcur)
        q_dec = proj(C_Q, C_K) * (GLA_DK ** -0.5) * jnp.exp(bc)
        kk = proj(C_K, C_V)
        k_dec = kk * jnp.exp(-bc)
        k_last = kk * jnp.exp(_slab_bcast(bcur, nt) - bc)
        next_piece()
        vv = proj(C_V, C_G)
        gg = proj(C_G, C_GL)
        next_piece()
        hr = ROWS // 2
        causal_top = cm_s[0:hr, 0:hr] > 0.5
        causal_bot = cm_s[hr:ROWS, :] > 0.5

        lane = lax.broadcasted_iota(jnp.int32, (SUBLANES, LANES), 1)
        sub = lax.broadcasted_iota(jnp.int32, (SUBLANES, LANES), 0)
        half = lane >> 6
        seq_sel = [(sub == 2 * jj + half).astype(F32) for jj in range(BATCH // 2)]

        def expand(m):
            mb = m.astype(BF16)
            return jnp.concatenate([mb * sel_s[jj] for jj in range(BATCH // 2)], axis=1)

        def both_halves(m, hh):
            keep = (lax.broadcasted_iota(jnp.int32, m.shape, 1) >> 6) == hh
            mh = jnp.where(keep, m, 0.0)
            return mh, mh + pltpu.roll(mh, GLA_DK, axis=1)

        gla_parts = []
        for hd in range(GLA_HEADS):
            p, hh = hd // 2, hd % 2
            pc = slice(p * LANES, (p + 1) * LANES)
            qh, q_both = both_halves(q_dec[:, pc], hh)
            _, k_both = both_halves(k_last[:, pc], hh)
            _, e_both = both_halves(e_last[:, pc], hh)
            v_h = vv[:, hd * GLA_DV:(hd + 1) * GLA_DV]
            v_hb = v_h.astype(BF16)
            qb = qh.astype(BF16)
            kb = k_dec[:, pc].astype(BF16)
            s_top = jnp.where(causal_top, _dot_nt(qb[0:hr], kb[0:hr]), 0.0)
            s_bot = jnp.where(causal_bot, _dot_nt(qb[hr:ROWS], kb), 0.0)
            o_h = jnp.concatenate([_dot(s_top.astype(BF16), v_hb[0:hr]),
                                   _dot(s_bot.astype(BF16), v_hb)], axis=0)
            st = st_s[hd]
            o_h = o_h + _dot_nt(expand(q_both), st.astype(BF16))
            kv_t = _dot(v_h.T.astype(BF16), expand(k_both))
            decay = jnp.concatenate(
                [jnp.sum(e_both * sq, axis=0, keepdims=True) for sq in seq_sel], axis=1)
            st_s[hd] = st * decay + kv_t
            o_n = _rms(o_h) * gn_ref[...]
            g_h = gg[:, hd * GLA_DV:(hd + 1) * GLA_DV]
            gla_parts.append(o_n * (g_h * _sigmoid(g_h)))

        mix_in = jnp.concatenate([lru_out] + gla_parts, axis=1).astype(BF16)
        mix = _dot(mix_in, wout_ref[...])
        o_ref[out_rows, :] = x_s[...] + _slab_bcast(gt_ref[...], nt) * mix
        assert not fill

    @pl.when(j == 0)
    def _init():
        _start_all(copies(0, 0))
        _start_all(copies(1, 1))
        xbuf[0:HALO, :] = jnp.zeros((HALO, D_LRU), F32)
        hc_s[...] = jnp.zeros_like(hc_s)
        st_s[...] = jnp.zeros_like(st_s)
        r = lax.broadcasted_iota(jnp.int32, (ROWS, ROWS), 0)
        c = lax.broadcasted_iota(jnp.int32, (ROWS, ROWS), 1)
        same_seq = (r & (BATCH - 1)) == (c & (BATCH - 1))
        cm_s[...] = jnp.where(same_seq & ((c >> 3) <= (r >> 3)), 1.0, 0.0).astype(F32)
        rr = lax.broadcasted_iota(jnp.int32, (ROWS, LANES), 0) & (BATCH - 1)
        ll = lax.broadcasted_iota(jnp.int32, (ROWS, LANES), 1) >> 6
        for jj in range(BATCH // 2):
            sel_s[jj] = jnp.where(rr == 2 * jj + ll, 1.0, 0.0).astype(BF16)
        _wait_all(copies(0, 0))
        for piece in input_half(0):
            piece()

    @pl.when(2 * j + 2 < n_tiles)
    def _fetch_next_even():
        _start_all(copies(2 * j + 2, 0))

    _wait_all(copies(2 * j + 1, 1))
    recurrent_half(0, slice(0, ROWS), input_half(1))

    @pl.when(2 * j + 2 < n_tiles)
    def _next_even_arrived():
        _wait_all(copies(2 * j + 2, 0))

    @pl.when(2 * j + 3 < n_tiles)
    def _next_odd():
        _start_all(copies(2 * j + 3, 1))

    recurrent_half(1, slice(ROWS, 2 * ROWS), input_half(0))


def _const_spec(shape):
    nd = len(shape)
    return pl.BlockSpec(shape, lambda i: (0,) * nd)


def _mixer(x, sc1p, sh1, gt1, g_mix, w_in_p, conv_w, conv_b, wr_d, br, wi_d, bi, lam,
           wa2_p, ba, gn, w_out_b):
    n = x.shape[0] * x.shape[1]
    consts = [sc1p, sh1, gt1, g_mix, w_in_p, conv_w, conv_b, wr_d, br, wi_d, bi, lam,
              wa2_p, ba, gn, w_out_b]
    seq_buf = pltpu.VMEM((SLAB, BATCH * XPITCH, LANES), F32)
    return pl.pallas_call(
        _mixer_kernel,
        grid=(n // (2 * ROWS),),
        in_specs=[pl.BlockSpec(memory_space=pl.ANY)]
        + [_const_spec(a.shape) for a in consts],
        out_specs=pl.BlockSpec((2 * ROWS, D_MODEL), lambda i: (i, 0)),
        out_shape=jax.ShapeDtypeStruct((n, D_MODEL), F32),
        scratch_shapes=[
            pltpu.VMEM((HALO + ROWS, D_LRU), F32),
            pltpu.VMEM((ROWS, D_LRU), F32),
            pltpu.VMEM((ROWS, D_LRU), F32),
            pltpu.VMEM((ROWS, D_LRU), F32),
            pltpu.VMEM((BATCH, D_LRU), F32),
            pltpu.VMEM((ROWS, QK), F32),
            pltpu.VMEM((ROWS, QK), F32),
            pltpu.VMEM((GLA_HEADS, GLA_DV, BATCH * GLA_DK), F32),
            pltpu.VMEM((ROWS, ROWS), F32),
            pltpu.VMEM((BATCH // 2, ROWS, LANES), BF16),
            pltpu.VMEM((ROWS, D_MODEL), BF16),
            seq_buf, seq_buf,
            pltpu.VMEM((ROWS, D_MODEL), F32),
            pltpu.VMEM((ROWS, D_MODEL), F32),
            pltpu.VMEM((ROWS, D_IN_PAD), F32),
            pltpu.VMEM((ROWS, D_IN_PAD), F32),
            pltpu.SemaphoreType.DMA((2,)),
        ],
        compiler_params=pltpu.CompilerParams(dimension_semantics=("arbitrary",),
                                             vmem_limit_bytes=VMEM_LIMIT_BYTES),
        name="mixer",
    )(x, *consts)


def _router_kernel(x_ref, sc_ref, sh_ref, gffn_ref, wrt_ref, brt_ref,
                   h2_ref, meta_ref, metat_ref, cnt_ref, base_s, tri_s):
    i = pl.program_id(0)
    nt = ROUTE_TILE // BATCH

    @pl.when(i == 0)
    def _init():
        base_s[...] = jnp.zeros_like(base_s)
        r = lax.broadcasted_iota(jnp.int32, (ROUTE_TILE, ROUTE_TILE), 0)
        c = lax.broadcasted_iota(jnp.int32, (ROUTE_TILE, ROUTE_TILE), 1)
        tri_s[...] = jnp.where(r < c, 1.0, 0.0).astype(BF16)

    x = x_ref[...]
    h2 = _rms(x) * gffn_ref[...] * _slab_bcast(sc_ref[...], nt) + _slab_bcast(sh_ref[...], nt)
    _pack_rows(h2_ref, h2, ROUTE_TILE)
    logits = _dot_nt(wrt_ref[...], h2.astype(BF16)) + brt_ref[...]
    fl_all = logits[0:N_EXPERTS, :]
    cl_all = logits[N_EXPERTS:ROUTE_ROWS, :]
    neg = jnp.float32(-jnp.inf)

    def first_max(vals, rows):
        m = jnp.max(vals, axis=0, keepdims=True)
        idx = jnp.min(jnp.where(vals == m, rows, vals.shape[0]), axis=0, keepdims=True)
        return m, idx

    crow = lax.broadcasted_iota(jnp.int32, cl_all.shape, 0)
    cl = jnp.where(crow < N_GROUPS, cl_all, neg)
    cmax, grp = first_max(cl, crow)
    p_grp = 1.0 / jnp.sum(jnp.exp(cl - cmax), axis=0, keepdims=True)
    frow = lax.broadcasted_iota(jnp.int32, fl_all.shape, 0)
    fl = jnp.where((frow >> 3) == grp, fl_all, neg)
    f1, i1 = first_max(fl, frow)
    f2, i2 = first_max(jnp.where(frow == i1, neg, fl), frow)
    z = jnp.sum(jnp.exp(fl - f1), axis=0, keepdims=True)
    tp1 = 1.0 / z
    tp2 = jnp.exp(f2 - f1) / z
    w1 = p_grp * tp1 / (tp1 + tp2)
    w2 = p_grp * tp2 / (tp1 + tp2)

    hit1 = frow == i1
    hit2 = frow == i2
    assign = jnp.where(hit1 | hit2, 1.0, 0.0)
    before = _dot(assign.astype(BF16), tri_s[...]) + base_s[:, 0:1]
    rank1 = jnp.sum(jnp.where(hit1, before, 0.0), axis=0, keepdims=True)
    rank2 = jnp.sum(jnp.where(hit2, before, 0.0), axis=0, keepdims=True)
    base_s[...] = base_s[...] + jnp.sum(assign, axis=1, keepdims=True)
    cnt_ref[...] = base_s[...]

    rows = [i1.astype(F32), i2.astype(F32), rank1, rank2, w1, w2]
    meta_t = jnp.concatenate(rows + [jnp.zeros((LANES - len(rows), ROUTE_TILE), F32)], axis=0)
    metat_ref[...] = meta_t[0:SUBLANES, :]
    meta_ref[...] = meta_t.T


def _router(x1, sc2p, sh2, g_ffn, w_rt, b_rt):
    n = x1.shape[0]
    consts = [sc2p, sh2, g_ffn, w_rt, b_rt]
    return pl.pallas_call(
        _router_kernel,
        grid=(n // ROUTE_TILE,),
        in_specs=[pl.BlockSpec((ROUTE_TILE, D_MODEL), lambda i: (i, 0))]
        + [_const_spec(a.shape) for a in consts],
        out_specs=[pl.BlockSpec((ROUTE_TILE * PACK, LANES), lambda i: (i, 0)),
                   pl.BlockSpec((ROUTE_TILE, LANES), lambda i: (i, 0)),
                   pl.BlockSpec((SUBLANES, ROUTE_TILE), lambda i: (0, i)),
                   pl.BlockSpec((N_EXPERTS, LANES), lambda i: (0, 0))],
        out_shape=[jax.ShapeDtypeStruct((n * PACK, LANES), jnp.uint32),
                   jax.ShapeDtypeStruct((n, LANES), F32),
                   jax.ShapeDtypeStruct((SUBLANES, n), F32),
                   jax.ShapeDtypeStruct((N_EXPERTS, LANES), F32)],
        scratch_shapes=[pltpu.VMEM((N_EXPERTS, LANES), F32),
                        pltpu.VMEM((ROUTE_TILE, ROUTE_TILE), BF16)],
        compiler_params=pltpu.CompilerParams(dimension_semantics=("arbitrary",),
                                             vmem_limit_bytes=VMEM_LIMIT_BYTES),
        name="router",
    )(x1, *consts)


def _slots_kernel(offs_ref, mt_ref, dest_ref):
    mt = mt_ref[...]
    e = mt[0:2, :].astype(jnp.int32)
    acc = mt[2:4, :].astype(jnp.int32)
    for j in range(N_EXPERTS):
        acc = acc + jnp.where(e == j, offs_ref[j], 0)
    dest_ref[...] = acc


def _slots(offs, meta_t):
    n = meta_t.shape[1]
    tl = min(n, 8192)
    grid_spec = pltpu.PrefetchScalarGridSpec(
        num_scalar_prefetch=1,
        grid=(n // tl,),
        in_specs=[pl.BlockSpec((SUBLANES, tl), lambda i, offs: (0, i))],
        out_specs=pl.BlockSpec((2, tl), lambda i, offs: (0, i)),
    )
    return pl.pallas_call(
        _slots_kernel,
        grid_spec=grid_spec,
        out_shape=jax.ShapeDtypeStruct((2, n), jnp.int32),
        name="slots",
    )(offs, meta_t)


def _row_copy(src, dst, sem):
    return pltpu.make_async_copy(src, dst, sem)


def _dispatch_kernel(nv_ref, dest_ref, h_ref, xs_ref, zbuf, hbuf, zsem, lsem, sem):
    n_blocks = xs_ref.shape[0] // MOE_BLOCK

    @pl.when(pl.program_id(0) == 0)
    def _zero_partial_blocks():
        zbuf[...] = jnp.zeros_like(zbuf)

        def fill(b):
            return _row_copy(zbuf, xs_ref.at[pl.ds(pl.multiple_of(b * MOE_BLOCK, MOE_BLOCK),
                                                   MOE_BLOCK)], zsem)

        def start(b, carry):
            @pl.when(nv_ref[b] < MOE_BLOCK)
            def _():
                fill(b).start()
            return carry

        def wait(b, carry):
            @pl.when(nv_ref[b] < MOE_BLOCK)
            def _():
                fill(b).wait()
            return carry

        lax.fori_loop(0, n_blocks, start, 0)
        lax.fori_loop(0, n_blocks, wait, 0)

    i = pl.program_id(0)
    n_steps = pl.num_programs(0)
    cur = lax.rem(i, 2)

    def load(step, slot):
        rows = pl.ds(pl.multiple_of(step * TOK_TILE, TOK_TILE), TOK_TILE)
        return _row_copy(h_ref.at[rows], hbuf.at[slot], lsem.at[slot])

    def wait_rows(slot):
        for k in range(2):
            _row_copy(hbuf.at[slot], xs_ref.at[pl.ds(0, TOK_TILE)], sem.at[slot]).wait()

    @pl.when(i == 0)
    def _first_load():
        load(0, 0).start()

    @pl.when(i > 0)
    def _drain_previous():
        wait_rows(1 - cur)

    @pl.when(i + 1 < n_steps)
    def _prefetch():
        load(i + 1, 1 - cur).start()

    load(i, cur).wait()

    for r in range(TOK_TILE):
        for k in range(2):
            _row_copy(hbuf.at[cur, r], xs_ref.at[dest_ref[k, r]], sem.at[cur]).start(priority=k)

    @pl.when(i == n_steps - 1)
    def _drain_last():
        wait_rows(cur)


def _dispatch(n_valid, dest, h2s, cap):
    n = h2s.shape[0]
    grid_spec = pltpu.PrefetchScalarGridSpec(
        num_scalar_prefetch=1,
        grid=(n // TOK_TILE,),
        in_specs=[pl.BlockSpec((2, TOK_TILE), lambda i, nv: (0, i), memory_space=pltpu.SMEM),
                  pl.BlockSpec(memory_space=pl.ANY)],
        out_specs=pl.BlockSpec(memory_space=pl.ANY),
        scratch_shapes=[pltpu.VMEM((MOE_BLOCK, PACK, LANES), jnp.uint32),
                        pltpu.VMEM((2, TOK_TILE, PACK, LANES), jnp.uint32),
                        pltpu.SemaphoreType.DMA(()),
                        pltpu.SemaphoreType.DMA((2,)),
                        pltpu.SemaphoreType.DMA((2,))],
    )
    return pl.pallas_call(
        _dispatch_kernel,
        grid_spec=grid_spec,
        out_shape=jax.ShapeDtypeStruct((cap, PACK, LANES), jnp.uint32),
        compiler_params=pltpu.CompilerParams(dimension_semantics=("arbitrary",)),
        name="dispatch",
    )(n_valid, dest, h2s)


def _expert_kernel(be_ref, nv_ref, xs_ref, wg_ref, wu_ref, wd_ref, ys_ref, wg_b, wu_b, wd_b):
    i = pl.program_id(0)
    nv = nv_ref[i]

    @pl.when((i == 0) | (be_ref[i] != be_ref[jnp.maximum(i - 1, 0)]))
    def _cast_weights():
        wg_b[...] = wg_ref[0].astype(BF16)
        wu_b[...] = wu_ref[0].astype(BF16)
        wd_b[...] = wd_ref[0].astype(BF16)

    @pl.when(nv > 0)
    def _compute():
        xb = _unpack_rows(xs_ref, MOE_BLOCK).astype(BF16)
        g = _dot(xb, wg_b[...])
        u = _dot(xb, wu_b[...])
        hid = (g * _sigmoid(g) * u).astype(BF16)
        for c in range(D_MODEL // DOWN_CHUNK):
            cols = slice(c * DOWN_CHUNK, (c + 1) * DOWN_CHUNK)
            out = _dot(hid, wd_b[:, cols])
            for j in range(DOWN_CHUNK // LANES):
                s = c * (DOWN_CHUNK // LANES) + j
                ys_ref[pl.ds(s, MOE_BLOCK, stride=SLAB), :] = out[:, j * LANES:(j + 1) * LANES]

    @pl.when(nv == 0)
    def _skip():
        ys_ref[...] = jnp.zeros_like(ys_ref)


def _experts(block_e, n_valid, xs, wg, wu, wd):
    n_blocks = xs.shape[0] // (MOE_BLOCK * PACK)
    grid_spec = pltpu.PrefetchScalarGridSpec(
        num_scalar_prefetch=2,
        grid=(n_blocks,),
        in_specs=[pl.BlockSpec((MOE_BLOCK * PACK, LANES), lambda i, be, nv: (i, 0)),
                  pl.BlockSpec((1, D_MODEL, D_EXPERT), lambda i, be, nv: (be[i], 0, 0)),
                  pl.BlockSpec((1, D_MODEL, D_EXPERT), lambda i, be, nv: (be[i], 0, 0)),
                  pl.BlockSpec((1, D_EXPERT, D_MODEL), lambda i, be, nv: (be[i], 0, 0))],
        out_specs=pl.BlockSpec((MOE_BLOCK * SLAB, LANES), lambda i, be, nv: (i, 0)),
        scratch_shapes=[pltpu.VMEM((D_MODEL, D_EXPERT), BF16),
                        pltpu.VMEM((D_MODEL, D_EXPERT), BF16),
                        pltpu.VMEM((D_EXPERT, D_MODEL), BF16)],
    )
    return pl.pallas_call(
        _expert_kernel,
        grid_spec=grid_spec,
        out_shape=jax.ShapeDtypeStruct((n_blocks * MOE_BLOCK * SLAB, LANES), F32),
        compiler_params=pltpu.CompilerParams(dimension_semantics=("arbitrary",),
                                             vmem_limit_bytes=VMEM_LIMIT_BYTES),
        name="experts",
    )(block_e, n_valid, xs, wg, wu, wd)


def _combine_kernel(dcur_ref, dnxt_ref, x_ref, meta_ref, gt_ref, gfin_ref, ys_ref, ys2d_ref,
                    o_ref, buf, sem, obuf, osem):
    i = pl.program_id(0)
    n_steps = pl.num_programs(0)
    cur = lax.rem(i, 2)
    nxt = 1 - cur

    def start_gather(dest_ref, slot, r):
        for k in range(2):
            dst = buf.at[slot, k, pl.ds(r * SLAB, SLAB)]
            _row_copy(ys_ref.at[dest_ref[k, r]], dst, sem.at[slot]).start(priority=k)

    def wait_gather(slot):
        for k in range(2):
            _row_copy(ys2d_ref.at[pl.ds(0, TOK_TILE * SLAB)], buf.at[slot, k], sem.at[slot]).wait()

    @pl.when(i == 0)
    def _first_gather():
        for r in range(TOK_TILE):
            start_gather(dcur_ref, 0, r)

    wait_gather(cur)

    @pl.when(i >= 2)
    def _drain_older():
        _wait_all(_seq_major_copies(o_ref, obuf.at[cur], osem.at[cur], i - 2, True))

    chunk = TOK_TILE // COMBINE_CHUNKS
    for c in range(COMBINE_CHUNKS):
        for r in range(c * chunk, (c + 1) * chunk):
            start_gather(dnxt_ref, nxt, r)
        rows = slice(c * chunk, (c + 1) * chunk)
        meta = meta_ref[rows, :]

        def expert_rows(k):
            return jnp.concatenate(
                [buf[cur, k, pl.ds(c * chunk * SLAB + s, chunk, stride=SLAB), :]
                 for s in range(SLAB)], axis=1)

        y = meta[:, 4:5] * expert_rows(0) + meta[:, 5:6] * expert_rows(1)
        x2 = x_ref[rows, :] + _slab_bcast(gt_ref[...], chunk // BATCH) * y
        out = _rms(x2) * gfin_ref[...]
        for tt in range(chunk // BATCH):
            t = c * (chunk // BATCH) + tt
            for l in range(SLAB):
                obuf[cur, l, pl.ds(t, BATCH, stride=XPITCH), :] = (
                    out[tt * BATCH:(tt + 1) * BATCH, l * LANES:(l + 1) * LANES])
    _start_all(_seq_major_copies(o_ref, obuf.at[cur], osem.at[cur], i, True))

    @pl.when(i == n_steps - 1)
    def _drain_last():
        wait_gather(nxt)

        @pl.when(i >= 1)
        def _():
            _wait_all(_seq_major_copies(o_ref, obuf.at[1 - cur], osem.at[1 - cur], i - 1, True))

        _wait_all(_seq_major_copies(o_ref, obuf.at[cur], osem.at[cur], i, True))


def _combine(dest, x1, meta, gt2, g_final, ys):
    n = x1.shape[0]
    assert TOK_TILE == ROWS
    last = n // TOK_TILE - 1
    return pl.pallas_call(
        _combine_kernel,
        grid=(n // TOK_TILE,),
        in_specs=[pl.BlockSpec((2, TOK_TILE), lambda i: (0, i), memory_space=pltpu.SMEM),
                  pl.BlockSpec((2, TOK_TILE), lambda i: (0, jnp.minimum(i + 1, last)),
                               memory_space=pltpu.SMEM),
                  pl.BlockSpec((TOK_TILE, D_MODEL), lambda i: (i, 0)),
                  pl.BlockSpec((TOK_TILE, LANES), lambda i: (i, 0)),
                  _const_spec(gt2.shape),
                  _const_spec(g_final.shape),
                  pl.BlockSpec(memory_space=pl.ANY),
                  pl.BlockSpec(memory_space=pl.ANY)],
        out_specs=pl.BlockSpec(memory_space=pl.ANY),
        out_shape=jax.ShapeDtypeStruct((BATCH, n // BATCH, D_MODEL), F32),
        scratch_shapes=[pltpu.VMEM((2, 2, TOK_TILE * SLAB, LANES), F32),
                        pltpu.SemaphoreType.DMA((2,)),
                        pltpu.VMEM((2, SLAB, BATCH * XPITCH, LANES), F32),
                        pltpu.SemaphoreType.DMA((2,))],
        compiler_params=pltpu.CompilerParams(dimension_semantics=("arbitrary",),
                                             vmem_limit_bytes=VMEM_LIMIT_BYTES),
        name="combine",
    )(dest, dest, x1, meta, gt2, g_final, ys.reshape(-1, SLAB, LANES), ys)


def _block_diag(w):
    h, d, _ = w.shape
    eye = jnp.eye(h, dtype=w.dtype)
    return (eye[:, None, :, None] * w[:, :, None, :]).reshape(h * d, h * d)


def kernel(x, c, w_ada, b_ada, g_mix, g_ffn, g_final, w_in, conv_w, conv_b, lru_wr, lru_br,
           lru_wi, lru_bi, lru_lambda, gla_wa2, gla_ba, gla_gnorm, w_out, w_coarse, b_coarse,
           w_fine, b_fine, w_gate, w_up, w_down):
    bsz, seq, d = x.shape
    assert bsz == BATCH and d == D_MODEL and seq % (2 * GLA_CHUNK) == 0
    assert w_ada.shape[0] == 1, "single-layer problem"
    n = bsz * seq
    row = lambda v: v.reshape(1, -1)

    mod = _ada(c, w_ada[0], b_ada[0])
    sh1, sc1, gt1, sh2, sc2, gt2 = jnp.split(mod, 6, axis=-1)

    w_in_p = jnp.pad(w_in[0], ((0, 0), (0, D_IN_PAD - w_in.shape[2]))).astype(BF16)
    wa2_p = jnp.pad(gla_wa2[0], ((0, LANES - GLA_GATE_RANK), (0, 0))).astype(BF16)
    x1 = _mixer(x, 1.0 + sc1, sh1, gt1, row(g_mix[0]), w_in_p, conv_w[0], row(conv_b[0]),
                _block_diag(lru_wr[0]).astype(BF16), row(lru_br[0]),
                _block_diag(lru_wi[0]).astype(BF16), row(lru_bi[0]), row(lru_lambda[0]),
                wa2_p, row(gla_ba[0]), row(gla_gnorm[0]), w_out[0].astype(BF16))

    pad_rows = ROUTE_ROWS - N_EXPERTS - N_GROUPS
    w_rt = jnp.pad(jnp.concatenate([w_fine[0], w_coarse[0]], axis=1).T,
                   ((0, pad_rows), (0, 0))).astype(BF16)
    b_rt = jnp.pad(jnp.concatenate([b_fine[0], b_coarse[0]]), (0, pad_rows)).reshape(-1, 1)
    h2, meta, meta_t, cnt = _router(x1, 1.0 + sc2, sh2, row(g_ffn[0]), w_rt, b_rt)

    counts = cnt[:, 0].astype(jnp.int32)
    padded = (counts + MOE_BLOCK - 1) // MOE_BLOCK * MOE_BLOCK
    pends = jnp.cumsum(padded)
    offs = pends - padded
    cap = (2 * n + MOE_BLOCK - 1) // MOE_BLOCK * MOE_BLOCK + N_EXPERTS * MOE_BLOCK
    n_blocks = cap // MOE_BLOCK
    starts = jnp.arange(n_blocks, dtype=jnp.int32) * MOE_BLOCK
    block_e = jnp.minimum(jnp.sum((pends[None, :] <= starts[:, None]).astype(jnp.int32), axis=1),
                          N_EXPERTS - 1)
    own = block_e[:, None] == jnp.arange(N_EXPERTS, dtype=jnp.int32)[None, :]
    seg_end = jnp.sum(jnp.where(own, (offs + counts)[None, :], 0), axis=1)
    n_valid = jnp.clip(seg_end - starts, 0, MOE_BLOCK).astype(jnp.int32)
    dest = _slots(offs.astype(jnp.int32), meta_t)

    xs = _dispatch(n_valid, dest, h2.reshape(n, PACK, LANES), cap)
    ys = _experts(block_e, n_valid, xs.reshape(cap * PACK, LANES),
                  w_gate[0], w_up[0], w_down[0])
    return _combine(dest, x1, meta, gt2, row(g_final), ys)
```

```python
import functools

import jax
import jax.numpy as jnp
import numpy as np
from jax import lax
from jax.experimental import pallas as pl
from jax.experimental.pallas import tpu as pltpu

F32 = jnp.float32
BF16 = jnp.bfloat16

SUBLANES = 8
LANES = 128
VMEM_LIMIT_BYTES = 56 * 1024 * 1024

D_MODEL = 1024
BATCH = SUBLANES
D_LRU = 512
LRU_HEADS = 8
CONV_WIDTH = 4
LRU_C = 8.0
D_GLA = 512
GLA_HEADS = 4
GLA_DV = 128
GLA_DK = 64
GLA_GATE_RANK = 16
GLA_GATE_NORM = 16.0
GLA_CHUNK = 64
N_GROUPS = 4
EXPERTS_PER_GROUP = 8
N_EXPERTS = 32
D_EXPERT = 512
MOE_BLOCK = 512
EPS = 1e-6

QK = GLA_HEADS * GLA_DK
ROWS = GLA_CHUNK * BATCH
HALO = (CONV_WIDTH - 1) * BATCH
SLAB = D_MODEL // LANES
PACK = D_MODEL // (2 * LANES)
C_LX, C_LY, C_Q, C_K, C_V, C_G, C_GL = 0, 512, 1024, 1280, 1536, 2048, 2560
D_IN_PAD = 2688
ROUTE_ROWS = N_EXPERTS + SUBLANES
TOK_TILE = 512
DOWN_CHUNK = 256
PROJ_CHUNK = 256
COMBINE_CHUNKS = 8
XPITCH = 72


def _dot(a, b):
    return jnp.dot(a, b, preferred_element_type=F32)


def _dot_nt(a, b):
    return lax.dot_general(a, b, (((1,), (1,)), ((), ())), preferred_element_type=F32)


def _softplus(z):
    return jnp.maximum(z, 0.0) + jnp.log1p(jnp.exp(-jnp.abs(z)))


def _sigmoid(z):
    return 0.5 * jnp.tanh(0.5 * z) + 0.5


def _rms(x):
    return x * lax.rsqrt(jnp.mean(x * x, axis=-1, keepdims=True) + EPS)


def _pack_rows(ref, val, rows):
    for s in range(PACK):
        lo = val[:, (2 * s) * LANES:(2 * s + 1) * LANES].astype(BF16).astype(F32)
        hi = val[:, (2 * s + 1) * LANES:(2 * s + 2) * LANES].astype(BF16).astype(F32)
        word = (lax.bitcast_convert_type(lo, jnp.uint32) >> 16) | lax.bitcast_convert_type(
            hi, jnp.uint32)
        ref[pl.ds(s, rows, stride=PACK), :] = word


def _unpack_rows(ref, rows):
    cols = []
    for s in range(PACK):
        word = ref[pl.ds(s, rows, stride=PACK), :]
        cols.append(lax.bitcast_convert_type(word << 16, F32))
        cols.append(lax.bitcast_convert_type(word & jnp.uint32(0xFFFF0000), F32))
    return jnp.concatenate(cols, axis=1)


def _slab_bcast(v, n):
    c = v.shape[-1]
    return jnp.broadcast_to(v[None], (n, SUBLANES, c)).reshape(n * SUBLANES, c)


def _seq_major_copies(hbm, buf, sem, step, to_hbm):
    copies = []
    t0 = pl.multiple_of(step * GLA_CHUNK, GLA_CHUNK)
    for b in range(BATCH):
        for l in range(SLAB):
            h = hbm.at[b, pl.ds(t0, GLA_CHUNK), pl.ds(l * LANES, LANES)]
            v = buf.at[l, pl.ds(b * XPITCH, GLA_CHUNK)]
            copies.append(pltpu.make_async_copy(v, h, sem) if to_hbm
                          else pltpu.make_async_copy(h, v, sem))
    return copies


def _start_all(copies):
    for c in copies:
        c.start()


def _wait_all(copies):
    for c in copies:
        c.wait()


def _ada_kernel(c_ref, w_ref, b_ref, o_ref):
    c = c_ref[...]
    s = c * _sigmoid(c)
    o_ref[...] = jnp.dot(s, w_ref[...], preferred_element_type=F32,
                         precision=lax.Precision.HIGHEST) + b_ref[...]


def _ada(c, w, b):
    n_out = w.shape[1]
    tn = 1024
    return pl.pallas_call(
        _ada_kernel,
        grid=(n_out // tn,),
        in_specs=[pl.BlockSpec((BATCH, D_MODEL), lambda j: (0, 0)),
                  pl.BlockSpec((D_MODEL, tn), lambda j: (0, j)),
                  pl.BlockSpec((1, tn), lambda j: (0, j))],
        out_specs=pl.BlockSpec((BATCH, tn), lambda j: (0, j)),
        out_shape=jax.ShapeDtypeStruct((BATCH, n_out), F32),
        name="ada",
    )(c, w, b.reshape(1, n_out))


def _mixer_kernel(x_ref, sc_ref, sh_ref, gt_ref, gmix_ref, win_ref, cw_ref, cb_ref,
                  wr_ref, br_ref, wi_ref, bi_ref, lam_ref, wa2_ref, ba_ref, gn_ref, wout_ref,
                  sc2_ref, sh2_ref, gffn_ref, wrt_ref, brt_ref,
                  o_ref, h2_ref, meta_ref, metat_ref, cnt_ref,
                  xbuf, a_s, u_s, hs_s, hc_s, la_s, bc_s, st_s, cm_s, sel_s, hb_s,
                  xin_a, xin_b, x_a, x_b, p_a, p_b, base_s, tri_s, xsem):
    j = pl.program_id(0)
    n_tiles = 2 * pl.num_programs(0)
    nt = GLA_CHUNK
    side = ((xin_a, x_a, p_a), (xin_b, x_b, p_b))

    def copies(tile, sd):
        return _seq_major_copies(x_ref, side[sd][0], xsem.at[sd], tile, False)

    def normalise(sd):
        xin, x_s, _ = side[sd]
        for t in range(nt):
            for l in range(SLAB):
                x_s[t * BATCH:(t + 1) * BATCH, l * LANES:(l + 1) * LANES] = (
                    xin[l, pl.ds(t, BATCH, stride=XPITCH), :])
        y = _rms(x_s[...]) * gmix_ref[...]
        hb_s[...] = (y * _slab_bcast(sc_ref[...], nt) + _slab_bcast(sh_ref[...], nt)).astype(BF16)

    def project(sd, c0, c1):
        side[sd][2][:, c0:c1] = _dot(hb_s[...], win_ref[:, c0:c1])

    def input_half(sd):
        pieces = [lambda: normalise(sd)]
        for c0 in range(0, D_IN_PAD, PROJ_CHUNK):
            pieces.append(functools.partial(project, sd, c0, min(c0 + PROJ_CHUNK, D_IN_PAD)))
        return pieces

    def route_pieces(half):
        rows = slice(half * ROWS, (half + 1) * ROWS)
        keep = {}

        def logits_piece():
            h2 = (_rms(o_ref[rows, :]) * gffn_ref[...] * _slab_bcast(sc2_ref[...], nt)
                  + _slab_bcast(sh2_ref[...], nt))
            _pack_rows(h2_ref.at[pl.ds(half * ROWS * PACK, ROWS * PACK)], h2, ROWS)
            keep["logits"] = _dot_nt(wrt_ref[...], h2.astype(BF16)) + brt_ref[...]

        def select_piece():
            logits = keep["logits"]
            fl_all = logits[0:N_EXPERTS, :]
            cl_all = logits[N_EXPERTS:ROUTE_ROWS, :]
            neg = jnp.float32(-jnp.inf)

            def first_max(vals, idx_rows):
                m = jnp.max(vals, axis=0, keepdims=True)
                idx = jnp.min(jnp.where(vals == m, idx_rows, vals.shape[0]), axis=0,
                              keepdims=True)
                return m, idx

            crow = lax.broadcasted_iota(jnp.int32, cl_all.shape, 0)
            cl = jnp.where(crow < N_GROUPS, cl_all, neg)
            cmax, grp = first_max(cl, crow)
            p_grp = 1.0 / jnp.sum(jnp.exp(cl - cmax), axis=0, keepdims=True)
            frow = lax.broadcasted_iota(jnp.int32, fl_all.shape, 0)
            fl = jnp.where((frow >> 3) == grp, fl_all, neg)
            f1, i1 = first_max(fl, frow)
            f2, i2 = first_max(jnp.where(frow == i1, neg, fl), frow)
            z = jnp.sum(jnp.exp(fl - f1), axis=0, keepdims=True)
            tp1 = 1.0 / z
            tp2 = jnp.exp(f2 - f1) / z
            w1 = p_grp * tp1 / (tp1 + tp2)
            w2 = p_grp * tp2 / (tp1 + tp2)

            hit1 = frow == i1
            hit2 = frow == i2
            assign = jnp.where(hit1 | hit2, 1.0, 0.0)
            before = _dot(assign.astype(BF16), tri_s[...]) + base_s[:, 0:1]
            rank1 = jnp.sum(jnp.where(hit1, before, 0.0), axis=0, keepdims=True)
            rank2 = jnp.sum(jnp.where(hit2, before, 0.0), axis=0, keepdims=True)
            base_s[...] = base_s[...] + jnp.sum(assign, axis=1, keepdims=True)
            cnt_ref[...] = base_s[...]

            parts = [i1.astype(F32), i2.astype(F32), rank1, rank2, w1, w2]
            meta_t = jnp.concatenate(parts + [jnp.zeros((LANES - len(parts), ROWS), F32)], axis=0)
            metat_ref[:, rows] = meta_t[0:SUBLANES, :]
            meta_ref[rows, :] = meta_t.T

        return [logits_piece, select_piece]

    def recurrent_half(sd, out_rows, fill):
        _, x_s, p_s = side[sd]
        fill = list(fill)

        def next_piece():
            if fill:
                fill.pop(0)()

        def proj(c0, c1):
            return p_s[:, c0:c1]

        next_piece()
        xbuf[HALO:HALO + ROWS, :] = proj(C_LX, C_LY)
        cw = cw_ref[...]
        cx = cb_ref[...] + sum(cw[k:k + 1, :] * xbuf[k * BATCH:k * BATCH + ROWS, :]
                               for k in range(CONV_WIDTH))
        xbuf[0:HALO, :] = xbuf[ROWS:ROWS + HALO, :]
        cxb = cx.astype(BF16)
        next_piece()

        def gate(w_ref, b_ref):
            hw = D_LRU // 2
            pre = jnp.concatenate([_dot(cxb[:, 0:hw], w_ref[0:hw, 0:hw]),
                                   _dot(cxb[:, hw:], w_ref[hw:, hw:])], axis=1)
            return _sigmoid(pre + b_ref[...])

        r_gate = gate(wr_ref, br_ref)
        next_piece()
        i_gate = gate(wi_ref, bi_ref)
        next_piece()
        decay_rate = (-LRU_C) * _softplus(-lam_ref[...])
        for rows in (slice(0, ROWS // 2), slice(ROWS // 2, ROWS)):
            log_a = r_gate[rows] * decay_rate
            a_s[rows, :] = jnp.exp(log_a)
            th = jnp.tanh(log_a)
            u_s[rows, :] = jnp.sqrt(-2.0 * th / (1.0 - th)) * (i_gate[rows] * cx[rows])
            next_piece()
        hcur = hc_s[...]
        for t in range(nt):
            sl = slice(t * BATCH, (t + 1) * BATCH)
            hcur = a_s[sl, :] * hcur + u_s[sl, :]
            hs_s[sl, :] = hcur
        hc_s[...] = hcur
        next_piece()
        lru_out = hs_s[...] * jax.nn.gelu(proj(C_LY, C_Q), approximate=True)
        next_piece()

        gate_lr = proj(C_GL, D_IN_PAD).astype(BF16)
        z = _dot(gate_lr, wa2_ref[...]) + ba_ref[...]
        la_s[...] = -_softplus(-z) * (1.0 / GLA_GATE_NORM)
        next_piece()
        bcur = jnp.zeros((BATCH, QK), F32)
        for t in range(nt):
            sl = slice(t * BATCH, (t + 1) * BATCH)
            bcur = bcur + la_s[sl, :]
            bc_s[sl, :] = bcur
        next_piece()
        bc = bc_s[...]
        e_last = jnp.exp(bcur)
        q_dec = proj(C_Q, C_K) * (GLA_DK ** -0.5) * jnp.exp(bc)
        kk = proj(C_K, C_V)
        k_dec = kk * jnp.exp(-bc)
        k_last = kk * jnp.exp(_slab_bcast(bcur, nt) - bc)
        next_piece()
        vv = proj(C_V, C_G)
        gg = proj(C_G, C_GL)
        next_piece()
        hr = ROWS // 2
        causal_top = cm_s[0:hr, 0:hr] > 0.5
        causal_bot = cm_s[hr:ROWS, :] > 0.5

        lane = lax.broadcasted_iota(jnp.int32, (SUBLANES, LANES), 1)
        sub = lax.broadcasted_iota(jnp.int32, (SUBLANES, LANES), 0)
        half = lane >> 6
        seq_sel = [(sub == 2 * jj + half).astype(F32) for jj in range(BATCH // 2)]

        def expand(m):
            mb = m.astype(BF16)
            return jnp.concatenate([mb * sel_s[jj] for jj in range(BATCH // 2)], axis=1)

        def both_halves(m, hh):
            keep = (lax.broadcasted_iota(jnp.int32, m.shape, 1) >> 6) == hh
            mh = jnp.where(keep, m, 0.0)
            return mh, mh + pltpu.roll(mh, GLA_DK, axis=1)

        gla_parts = []
        for hd in range(GLA_HEADS):
            p, hh = hd // 2, hd % 2
            pc = slice(p * LANES, (p + 1) * LANES)
            qh, q_both = both_halves(q_dec[:, pc], hh)
            _, k_both = both_halves(k_last[:, pc], hh)
            _, e_both = both_halves(e_last[:, pc], hh)
            v_h = vv[:, hd * GLA_DV:(hd + 1) * GLA_DV]
            v_hb = v_h.astype(BF16)
            qb = qh.astype(BF16)
            kb = k_dec[:, pc].astype(BF16)
            s_top = jnp.where(causal_top, _dot_nt(qb[0:hr], kb[0:hr]), 0.0)
            s_bot = jnp.where(causal_bot, _dot_nt(qb[hr:ROWS], kb), 0.0)
            o_h = jnp.concatenate([_dot(s_top.astype(BF16), v_hb[0:hr]),
                                   _dot(s_bot.astype(BF16), v_hb)], axis=0)
            st = st_s[hd]
            o_h = o_h + _dot_nt(expand(q_both), st.astype(BF16))
            kv_t = _dot(v_h.T.astype(BF16), expand(k_both))
            decay = jnp.concatenate(
                [jnp.sum(e_both * sq, axis=0, keepdims=True) for sq in seq_sel], axis=1)
            st_s[hd] = st * decay + kv_t
            o_n = _rms(o_h) * gn_ref[...]
            g_h = gg[:, hd * GLA_DV:(hd + 1) * GLA_DV]
            gla_parts.append(o_n * (g_h * _sigmoid(g_h)))
            if hd < 2:
                next_piece()

        mix_in = jnp.concatenate([lru_out] + gla_parts, axis=1).astype(BF16)
        mix = _dot(mix_in, wout_ref[...])
        o_ref[out_rows, :] = x_s[...] + _slab_bcast(gt_ref[...], nt) * mix
        assert not fill

    @pl.when(j == 0)
    def _init():
        _start_all(copies(0, 0))
        _start_all(copies(1, 1))
        xbuf[0:HALO, :] = jnp.zeros((HALO, D_LRU), F32)
        hc_s[...] = jnp.zeros_like(hc_s)
        st_s[...] = jnp.zeros_like(st_s)
        r = lax.broadcasted_iota(jnp.int32, (ROWS, ROWS), 0)
        c = lax.broadcasted_iota(jnp.int32, (ROWS, ROWS), 1)
        same_seq = (r & (BATCH - 1)) == (c & (BATCH - 1))
        cm_s[...] = jnp.where(same_seq & ((c >> 3) <= (r >> 3)), 1.0, 0.0).astype(F32)
        rr = lax.broadcasted_iota(jnp.int32, (ROWS, LANES), 0) & (BATCH - 1)
        ll = lax.broadcasted_iota(jnp.int32, (ROWS, LANES), 1) >> 6
        for jj in range(BATCH // 2):
            sel_s[jj] = jnp.where(rr == 2 * jj + ll, 1.0, 0.0).astype(BF16)
        base_s[...] = jnp.zeros_like(base_s)
        tri_s[...] = jnp.where(r < c, 1.0, 0.0).astype(BF16)
        _wait_all(copies(0, 0))
        for piece in input_half(0):
            piece()

    @pl.when(2 * j + 2 < n_tiles)
    def _fetch_next_even():
        _start_all(copies(2 * j + 2, 0))

    _wait_all(copies(2 * j + 1, 1))
    recurrent_half(0, slice(0, ROWS), input_half(1))

    @pl.when(2 * j + 2 < n_tiles)
    def _next_even_arrived():
        _wait_all(copies(2 * j + 2, 0))

    @pl.when(2 * j + 3 < n_tiles)
    def _next_odd():
        _start_all(copies(2 * j + 3, 1))

    recurrent_half(1, slice(ROWS, 2 * ROWS), input_half(0) + route_pieces(0))
    for piece in route_pieces(1):
        piece()


def _const_spec(shape):
    nd = len(shape)
    return pl.BlockSpec(shape, lambda i: (0,) * nd)


def _mixer(x, sc1p, sh1, gt1, g_mix, w_in_p, conv_w, conv_b, wr_d, br, wi_d, bi, lam,
           wa2_p, ba, gn, w_out_b, sc2p, sh2, g_ffn, w_rt, b_rt):
    n = x.shape[0] * x.shape[1]
    consts = [sc1p, sh1, gt1, g_mix, w_in_p, conv_w, conv_b, wr_d, br, wi_d, bi, lam,
              wa2_p, ba, gn, w_out_b, sc2p, sh2, g_ffn, w_rt, b_rt]
    seq_buf = pltpu.VMEM((SLAB, BATCH * XPITCH, LANES), F32)
    return pl.pallas_call(
        _mixer_kernel,
        grid=(n // (2 * ROWS),),
        in_specs=[pl.BlockSpec(memory_space=pl.ANY)]
        + [_const_spec(a.shape) for a in consts],
        out_specs=[pl.BlockSpec((2 * ROWS, D_MODEL), lambda i: (i, 0)),
                   pl.BlockSpec((2 * ROWS * PACK, LANES), lambda i: (i, 0)),
                   pl.BlockSpec((2 * ROWS, LANES), lambda i: (i, 0)),
                   pl.BlockSpec((SUBLANES, 2 * ROWS), lambda i: (0, i)),
                   pl.BlockSpec((N_EXPERTS, LANES), lambda i: (0, 0))],
        out_shape=[jax.ShapeDtypeStruct((n, D_MODEL), F32),
                   jax.ShapeDtypeStruct((n * PACK, LANES), jnp.uint32),
                   jax.ShapeDtypeStruct((n, LANES), F32),
                   jax.ShapeDtypeStruct((SUBLANES, n), F32),
                   jax.ShapeDtypeStruct((N_EXPERTS, LANES), F32)],
        scratch_shapes=[
            pltpu.VMEM((HALO + ROWS, D_LRU), F32),
            pltpu.VMEM((ROWS, D_LRU), F32),
            pltpu.VMEM((ROWS, D_LRU), F32),
            pltpu.VMEM((ROWS, D_LRU), F32),
            pltpu.VMEM((BATCH, D_LRU), F32),
            pltpu.VMEM((ROWS, QK), F32),
            pltpu.VMEM((ROWS, QK), F32),
            pltpu.VMEM((GLA_HEADS, GLA_DV, BATCH * GLA_DK), F32),
            pltpu.VMEM((ROWS, ROWS), F32),
            pltpu.VMEM((BATCH // 2, ROWS, LANES), BF16),
            pltpu.VMEM((ROWS, D_MODEL), BF16),
            seq_buf, seq_buf,
            pltpu.VMEM((ROWS, D_MODEL), F32),
            pltpu.VMEM((ROWS, D_MODEL), F32),
            pltpu.VMEM((ROWS, D_IN_PAD), F32),
            pltpu.VMEM((ROWS, D_IN_PAD), F32),
            pltpu.VMEM((N_EXPERTS, LANES), F32),
            pltpu.VMEM((ROWS, ROWS), BF16),
            pltpu.SemaphoreType.DMA((2,)),
        ],
        compiler_params=pltpu.CompilerParams(dimension_semantics=("arbitrary",),
                                             vmem_limit_bytes=VMEM_LIMIT_BYTES),
        name="mixer",
    )(x, *consts)


def _slots_kernel(offs_ref, mt_ref, dest_ref):
    mt = mt_ref[...]
    e = mt[0:2, :].astype(jnp.int32)
    acc = mt[2:4, :].astype(jnp.int32)
    for j in range(N_EXPERTS):
        acc = acc + jnp.where(e == j, offs_ref[j], 0)
    dest_ref[...] = acc


def _slots(offs, meta_t):
    n = meta_t.shape[1]
    tl = min(n, 8192)
    grid_spec = pltpu.PrefetchScalarGridSpec(
        num_scalar_prefetch=1,
        grid=(n // tl,),
        in_specs=[pl.BlockSpec((SUBLANES, tl), lambda i, offs: (0, i))],
        out_specs=pl.BlockSpec((2, tl), lambda i, offs: (0, i)),
    )
    return pl.pallas_call(
        _slots_kernel,
        grid_spec=grid_spec,
        out_shape=jax.ShapeDtypeStruct((2, n), jnp.int32),
        name="slots",
    )(offs, meta_t)


def _row_copy(src, dst, sem):
    return pltpu.make_async_copy(src, dst, sem)


def _dispatch_kernel(nv_ref, dest_ref, h_ref, xs_ref, zbuf, hbuf, zsem, lsem, sem):
    n_blocks = xs_ref.shape[0] // MOE_BLOCK

    @pl.when(pl.program_id(0) == 0)
    def _zero_partial_blocks():
        zbuf[...] = jnp.zeros_like(zbuf)

        def fill(b):
            return _row_copy(zbuf, xs_ref.at[pl.ds(pl.multiple_of(b * MOE_BLOCK, MOE_BLOCK),
                                                   MOE_BLOCK)], zsem)

        def start(b, carry):
            @pl.when(nv_ref[b] < MOE_BLOCK)
            def _():
                fill(b).start()
            return carry

        def wait(b, carry):
            @pl.when(nv_ref[b] < MOE_BLOCK)
            def _():
                fill(b).wait()
            return carry

        lax.fori_loop(0, n_blocks, start, 0)
        lax.fori_loop(0, n_blocks, wait, 0)

    i = pl.program_id(0)
    n_steps = pl.num_programs(0)
    cur = lax.rem(i, 2)

    def load(step, slot):
        rows = pl.ds(pl.multiple_of(step * TOK_TILE, TOK_TILE), TOK_TILE)
        return _row_copy(h_ref.at[rows], hbuf.at[slot], lsem.at[slot])

    def wait_rows(slot):
        for k in range(2):
            _row_copy(hbuf.at[slot], xs_ref.at[pl.ds(0, TOK_TILE)], sem.at[slot]).wait()

    @pl.when(i == 0)
    def _first_load():
        load(0, 0).start()

    @pl.when(i > 0)
    def _drain_previous():
        wait_rows(1 - cur)

    @pl.when(i + 1 < n_steps)
    def _prefetch():
        load(i + 1, 1 - cur).start()

    load(i, cur).wait()

    for r in range(TOK_TILE):
        for k in range(2):
            _row_copy(hbuf.at[cur, r], xs_ref.at[dest_ref[k, r]], sem.at[cur]).start(priority=k)

    @pl.when(i == n_steps - 1)
    def _drain_last():
        wait_rows(cur)


def _dispatch(n_valid, dest, h2s, cap):
    n = h2s.shape[0]
    grid_spec = pltpu.PrefetchScalarGridSpec(
        num_scalar_prefetch=1,
        grid=(n // TOK_TILE,),
        in_specs=[pl.BlockSpec((2, TOK_TILE), lambda i, nv: (0, i), memory_space=pltpu.SMEM),
                  pl.BlockSpec(memory_space=pl.ANY)],
        out_specs=pl.BlockSpec(memory_space=pl.ANY),
        scratch_shapes=[pltpu.VMEM((MOE_BLOCK, PACK, LANES), jnp.uint32),
                        pltpu.VMEM((2, TOK_TILE, PACK, LANES), jnp.uint32),
                        pltpu.SemaphoreType.DMA(()),
                        pltpu.SemaphoreType.DMA((2,)),
                        pltpu.SemaphoreType.DMA((2,))],
    )
    return pl.pallas_call(
        _dispatch_kernel,
        grid_spec=grid_spec,
        out_shape=jax.ShapeDtypeStruct((cap, PACK, LANES), jnp.uint32),
        compiler_params=pltpu.CompilerParams(dimension_semantics=("arbitrary",)),
        name="dispatch",
    )(n_valid, dest, h2s)


def _expert_kernel(be_ref, nv_ref, xs_ref, wg_ref, wu_ref, wd_ref, ys_ref, wg_b, wu_b, wd_b):
    i = pl.program_id(0)
    nv = nv_ref[i]

    @pl.when((i == 0) | (be_ref[i] != be_ref[jnp.maximum(i - 1, 0)]))
    def _cast_weights():
        wg_b[...] = wg_ref[0].astype(BF16)
        wu_b[...] = wu_ref[0].astype(BF16)
        wd_b[...] = wd_ref[0].astype(BF16)

    @pl.when(nv > 0)
    def _compute():
        xb = _unpack_rows(xs_ref, MOE_BLOCK).astype(BF16)
        g = _dot(xb, wg_b[...])
        u = _dot(xb, wu_b[...])
        hid = (g * _sigmoid(g) * u).astype(BF16)
        for c in range(D_MODEL // DOWN_CHUNK):
            cols = slice(c * DOWN_CHUNK, (c + 1) * DOWN_CHUNK)
            out = _dot(hid, wd_b[:, cols])
            for j in range(DOWN_CHUNK // LANES):
                s = c * (DOWN_CHUNK // LANES) + j
                ys_ref[pl.ds(s, MOE_BLOCK, stride=SLAB), :] = out[:, j * LANES:(j + 1) * LANES]

    @pl.when(nv == 0)
    def _skip():
        ys_ref[...] = jnp.zeros_like(ys_ref)


def _experts(block_e, n_valid, xs, wg, wu, wd):
    n_blocks = xs.shape[0] // (MOE_BLOCK * PACK)
    grid_spec = pltpu.PrefetchScalarGridSpec(
        num_scalar_prefetch=2,
        grid=(n_blocks,),
        in_specs=[pl.BlockSpec((MOE_BLOCK * PACK, LANES), lambda i, be, nv: (i, 0)),
                  pl.BlockSpec((1, D_MODEL, D_EXPERT), lambda i, be, nv: (be[i], 0, 0)),
                  pl.BlockSpec((1, D_MODEL, D_EXPERT), lambda i, be, nv: (be[i], 0, 0)),
                  pl.BlockSpec((1, D_EXPERT, D_MODEL), lambda i, be, nv: (be[i], 0, 0))],
        out_specs=pl.BlockSpec((MOE_BLOCK * SLAB, LANES), lambda i, be, nv: (i, 0)),
        scratch_shapes=[pltpu.VMEM((D_MODEL, D_EXPERT), BF16),
                        pltpu.VMEM((D_MODEL, D_EXPERT), BF16),
                        pltpu.VMEM((D_EXPERT, D_MODEL), BF16)],
    )
    return pl.pallas_call(
        _expert_kernel,
        grid_spec=grid_spec,
        out_shape=jax.ShapeDtypeStruct((n_blocks * MOE_BLOCK * SLAB, LANES), F32),
        compiler_params=pltpu.CompilerParams(dimension_semantics=("arbitrary",),
                                             vmem_limit_bytes=VMEM_LIMIT_BYTES),
        name="experts",
    )(block_e, n_valid, xs, wg, wu, wd)


def _combine_kernel(dcur_ref, dnxt_ref, x_ref, meta_ref, gt_ref, gfin_ref, ys_ref, ys2d_ref,
                    o_ref, buf, sem, obuf, osem):
    i = pl.program_id(0)
    n_steps = pl.num_programs(0)
    cur = lax.rem(i, 2)
    nxt = 1 - cur

    def start_gather(dest_ref, slot, r):
        for k in range(2):
            dst = buf.at[slot, k, pl.ds(r * SLAB, SLAB)]
            _row_copy(ys_ref.at[dest_ref[k, r]], dst, sem.at[slot]).start(priority=k)

    def wait_gather(slot):
        for k in range(2):
            _row_copy(ys2d_ref.at[pl.ds(0, TOK_TILE * SLAB)], buf.at[slot, k], sem.at[slot]).wait()

    @pl.when(i == 0)
    def _first_gather():
        for r in range(TOK_TILE):
            start_gather(dcur_ref, 0, r)

    wait_gather(cur)

    @pl.when(i >= 2)
    def _drain_older():
        _wait_all(_seq_major_copies(o_ref, obuf.at[cur], osem.at[cur], i - 2, True))

    chunk = TOK_TILE // COMBINE_CHUNKS
    for c in range(COMBINE_CHUNKS):
        for r in range(c * chunk, (c + 1) * chunk):
            start_gather(dnxt_ref, nxt, r)
        rows = slice(c * chunk, (c + 1) * chunk)
        meta = meta_ref[rows, :]

        def expert_rows(k):
            return jnp.concatenate(
                [buf[cur, k, pl.ds(c * chunk * SLAB + s, chunk, stride=SLAB), :]
                 for s in range(SLAB)], axis=1)

        y = meta[:, 4:5] * expert_rows(0) + meta[:, 5:6] * expert_rows(1)
        x2 = x_ref[rows, :] + _slab_bcast(gt_ref[...], chunk // BATCH) * y
        out = _rms(x2) * gfin_ref[...]
        for tt in range(chunk // BATCH):
            t = c * (chunk // BATCH) + tt
            for l in range(SLAB):
                obuf[cur, l, pl.ds(t, BATCH, stride=XPITCH), :] = (
                    out[tt * BATCH:(tt + 1) * BATCH, l * LANES:(l + 1) * LANES])
    _start_all(_seq_major_copies(o_ref, obuf.at[cur], osem.at[cur], i, True))

    @pl.when(i == n_steps - 1)
    def _drain_last():
        wait_gather(nxt)

        @pl.when(i >= 1)
        def _():
            _wait_all(_seq_major_copies(o_ref, obuf.at[1 - cur], osem.at[1 - cur], i - 1, True))

        _wait_all(_seq_major_copies(o_ref, obuf.at[cur], osem.at[cur], i, True))


def _combine(dest, x1, meta, gt2, g_final, ys):
    n = x1.shape[0]
    assert TOK_TILE == ROWS
    last = n // TOK_TILE - 1
    return pl.pallas_call(
        _combine_kernel,
        grid=(n // TOK_TILE,),
        in_specs=[pl.BlockSpec((2, TOK_TILE), lambda i: (0, i), memory_space=pltpu.SMEM),
                  pl.BlockSpec((2, TOK_TILE), lambda i: (0, jnp.minimum(i + 1, last)),
                               memory_space=pltpu.SMEM),
                  pl.BlockSpec((TOK_TILE, D_MODEL), lambda i: (i, 0)),
                  pl.BlockSpec((TOK_TILE, LANES), lambda i: (i, 0)),
                  _const_spec(gt2.shape),
                  _const_spec(g_final.shape),
                  pl.BlockSpec(memory_space=pl.ANY),
                  pl.BlockSpec(memory_space=pl.ANY)],
        out_specs=pl.BlockSpec(memory_space=pl.ANY),
        out_shape=jax.ShapeDtypeStruct((BATCH, n // BATCH, D_MODEL), F32),
        scratch_shapes=[pltpu.VMEM((2, 2, TOK_TILE * SLAB, LANES), F32),
                        pltpu.SemaphoreType.DMA((2,)),
                        pltpu.VMEM((2, SLAB, BATCH * XPITCH, LANES), F32),
                        pltpu.SemaphoreType.DMA((2,))],
        compiler_params=pltpu.CompilerParams(dimension_semantics=("arbitrary",),
                                             vmem_limit_bytes=VMEM_LIMIT_BYTES),
        name="combine",
    )(dest, dest, x1, meta, gt2, g_final, ys.reshape(-1, SLAB, LANES), ys)


def _block_diag(w):
    h, d, _ = w.shape
    eye = jnp.eye(h, dtype=w.dtype)
    return (eye[:, None, :, None] * w[:, :, None, :]).reshape(h * d, h * d)


def kernel(x, c, w_ada, b_ada, g_mix, g_ffn, g_final, w_in, conv_w, conv_b, lru_wr, lru_br,
           lru_wi, lru_bi, lru_lambda, gla_wa2, gla_ba, gla_gnorm, w_out, w_coarse, b_coarse,
           w_fine, b_fine, w_gate, w_up, w_down):
    bsz, seq, d = x.shape
    assert bsz == BATCH and d == D_MODEL and seq % (2 * GLA_CHUNK) == 0
    assert w_ada.shape[0] == 1, "single-layer problem"
    n = bsz * seq
    row = lambda v: v.reshape(1, -1)

    mod = _ada(c, w_ada[0], b_ada[0])
    sh1, sc1, gt1, sh2, sc2, gt2 = jnp.split(mod, 6, axis=-1)

    w_in_p = jnp.pad(w_in[0], ((0, 0), (0, D_IN_PAD - w_in.shape[2]))).astype(BF16)
    wa2_p = jnp.pad(gla_wa2[0], ((0, LANES - GLA_GATE_RANK), (0, 0))).astype(BF16)
    pad_rows = ROUTE_ROWS - N_EXPERTS - N_GROUPS
    w_rt = jnp.pad(jnp.concatenate([w_fine[0], w_coarse[0]], axis=1).T,
                   ((0, pad_rows), (0, 0))).astype(BF16)
    b_rt = jnp.pad(jnp.concatenate([b_fine[0], b_coarse[0]]), (0, pad_rows)).reshape(-1, 1)
    x1, h2, meta, meta_t, cnt = _mixer(
        x, 1.0 + sc1, sh1, gt1, row(g_mix[0]), w_in_p, conv_w[0], row(conv_b[0]),
        _block_diag(lru_wr[0]).astype(BF16), row(lru_br[0]),
        _block_diag(lru_wi[0]).astype(BF16), row(lru_bi[0]), row(lru_lambda[0]),
        wa2_p, row(gla_ba[0]), row(gla_gnorm[0]), w_out[0].astype(BF16),
        1.0 + sc2, sh2, row(g_ffn[0]), w_rt, b_rt)

    counts = cnt[:, 0].astype(jnp.int32)
    padded = (counts + MOE_BLOCK - 1) // MOE_BLOCK * MOE_BLOCK
    pends = jnp.cumsum(padded)
    offs = pends - padded
    cap = (2 * n + MOE_BLOCK - 1) // MOE_BLOCK * MOE_BLOCK + N_EXPERTS * MOE_BLOCK
    n_blocks = cap // MOE_BLOCK
    starts = jnp.arange(n_blocks, dtype=jnp.int32) * MOE_BLOCK
    block_e = jnp.minimum(jnp.sum((pends[None, :] <= starts[:, None]).astype(jnp.int32), axis=1),
                          N_EXPERTS - 1)
    own = block_e[:, None] == jnp.arange(N_EXPERTS, dtype=jnp.int32)[None, :]
    seg_end = jnp.sum(jnp.where(own, (offs + counts)[None, :], 0), axis=1)
    n_valid = jnp.clip(seg_end - starts, 0, MOE_BLOCK).astype(jnp.int32)
    dest = _slots(offs.astype(jnp.int32), meta_t)

    xs = _dispatch(n_valid, dest, h2.reshape(n, PACK, LANES), cap)
    ys = _experts(block_e, n_valid, xs.reshape(cap * PACK, LANES),
                  w_gate[0], w_up[0], w_down[0])
    return _combine(dest, x1, meta, gt2, row(g_final), ys)
```

```python
import functools

import jax
import jax.numpy as jnp
import numpy as np
from jax import lax
from jax.experimental import pallas as pl
from jax.experimental.pallas import tpu as pltpu

F32 = jnp.float32
BF16 = jnp.bfloat16

SUBLANES = 8
LANES = 128
VMEM_LIMIT_BYTES = 56 * 1024 * 1024

D_MODEL = 1024
BATCH = SUBLANES
D_LRU = 512
LRU_HEADS = 8
CONV_WIDTH = 4
LRU_C = 8.0
D_GLA = 512
GLA_HEADS = 4
GLA_DV = 128
GLA_DK = 64
GLA_GATE_RANK = 16
GLA_GATE_NORM = 16.0
GLA_CHUNK = 64
N_GROUPS = 4
EXPERTS_PER_GROUP = 8
N_EXPERTS = 32
D_EXPERT = 512
MOE_BLOCK = 1024
EPS = 1e-6

QK = GLA_HEADS * GLA_DK
ROWS = GLA_CHUNK * BATCH
HALO = (CONV_WIDTH - 1) * BATCH
SLAB = D_MODEL // LANES
PACK = D_MODEL // (2 * LANES)
C_LX, C_LY, C_Q, C_K, C_V, C_G, C_GL = 0, 512, 1024, 1280, 1536, 2048, 2560
D_IN_PAD = 2688
ROUTE_ROWS = N_EXPERTS + SUBLANES
TOK_TILE = 512
DOWN_CHUNK = 256
PROJ_CHUNK = 256
COMBINE_CHUNKS = 8
XPITCH = 72


def _dot(a, b):
    return jnp.dot(a, b, preferred_element_type=F32)


def _dot_nt(a, b):
    return lax.dot_general(a, b, (((1,), (1,)), ((), ())), preferred_element_type=F32)


def _softplus(z):
    return jnp.maximum(z, 0.0) + jnp.log1p(jnp.exp(-jnp.abs(z)))


def _sigmoid(z):
    return 0.5 * jnp.tanh(0.5 * z) + 0.5


def _rms(x):
    return x * lax.rsqrt(jnp.mean(x * x, axis=-1, keepdims=True) + EPS)


def _pack_rows(ref, val, rows):
    for s in range(PACK):
        lo = val[:, (2 * s) * LANES:(2 * s + 1) * LANES].astype(BF16).astype(F32)
        hi = val[:, (2 * s + 1) * LANES:(2 * s + 2) * LANES].astype(BF16).astype(F32)
        word = (lax.bitcast_convert_type(lo, jnp.uint32) >> 16) | lax.bitcast_convert_type(
            hi, jnp.uint32)
        ref[pl.ds(s, rows, stride=PACK), :] = word


def _unpack_rows(ref, rows):
    cols = []
    for s in range(PACK):
        word = ref[pl.ds(s, rows, stride=PACK), :]
        cols.append(lax.bitcast_convert_type(word << 16, F32))
        cols.append(lax.bitcast_convert_type(word & jnp.uint32(0xFFFF0000), F32))
    return jnp.concatenate(cols, axis=1)


def _slab_bcast(v, n):
    c = v.shape[-1]
    return jnp.broadcast_to(v[None], (n, SUBLANES, c)).reshape(n * SUBLANES, c)


def _seq_major_copies(hbm, buf, sem, step, to_hbm):
    copies = []
    t0 = pl.multiple_of(step * GLA_CHUNK, GLA_CHUNK)
    for b in range(BATCH):
        for l in range(SLAB):
            h = hbm.at[b, pl.ds(t0, GLA_CHUNK), pl.ds(l * LANES, LANES)]
            v = buf.at[l, pl.ds(b * XPITCH, GLA_CHUNK)]
            copies.append(pltpu.make_async_copy(v, h, sem) if to_hbm
                          else pltpu.make_async_copy(h, v, sem))
    return copies


def _start_all(copies):
    for c in copies:
        c.start()


def _wait_all(copies):
    for c in copies:
        c.wait()


def _ada_kernel(c_ref, w_ref, b_ref, o_ref):
    c = c_ref[...]
    s = c * _sigmoid(c)
    o_ref[...] = jnp.dot(s, w_ref[...], preferred_element_type=F32,
                         precision=lax.Precision.HIGHEST) + b_ref[...]


def _ada(c, w, b):
    n_out = w.shape[1]
    tn = 1024
    return pl.pallas_call(
        _ada_kernel,
        grid=(n_out // tn,),
        in_specs=[pl.BlockSpec((BATCH, D_MODEL), lambda j: (0, 0)),
                  pl.BlockSpec((D_MODEL, tn), lambda j: (0, j)),
                  pl.BlockSpec((1, tn), lambda j: (0, j))],
        out_specs=pl.BlockSpec((BATCH, tn), lambda j: (0, j)),
        out_shape=jax.ShapeDtypeStruct((BATCH, n_out), F32),
        name="ada",
    )(c, w, b.reshape(1, n_out))


def _mixer_kernel(x_ref, sc_ref, sh_ref, gt_ref, gmix_ref, win_ref, cw_ref, cb_ref,
                  wr_ref, br_ref, wi_ref, bi_ref, lam_ref, wa2_ref, ba_ref, gn_ref, wout_ref,
                  sc2_ref, sh2_ref, gffn_ref, wrt_ref, brt_ref,
                  o_ref, h2_ref, meta_ref, metat_ref, cnt_ref,
                  xbuf, a_s, u_s, hs_s, hc_s, la_s, bc_s, st_s, cm_s, sel_s, hb_s,
                  xin_a, xin_b, x_a, x_b, p_a, p_b, base_s, tri_s, xsem):
    j = pl.program_id(0)
    n_tiles = 2 * pl.num_programs(0)
    nt = GLA_CHUNK
    side = ((xin_a, x_a, p_a), (xin_b, x_b, p_b))

    def copies(tile, sd):
        return _seq_major_copies(x_ref, side[sd][0], xsem.at[sd], tile, False)

    def normalise(sd):
        xin, x_s, _ = side[sd]
        for t in range(nt):
            for l in range(SLAB):
                x_s[t * BATCH:(t + 1) * BATCH, l * LANES:(l + 1) * LANES] = (
                    xin[l, pl.ds(t, BATCH, stride=XPITCH), :])
        y = _rms(x_s[...]) * gmix_ref[...]
        hb_s[...] = (y * _slab_bcast(sc_ref[...], nt) + _slab_bcast(sh_ref[...], nt)).astype(BF16)

    def project(sd, c0, c1):
        side[sd][2][:, c0:c1] = _dot(hb_s[...], win_ref[:, c0:c1])

    def input_half(sd):
        pieces = [lambda: normalise(sd)]
        for c0 in range(0, D_IN_PAD, PROJ_CHUNK):
            pieces.append(functools.partial(project, sd, c0, min(c0 + PROJ_CHUNK, D_IN_PAD)))
        return pieces

    def route_pieces(half):
        rows = slice(half * ROWS, (half + 1) * ROWS)
        keep = {}

        def logits_piece():
            h2 = (_rms(o_ref[rows, :]) * gffn_ref[...] * _slab_bcast(sc2_ref[...], nt)
                  + _slab_bcast(sh2_ref[...], nt))
            _pack_rows(h2_ref.at[pl.ds(half * ROWS * PACK, ROWS * PACK)], h2, ROWS)
            keep["logits"] = _dot_nt(wrt_ref[...], h2.astype(BF16)) + brt_ref[...]

        def select_piece():
            logits = keep["logits"]
            fl_all = logits[0:N_EXPERTS, :]
            cl_all = logits[N_EXPERTS:ROUTE_ROWS, :]
            neg = jnp.float32(-jnp.inf)

            def first_max(vals, idx_rows):
                m = jnp.max(vals, axis=0, keepdims=True)
                idx = jnp.min(jnp.where(vals == m, idx_rows, vals.shape[0]), axis=0,
                              keepdims=True)
                return m, idx

            crow = lax.broadcasted_iota(jnp.int32, cl_all.shape, 0)
            cl = jnp.where(crow < N_GROUPS, cl_all, neg)
            cmax, grp = first_max(cl, crow)
            p_grp = 1.0 / jnp.sum(jnp.exp(cl - cmax), axis=0, keepdims=True)
            frow = lax.broadcasted_iota(jnp.int32, fl_all.shape, 0)
            fl = jnp.where((frow >> 3) == grp, fl_all, neg)
            f1, i1 = first_max(fl, frow)
            f2, i2 = first_max(jnp.where(frow == i1, neg, fl), frow)
            z = jnp.sum(jnp.exp(fl - f1), axis=0, keepdims=True)
            tp1 = 1.0 / z
            tp2 = jnp.exp(f2 - f1) / z
            w1 = p_grp * tp1 / (tp1 + tp2)
            w2 = p_grp * tp2 / (tp1 + tp2)

            hit1 = frow == i1
            hit2 = frow == i2
            assign = jnp.where(hit1 | hit2, 1.0, 0.0)
            before = _dot(assign.astype(BF16), tri_s[...]) + base_s[:, 0:1]
            rank1 = jnp.sum(jnp.where(hit1, before, 0.0), axis=0, keepdims=True)
            rank2 = jnp.sum(jnp.where(hit2, before, 0.0), axis=0, keepdims=True)
            base_s[...] = base_s[...] + jnp.sum(assign, axis=1, keepdims=True)
            cnt_ref[...] = base_s[...]

            parts = [i1.astype(F32), i2.astype(F32), rank1, rank2, w1, w2]
            meta_t = jnp.concatenate(parts + [jnp.zeros((LANES - len(parts), ROWS), F32)], axis=0)
            metat_ref[:, rows] = meta_t[0:SUBLANES, :]
            meta_ref[rows, :] = meta_t.T

        return [logits_piece, select_piece]

    def recurrent_half(sd, out_rows, fill):
        _, x_s, p_s = side[sd]
        fill = list(fill)

        def next_piece():
            if fill:
                fill.pop(0)()

        def proj(c0, c1):
            return p_s[:, c0:c1]

        next_piece()
        xbuf[HALO:HALO + ROWS, :] = proj(C_LX, C_LY)
        cw = cw_ref[...]
        cx = cb_ref[...] + sum(cw[k:k + 1, :] * xbuf[k * BATCH:k * BATCH + ROWS, :]
                               for k in range(CONV_WIDTH))
        xbuf[0:HALO, :] = xbuf[ROWS:ROWS + HALO, :]
        cxb = cx.astype(BF16)
        next_piece()

        def gate(w_ref, b_ref):
            hw = D_LRU // 2
            pre = jnp.concatenate([_dot(cxb[:, 0:hw], w_ref[0:hw, 0:hw]),
                                   _dot(cxb[:, hw:], w_ref[hw:, hw:])], axis=1)
            return _sigmoid(pre + b_ref[...])

        r_gate = gate(wr_ref, br_ref)
        next_piece()
        i_gate = gate(wi_ref, bi_ref)
        next_piece()
        decay_rate = (-LRU_C) * _softplus(-lam_ref[...])
        for rows in (slice(0, ROWS // 2), slice(ROWS // 2, ROWS)):
            log_a = r_gate[rows] * decay_rate
            a_s[rows, :] = jnp.exp(log_a)
            th = jnp.tanh(log_a)
            u_s[rows, :] = jnp.sqrt(-2.0 * th / (1.0 - th)) * (i_gate[rows] * cx[rows])
            next_piece()
        hcur = hc_s[...]
        for t in range(nt):
            sl = slice(t * BATCH, (t + 1) * BATCH)
            hcur = a_s[sl, :] * hcur + u_s[sl, :]
            hs_s[sl, :] = hcur
        hc_s[...] = hcur
        next_piece()
        lru_out = hs_s[...] * jax.nn.gelu(proj(C_LY, C_Q), approximate=True)
        next_piece()

        gate_lr = proj(C_GL, D_IN_PAD).astype(BF16)
        z = _dot(gate_lr, wa2_ref[...]) + ba_ref[...]
        la_s[...] = -_softplus(-z) * (1.0 / GLA_GATE_NORM)
        next_piece()
        bcur = jnp.zeros((BATCH, QK), F32)
        for t in range(nt):
            sl = slice(t * BATCH, (t + 1) * BATCH)
            bcur = bcur + la_s[sl, :]
            bc_s[sl, :] = bcur
        next_piece()
        bc = bc_s[...]
        e_last = jnp.exp(bcur)
        q_dec = proj(C_Q, C_K) * (GLA_DK ** -0.5) * jnp.exp(bc)
        kk = proj(C_K, C_V)
        k_dec = kk * jnp.exp(-bc)
        k_last = kk * jnp.exp(_slab_bcast(bcur, nt) - bc)
        next_piece()
        vv = proj(C_V, C_G)
        gg = proj(C_G, C_GL)
        next_piece()
        hr = ROWS // 2
        causal_top = cm_s[0:hr, 0:hr] > 0.5
        causal_bot = cm_s[hr:ROWS, :] > 0.5

        lane = lax.broadcasted_iota(jnp.int32, (SUBLANES, LANES), 1)
        sub = lax.broadcasted_iota(jnp.int32, (SUBLANES, LANES), 0)
        half = lane >> 6
        seq_sel = [(sub == 2 * jj + half).astype(F32) for jj in range(BATCH // 2)]

        def expand(m):
            mb = m.astype(BF16)
            return jnp.concatenate([mb * sel_s[jj] for jj in range(BATCH // 2)], axis=1)

        def both_halves(m, hh):
            keep = (lax.broadcasted_iota(jnp.int32, m.shape, 1) >> 6) == hh
            mh = jnp.where(keep, m, 0.0)
            return mh, mh + pltpu.roll(mh, GLA_DK, axis=1)

        gla_parts = []
        for hd in range(GLA_HEADS):
            p, hh = hd // 2, hd % 2
            pc = slice(p * LANES, (p + 1) * LANES)
            qh, q_both = both_halves(q_dec[:, pc], hh)
            _, k_both = both_halves(k_last[:, pc], hh)
            _, e_both = both_halves(e_last[:, pc], hh)
            v_h = vv[:, hd * GLA_DV:(hd + 1) * GLA_DV]
            v_hb = v_h.astype(BF16)
            qb = qh.astype(BF16)
            kb = k_dec[:, pc].astype(BF16)
            s_top = jnp.where(causal_top, _dot_nt(qb[0:hr], kb[0:hr]), 0.0)
            s_bot = jnp.where(causal_bot, _dot_nt(qb[hr:ROWS], kb), 0.0)
            o_h = jnp.concatenate([_dot(s_top.astype(BF16), v_hb[0:hr]),
                                   _dot(s_bot.astype(BF16), v_hb)], axis=0)
            st = st_s[hd]
            o_h = o_h + _dot_nt(expand(q_both), st.astype(BF16))
            kv_t = _dot(v_h.T.astype(BF16), expand(k_both))
            decay = jnp.concatenate(
                [jnp.sum(e_both * sq, axis=0, keepdims=True) for sq in seq_sel], axis=1)
            st_s[hd] = st * decay + kv_t
            o_n = _rms(o_h) * gn_ref[...]
            g_h = gg[:, hd * GLA_DV:(hd + 1) * GLA_DV]
            gla_parts.append(o_n * (g_h * _sigmoid(g_h)))
            if hd < 2:
                next_piece()

        mix_in = jnp.concatenate([lru_out] + gla_parts, axis=1).astype(BF16)
        mix = _dot(mix_in, wout_ref[...])
        o_ref[out_rows, :] = x_s[...] + _slab_bcast(gt_ref[...], nt) * mix
        assert not fill

    @pl.when(j == 0)
    def _init():
        _start_all(copies(0, 0))
        _start_all(copies(1, 1))
        xbuf[0:HALO, :] = jnp.zeros((HALO, D_LRU), F32)
        hc_s[...] = jnp.zeros_like(hc_s)
        st_s[...] = jnp.zeros_like(st_s)
        r = lax.broadcasted_iota(jnp.int32, (ROWS, ROWS), 0)
        c = lax.broadcasted_iota(jnp.int32, (ROWS, ROWS), 1)
        same_seq = (r & (BATCH - 1)) == (c & (BATCH - 1))
        cm_s[...] = jnp.where(same_seq & ((c >> 3) <= (r >> 3)), 1.0, 0.0).astype(F32)
        rr = lax.broadcasted_iota(jnp.int32, (ROWS, LANES), 0) & (BATCH - 1)
        ll = lax.broadcasted_iota(jnp.int32, (ROWS, LANES), 1) >> 6
        for jj in range(BATCH // 2):
            sel_s[jj] = jnp.where(rr == 2 * jj + ll, 1.0, 0.0).astype(BF16)
        base_s[...] = jnp.zeros_like(base_s)
        tri_s[...] = jnp.where(r < c, 1.0, 0.0).astype(BF16)
        _wait_all(copies(0, 0))
        for piece in input_half(0):
            piece()

    @pl.when(2 * j + 2 < n_tiles)
    def _fetch_next_even():
        _start_all(copies(2 * j + 2, 0))

    _wait_all(copies(2 * j + 1, 1))
    recurrent_half(0, slice(0, ROWS), input_half(1))

    @pl.when(2 * j + 2 < n_tiles)
    def _next_even_arrived():
        _wait_all(copies(2 * j + 2, 0))

    @pl.when(2 * j + 3 < n_tiles)
    def _next_odd():
        _start_all(copies(2 * j + 3, 1))

    recurrent_half(1, slice(ROWS, 2 * ROWS), input_half(0) + route_pieces(0))
    for piece in route_pieces(1):
        piece()


def _const_spec(shape):
    nd = len(shape)
    return pl.BlockSpec(shape, lambda i: (0,) * nd)


def _mixer(x, sc1p, sh1, gt1, g_mix, w_in_p, conv_w, conv_b, wr_d, br, wi_d, bi, lam,
           wa2_p, ba, gn, w_out_b, sc2p, sh2, g_ffn, w_rt, b_rt):
    n = x.shape[0] * x.shape[1]
    consts = [sc1p, sh1, gt1, g_mix, w_in_p, conv_w, conv_b, wr_d, br, wi_d, bi, lam,
              wa2_p, ba, gn, w_out_b, sc2p, sh2, g_ffn, w_rt, b_rt]
    seq_buf = pltpu.VMEM((SLAB, BATCH * XPITCH, LANES), F32)
    return pl.pallas_call(
        _mixer_kernel,
        grid=(n // (2 * ROWS),),
        in_specs=[pl.BlockSpec(memory_space=pl.ANY)]
        + [_const_spec(a.shape) for a in consts],
        out_specs=[pl.BlockSpec((2 * ROWS, D_MODEL), lambda i: (i, 0)),
                   pl.BlockSpec((2 * ROWS * PACK, LANES), lambda i: (i, 0)),
                   pl.BlockSpec((2 * ROWS, LANES), lambda i: (i, 0)),
                   pl.BlockSpec((SUBLANES, 2 * ROWS), lambda i: (0, i)),
                   pl.BlockSpec((N_EXPERTS, LANES), lambda i: (0, 0))],
        out_shape=[jax.ShapeDtypeStruct((n, D_MODEL), F32),
                   jax.ShapeDtypeStruct((n * PACK, LANES), jnp.uint32),
                   jax.ShapeDtypeStruct((n, LANES), F32),
                   jax.ShapeDtypeStruct((SUBLANES, n), F32),
                   jax.ShapeDtypeStruct((N_EXPERTS, LANES), F32)],
        scratch_shapes=[
            pltpu.VMEM((HALO + ROWS, D_LRU), F32),
            pltpu.VMEM((ROWS, D_LRU), F32),
            pltpu.VMEM((ROWS, D_LRU), F32),
            pltpu.VMEM((ROWS, D_LRU), F32),
            pltpu.VMEM((BATCH, D_LRU), F32),
            pltpu.VMEM((ROWS, QK), F32),
            pltpu.VMEM((ROWS, QK), F32),
            pltpu.VMEM((GLA_HEADS, GLA_DV, BATCH * GLA_DK), F32),
            pltpu.VMEM((ROWS, ROWS), F32),
            pltpu.VMEM((BATCH // 2, ROWS, LANES), BF16),
            pltpu.VMEM((ROWS, D_MODEL), BF16),
            seq_buf, seq_buf,
            pltpu.VMEM((ROWS, D_MODEL), F32),
            pltpu.VMEM((ROWS, D_MODEL), F32),
            pltpu.VMEM((ROWS, D_IN_PAD), F32),
            pltpu.VMEM((ROWS, D_IN_PAD), F32),
            pltpu.VMEM((N_EXPERTS, LANES), F32),
            pltpu.VMEM((ROWS, ROWS), BF16),
            pltpu.SemaphoreType.DMA((2,)),
        ],
        compiler_params=pltpu.CompilerParams(dimension_semantics=("arbitrary",),
                                             vmem_limit_bytes=VMEM_LIMIT_BYTES),
        name="mixer",
    )(x, *consts)


def _slots_kernel(offs_ref, mt_ref, dest_ref):
    mt = mt_ref[...]
    e = mt[0:2, :].astype(jnp.int32)
    acc = mt[2:4, :].astype(jnp.int32)
    for j in range(N_EXPERTS):
        acc = acc + jnp.where(e == j, offs_ref[j], 0)
    dest_ref[...] = acc


def _slots(offs, meta_t):
    n = meta_t.shape[1]
    tl = min(n, 8192)
    grid_spec = pltpu.PrefetchScalarGridSpec(
        num_scalar_prefetch=1,
        grid=(n // tl,),
        in_specs=[pl.BlockSpec((SUBLANES, tl), lambda i, offs: (0, i))],
        out_specs=pl.BlockSpec((2, tl), lambda i, offs: (0, i)),
    )
    return pl.pallas_call(
        _slots_kernel,
        grid_spec=grid_spec,
        out_shape=jax.ShapeDtypeStruct((2, n), jnp.int32),
        name="slots",
    )(offs, meta_t)


def _row_copy(src, dst, sem):
    return pltpu.make_async_copy(src, dst, sem)


def _dispatch_kernel(nv_ref, dest_ref, h_ref, xs_ref, zbuf, hbuf, zsem, lsem, sem):
    n_blocks = xs_ref.shape[0] // MOE_BLOCK

    @pl.when(pl.program_id(0) == 0)
    def _zero_partial_blocks():
        zbuf[...] = jnp.zeros_like(zbuf)

        def fill(b):
            return _row_copy(zbuf, xs_ref.at[pl.ds(pl.multiple_of(b * MOE_BLOCK, MOE_BLOCK),
                                                   MOE_BLOCK)], zsem)

        def start(b, carry):
            @pl.when(nv_ref[b] < MOE_BLOCK)
            def _():
                fill(b).start()
            return carry

        def wait(b, carry):
            @pl.when(nv_ref[b] < MOE_BLOCK)
            def _():
                fill(b).wait()
            return carry

        lax.fori_loop(0, n_blocks, start, 0)
        lax.fori_loop(0, n_blocks, wait, 0)

    i = pl.program_id(0)
    n_steps = pl.num_programs(0)
    cur = lax.rem(i, 2)

    def load(step, slot):
        rows = pl.ds(pl.multiple_of(step * TOK_TILE, TOK_TILE), TOK_TILE)
        return _row_copy(h_ref.at[rows], hbuf.at[slot], lsem.at[slot])

    def wait_rows(slot):
        for k in range(2):
            _row_copy(hbuf.at[slot], xs_ref.at[pl.ds(0, TOK_TILE)], sem.at[slot]).wait()

    @pl.when(i == 0)
    def _first_load():
        load(0, 0).start()

    @pl.when(i > 0)
    def _drain_previous():
        wait_rows(1 - cur)

    @pl.when(i + 1 < n_steps)
    def _prefetch():
        load(i + 1, 1 - cur).start()

    load(i, cur).wait()

    for r in range(TOK_TILE):
        for k in range(2):
            _row_copy(hbuf.at[cur, r], xs_ref.at[dest_ref[k, r]], sem.at[cur]).start(priority=k)

    @pl.when(i == n_steps - 1)
    def _drain_last():
        wait_rows(cur)


def _dispatch(n_valid, dest, h2s, cap):
    n = h2s.shape[0]
    grid_spec = pltpu.PrefetchScalarGridSpec(
        num_scalar_prefetch=1,
        grid=(n // TOK_TILE,),
        in_specs=[pl.BlockSpec((2, TOK_TILE), lambda i, nv: (0, i), memory_space=pltpu.SMEM),
                  pl.BlockSpec(memory_space=pl.ANY)],
        out_specs=pl.BlockSpec(memory_space=pl.ANY),
        scratch_shapes=[pltpu.VMEM((MOE_BLOCK, PACK, LANES), jnp.uint32),
                        pltpu.VMEM((2, TOK_TILE, PACK, LANES), jnp.uint32),
                        pltpu.SemaphoreType.DMA(()),
                        pltpu.SemaphoreType.DMA((2,)),
                        pltpu.SemaphoreType.DMA((2,))],
    )
    return pl.pallas_call(
        _dispatch_kernel,
        grid_spec=grid_spec,
        out_shape=jax.ShapeDtypeStruct((cap, PACK, LANES), jnp.uint32),
        compiler_params=pltpu.CompilerParams(dimension_semantics=("arbitrary",)),
        name="dispatch",
    )(n_valid, dest, h2s)


def _expert_kernel(be_ref, nv_ref, xs_ref, wg_ref, wu_ref, wd_ref, ys_ref, wg_b, wu_b, wd_b):
    i = pl.program_id(0)
    nv = nv_ref[i]

    @pl.when((i == 0) | (be_ref[i] != be_ref[jnp.maximum(i - 1, 0)]))
    def _cast_weights():
        wg_b[...] = wg_ref[0].astype(BF16)
        wu_b[...] = wu_ref[0].astype(BF16)
        wd_b[...] = wd_ref[0].astype(BF16)

    @pl.when(nv > 0)
    def _compute():
        xb = _unpack_rows(xs_ref, MOE_BLOCK).astype(BF16)
        g = _dot(xb, wg_b[...])
        u = _dot(xb, wu_b[...])
        hid = (g * _sigmoid(g) * u).astype(BF16)
        for c in range(D_MODEL // DOWN_CHUNK):
            cols = slice(c * DOWN_CHUNK, (c + 1) * DOWN_CHUNK)
            out = _dot(hid, wd_b[:, cols])
            for j in range(DOWN_CHUNK // LANES):
                s = c * (DOWN_CHUNK // LANES) + j
                ys_ref[pl.ds(s, MOE_BLOCK, stride=SLAB), :] = out[:, j * LANES:(j + 1) * LANES]

    @pl.when(nv == 0)
    def _skip():
        ys_ref[...] = jnp.zeros_like(ys_ref)


def _experts(block_e, n_valid, xs, wg, wu, wd):
    n_blocks = xs.shape[0] // (MOE_BLOCK * PACK)
    grid_spec = pltpu.PrefetchScalarGridSpec(
        num_scalar_prefetch=2,
        grid=(n_blocks,),
        in_specs=[pl.BlockSpec((MOE_BLOCK * PACK, LANES), lambda i, be, nv: (i, 0)),
                  pl.BlockSpec((1, D_MODEL, D_EXPERT), lambda i, be, nv: (be[i], 0, 0)),
                  pl.BlockSpec((1, D_MODEL, D_EXPERT), lambda i, be, nv: (be[i], 0, 0)),
                  pl.BlockSpec((1, D_EXPERT, D_MODEL), lambda i, be, nv: (be[i], 0, 0))],
        out_specs=pl.BlockSpec((MOE_BLOCK * SLAB, LANES), lambda i, be, nv: (i, 0)),
        scratch_shapes=[pltpu.VMEM((D_MODEL, D_EXPERT), BF16),
                        pltpu.VMEM((D_MODEL, D_EXPERT), BF16),
                        pltpu.VMEM((D_EXPERT, D_MODEL), BF16)],
    )
    return pl.pallas_call(
        _expert_kernel,
        grid_spec=grid_spec,
        out_shape=jax.ShapeDtypeStruct((n_blocks * MOE_BLOCK * SLAB, LANES), F32),
        compiler_params=pltpu.CompilerParams(dimension_semantics=("arbitrary",),
                                             vmem_limit_bytes=VMEM_LIMIT_BYTES),
        name="experts",
    )(block_e, n_valid, xs, wg, wu, wd)


def _combine_kernel(dcur_ref, dnxt_ref, x_ref, meta_ref, gt_ref, gfin_ref, ys_ref, ys2d_ref,
                    o_ref, buf, sem, obuf, osem):
    i = pl.program_id(0)
    n_steps = pl.num_programs(0)
    cur = lax.rem(i, 2)
    nxt = 1 - cur

    def start_gather(dest_ref, slot, r):
        for k in range(2):
            dst = buf.at[slot, k, pl.ds(r * SLAB, SLAB)]
            _row_copy(ys_ref.at[dest_ref[k, r]], dst, sem.at[slot]).start(priority=k)

    def wait_gather(slot):
        for k in range(2):
            _row_copy(ys2d_ref.at[pl.ds(0, TOK_TILE * SLAB)], buf.at[slot, k], sem.at[slot]).wait()

    @pl.when(i == 0)
    def _first_gather():
        for r in range(TOK_TILE):
            start_gather(dcur_ref, 0, r)

    wait_gather(cur)

    @pl.when(i >= 2)
    def _drain_older():
        _wait_all(_seq_major_copies(o_ref, obuf.at[cur], osem.at[cur], i - 2, True))

    chunk = TOK_TILE // COMBINE_CHUNKS
    for c in range(COMBINE_CHUNKS):
        for r in range(c * chunk, (c + 1) * chunk):
            start_gather(dnxt_ref, nxt, r)
        rows = slice(c * chunk, (c + 1) * chunk)
        meta = meta_ref[rows, :]

        def expert_rows(k):
            return jnp.concatenate(
                [buf[cur, k, pl.ds(c * chunk * SLAB + s, chunk, stride=SLAB), :]
                 for s in range(SLAB)], axis=1)

        y = meta[:, 4:5] * expert_rows(0) + meta[:, 5:6] * expert_rows(1)
        x2 = x_ref[rows, :] + _slab_bcast(gt_ref[...], chunk // BATCH) * y
        out = _rms(x2) * gfin_ref[...]
        for tt in range(chunk // BATCH):
            t = c * (chunk // BATCH) + tt
            for l in range(SLAB):
                obuf[cur, l, pl.ds(t, BATCH, stride=XPITCH), :] = (
                    out[tt * BATCH:(tt + 1) * BATCH, l * LANES:(l + 1) * LANES])
    _start_all(_seq_major_copies(o_ref, obuf.at[cur], osem.at[cur], i, True))

    @pl.when(i == n_steps - 1)
    def _drain_last():
        wait_gather(nxt)

        @pl.when(i >= 1)
        def _():
            _wait_all(_seq_major_copies(o_ref, obuf.at[1 - cur], osem.at[1 - cur], i - 1, True))

        _wait_all(_seq_major_copies(o_ref, obuf.at[cur], osem.at[cur], i, True))


def _combine(dest, x1, meta, gt2, g_final, ys):
    n = x1.shape[0]
    assert TOK_TILE == ROWS
    last = n // TOK_TILE - 1
    return pl.pallas_call(
        _combine_kernel,
        grid=(n // TOK_TILE,),
        in_specs=[pl.BlockSpec((2, TOK_TILE), lambda i: (0, i), memory_space=pltpu.SMEM),
                  pl.BlockSpec((2, TOK_TILE), lambda i: (0, jnp.minimum(i + 1, last)),
                               memory_space=pltpu.SMEM),
                  pl.BlockSpec((TOK_TILE, D_MODEL), lambda i: (i, 0)),
                  pl.BlockSpec((TOK_TILE, LANES), lambda i: (i, 0)),
                  _const_spec(gt2.shape),
                  _const_spec(g_final.shape),
                  pl.BlockSpec(memory_space=pl.ANY),
                  pl.BlockSpec(memory_space=pl.ANY)],
        out_specs=pl.BlockSpec(memory_space=pl.ANY),
        out_shape=jax.ShapeDtypeStruct((BATCH, n // BATCH, D_MODEL), F32),
        scratch_shapes=[pltpu.VMEM((2, 2, TOK_TILE * SLAB, LANES), F32),
                        pltpu.SemaphoreType.DMA((2,)),
                        pltpu.VMEM((2, SLAB, BATCH * XPITCH, LANES), F32),
                        pltpu.SemaphoreType.DMA((2,))],
        compiler_params=pltpu.CompilerParams(dimension_semantics=("arbitrary",),
                                             vmem_limit_bytes=VMEM_LIMIT_BYTES),
        name="combine",
    )(dest, dest, x1, meta, gt2, g_final, ys.reshape(-1, SLAB, LANES), ys)


def _block_diag(w):
    h, d, _ = w.shape
    eye = jnp.eye(h, dtype=w.dtype)
    return (eye[:, None, :, None] * w[:, :, None, :]).reshape(h * d, h * d)


def kernel(x, c, w_ada, b_ada, g_mix, g_ffn, g_final, w_in, conv_w, conv_b, lru_wr, lru_br,
           lru_wi, lru_bi, lru_lambda, gla_wa2, gla_ba, gla_gnorm, w_out, w_coarse, b_coarse,
           w_fine, b_fine, w_gate, w_up, w_down):
    bsz, seq, d = x.shape
    assert bsz == BATCH and d == D_MODEL and seq % (2 * GLA_CHUNK) == 0
    assert w_ada.shape[0] == 1, "single-layer problem"
    n = bsz * seq
    row = lambda v: v.reshape(1, -1)

    mod = _ada(c, w_ada[0], b_ada[0])
    sh1, sc1, gt1, sh2, sc2, gt2 = jnp.split(mod, 6, axis=-1)

    w_in_p = jnp.pad(w_in[0], ((0, 0), (0, D_IN_PAD - w_in.shape[2]))).astype(BF16)
    wa2_p = jnp.pad(gla_wa2[0], ((0, LANES - GLA_GATE_RANK), (0, 0))).astype(BF16)
    pad_rows = ROUTE_ROWS - N_EXPERTS - N_GROUPS
    w_rt = jnp.pad(jnp.concatenate([w_fine[0], w_coarse[0]], axis=1).T,
                   ((0, pad_rows), (0, 0))).astype(BF16)
    b_rt = jnp.pad(jnp.concatenate([b_fine[0], b_coarse[0]]), (0, pad_rows)).reshape(-1, 1)
    x1, h2, meta, meta_t, cnt = _mixer(
        x, 1.0 + sc1, sh1, gt1, row(g_mix[0]), w_in_p, conv_w[0], row(conv_b[0]),
        _block_diag(lru_wr[0]).astype(BF16), row(lru_br[0]),
        _block_diag(lru_wi[0]).astype(BF16), row(lru_bi[0]), row(lru_lambda[0]),
        wa2_p, row(gla_ba[0]), row(gla_gnorm[0]), w_out[0].astype(BF16),
        1.0 + sc2, sh2, row(g_ffn[0]), w_rt, b_rt)

    counts = cnt[:, 0].astype(jnp.int32)
    padded = (counts + MOE_BLOCK - 1) // MOE_BLOCK * MOE_BLOCK
    pends = jnp.cumsum(padded)
    offs = pends - padded
    cap = (2 * n + MOE_BLOCK - 1) // MOE_BLOCK * MOE_BLOCK + N_EXPERTS * MOE_BLOCK
    n_blocks = cap // MOE_BLOCK
    starts = jnp.arange(n_blocks, dtype=jnp.int32) * MOE_BLOCK
    block_e = jnp.minimum(jnp.sum((pends[None, :] <= starts[:, None]).astype(jnp.int32), axis=1),
                          N_EXPERTS - 1)
    own = block_e[:, None] == jnp.arange(N_EXPERTS, dtype=jnp.int32)[None, :]
    seg_end = jnp.sum(jnp.where(own, (offs + counts)[None, :], 0), axis=1)
    n_valid = jnp.clip(seg_end - starts, 0, MOE_BLOCK).astype(jnp.int32)
    dest = _slots(offs.astype(jnp.int32), meta_t)

    xs = _dispatch(n_valid, dest, h2.reshape(n, PACK, LANES), cap)
    ys = _experts(block_e, n_valid, xs.reshape(cap * PACK, LANES),
                  w_gate[0], w_up[0], w_down[0])
    return _combine(dest, x1, meta, gt2, row(g_final), ys)
```

```python
import functools

import jax
import jax.numpy as jnp
import numpy as np
from jax import lax
from jax.experimental import pallas as pl
from jax.experimental.pallas import tpu as pltpu

F32 = jnp.float32
BF16 = jnp.bfloat16

SUBLANES = 8
LANES = 128
VMEM_LIMIT_BYTES = 56 * 1024 * 1024

D_MODEL = 1024
BATCH = SUBLANES
D_LRU = 512
LRU_HEADS = 8
CONV_WIDTH = 4
LRU_C = 8.0
D_GLA = 512
GLA_HEADS = 4
GLA_DV = 128
GLA_DK = 64
GLA_GATE_RANK = 16
GLA_GATE_NORM = 16.0
GLA_CHUNK = 64
N_GROUPS = 4
EXPERTS_PER_GROUP = 8
N_EXPERTS = 32
D_EXPERT = 512
MOE_BLOCK = 1024
ZERO_ROWS = 256
EPS = 1e-6

QK = GLA_HEADS * GLA_DK
ROWS = GLA_CHUNK * BATCH
HALO = (CONV_WIDTH - 1) * BATCH
SLAB = D_MODEL // LANES
PACK = D_MODEL // (2 * LANES)
C_LX, C_LY, C_Q, C_K, C_V, C_G, C_GL = 0, 512, 1024, 1280, 1536, 2048, 2560
D_IN_PAD = 2688
ROUTE_ROWS = N_EXPERTS + SUBLANES
TOK_TILE = 512
DOWN_CHUNK = 256
PROJ_CHUNK = 256
COMBINE_CHUNKS = 8
XPITCH = 72


def _dot(a, b):
    return jnp.dot(a, b, preferred_element_type=F32)


def _dot_nt(a, b):
    return lax.dot_general(a, b, (((1,), (1,)), ((), ())), preferred_element_type=F32)


def _softplus(z):
    return jnp.maximum(z, 0.0) + jnp.log1p(jnp.exp(-jnp.abs(z)))


def _sigmoid(z):
    return 0.5 * jnp.tanh(0.5 * z) + 0.5


def _rms(x):
    return x * lax.rsqrt(jnp.mean(x * x, axis=-1, keepdims=True) + EPS)


def _pack_rows(ref, val, rows):
    for s in range(PACK):
        lo = val[:, (2 * s) * LANES:(2 * s + 1) * LANES].astype(BF16).astype(F32)
        hi = val[:, (2 * s + 1) * LANES:(2 * s + 2) * LANES].astype(BF16).astype(F32)
        word = (lax.bitcast_convert_type(lo, jnp.uint32) >> 16) | lax.bitcast_convert_type(
            hi, jnp.uint32)
        ref[pl.ds(s, rows, stride=PACK), :] = word


def _unpack_rows(ref, rows):
    cols = []
    for s in range(PACK):
        word = ref[pl.ds(s, rows, stride=PACK), :]
        cols.append(lax.bitcast_convert_type(word << 16, F32))
        cols.append(lax.bitcast_convert_type(word & jnp.uint32(0xFFFF0000), F32))
    return jnp.concatenate(cols, axis=1)


def _slab_bcast(v, n):
    c = v.shape[-1]
    return jnp.broadcast_to(v[None], (n, SUBLANES, c)).reshape(n * SUBLANES, c)


def _seq_major_copies(hbm, buf, sem, step, to_hbm):
    copies = []
    t0 = pl.multiple_of(step * GLA_CHUNK, GLA_CHUNK)
    for b in range(BATCH):
        for l in range(SLAB):
            h = hbm.at[b, pl.ds(t0, GLA_CHUNK), pl.ds(l * LANES, LANES)]
            v = buf.at[l, pl.ds(b * XPITCH, GLA_CHUNK)]
            copies.append(pltpu.make_async_copy(v, h, sem) if to_hbm
                          else pltpu.make_async_copy(h, v, sem))
    return copies


def _start_all(copies):
    for c in copies:
        c.start()


def _wait_all(copies):
    for c in copies:
        c.wait()


def _ada_kernel(c_ref, w_ref, b_ref, o_ref):
    c = c_ref[...]
    s = c * _sigmoid(c)
    o_ref[...] = jnp.dot(s, w_ref[...], preferred_element_type=F32,
                         precision=lax.Precision.HIGHEST) + b_ref[...]


def _ada(c, w, b):
    n_out = w.shape[1]
    tn = 1024
    return pl.pallas_call(
        _ada_kernel,
        grid=(n_out // tn,),
        in_specs=[pl.BlockSpec((BATCH, D_MODEL), lambda j: (0, 0)),
                  pl.BlockSpec((D_MODEL, tn), lambda j: (0, j)),
                  pl.BlockSpec((1, tn), lambda j: (0, j))],
        out_specs=pl.BlockSpec((BATCH, tn), lambda j: (0, j)),
        out_shape=jax.ShapeDtypeStruct((BATCH, n_out), F32),
        name="ada",
    )(c, w, b.reshape(1, n_out))


def _mixer_kernel(x_ref, sc_ref, sh_ref, gt_ref, gmix_ref, win_ref, cw_ref, cb_ref,
                  wr_ref, br_ref, wi_ref, bi_ref, lam_ref, wa2_ref, ba_ref, gn_ref, wout_ref,
                  sc2_ref, sh2_ref, gffn_ref, wrt_ref, brt_ref,
                  o_ref, h2_ref, meta_ref, metat_ref, cnt_ref,
                  xbuf, a_s, u_s, hs_s, hc_s, la_s, bc_s, st_s, cm_s, sel_s, hb_s,
                  xin_a, xin_b, x_a, x_b, p_a, p_b, base_s, tri_s, xsem):
    j = pl.program_id(0)
    n_tiles = 2 * pl.num_programs(0)
    nt = GLA_CHUNK
    side = ((xin_a, x_a, p_a), (xin_b, x_b, p_b))

    def copies(tile, sd):
        return _seq_major_copies(x_ref, side[sd][0], xsem.at[sd], tile, False)

    def normalise(sd):
        xin, x_s, _ = side[sd]
        for t in range(nt):
            for l in range(SLAB):
                x_s[t * BATCH:(t + 1) * BATCH, l * LANES:(l + 1) * LANES] = (
                    xin[l, pl.ds(t, BATCH, stride=XPITCH), :])
        y = _rms(x_s[...]) * gmix_ref[...]
        hb_s[...] = (y * _slab_bcast(sc_ref[...], nt) + _slab_bcast(sh_ref[...], nt)).astype(BF16)

    def project(sd, c0, c1):
        side[sd][2][:, c0:c1] = _dot(hb_s[...], win_ref[:, c0:c1])

    def input_half(sd):
        pieces = [lambda: normalise(sd)]
        for c0 in range(0, D_IN_PAD, PROJ_CHUNK):
            pieces.append(functools.partial(project, sd, c0, min(c0 + PROJ_CHUNK, D_IN_PAD)))
        return pieces

    def route_pieces(half):
        rows = slice(half * ROWS, (half + 1) * ROWS)
        keep = {}

        def logits_piece():
            h2 = (_rms(o_ref[rows, :]) * gffn_ref[...] * _slab_bcast(sc2_ref[...], nt)
                  + _slab_bcast(sh2_ref[...], nt))
            _pack_rows(h2_ref.at[pl.ds(half * ROWS * PACK, ROWS * PACK)], h2, ROWS)
            keep["logits"] = _dot_nt(wrt_ref[...], h2.astype(BF16)) + brt_ref[...]

        def select_piece():
            logits = keep["logits"]
            fl_all = logits[0:N_EXPERTS, :]
            cl_all = logits[N_EXPERTS:ROUTE_ROWS, :]
            neg = jnp.float32(-jnp.inf)

            def first_max(vals, idx_rows):
                m = jnp.max(vals, axis=0, keepdims=True)
                idx = jnp.min(jnp.where(vals == m, idx_rows, vals.shape[0]), axis=0,
                              keepdims=True)
                return m, idx

            crow = lax.broadcasted_iota(jnp.int32, cl_all.shape, 0)
            cl = jnp.where(crow < N_GROUPS, cl_all, neg)
            cmax, grp = first_max(cl, crow)
            p_grp = 1.0 / jnp.sum(jnp.exp(cl - cmax), axis=0, keepdims=True)
            frow = lax.broadcasted_iota(jnp.int32, fl_all.shape, 0)
            fl = jnp.where((frow >> 3) == grp, fl_all, neg)
            f1, i1 = first_max(fl, frow)
            f2, i2 = first_max(jnp.where(frow == i1, neg, fl), frow)
            z = jnp.sum(jnp.exp(fl - f1), axis=0, keepdims=True)
            tp1 = 1.0 / z
            tp2 = jnp.exp(f2 - f1) / z
            w1 = p_grp * tp1 / (tp1 + tp2)
            w2 = p_grp * tp2 / (tp1 + tp2)

            hit1 = frow == i1
            hit2 = frow == i2
            assign = jnp.where(hit1 | hit2, 1.0, 0.0)
            before = _dot(assign.astype(BF16), tri_s[...]) + base_s[:, 0:1]
            rank1 = jnp.sum(jnp.where(hit1, before, 0.0), axis=0, keepdims=True)
            rank2 = jnp.sum(jnp.where(hit2, before, 0.0), axis=0, keepdims=True)
            base_s[...] = base_s[...] + jnp.sum(assign, axis=1, keepdims=True)
            cnt_ref[...] = base_s[...]

            parts = [i1.astype(F32), i2.astype(F32), rank1, rank2, w1, w2]
            meta_t = jnp.concatenate(parts + [jnp.zeros((LANES - len(parts), ROWS), F32)], axis=0)
            metat_ref[:, rows] = meta_t[0:SUBLANES, :]
            meta_ref[rows, :] = meta_t.T

        return [logits_piece, select_piece]

    def recurrent_half(sd, out_rows, fill):
        _, x_s, p_s = side[sd]
        fill = list(fill)

        def next_piece():
            if fill:
                fill.pop(0)()

        def proj(c0, c1):
            return p_s[:, c0:c1]

        next_piece()
        xbuf[HALO:HALO + ROWS, :] = proj(C_LX, C_LY)
        cw = cw_ref[...]
        cx = cb_ref[...] + sum(cw[k:k + 1, :] * xbuf[k * BATCH:k * BATCH + ROWS, :]
                               for k in range(CONV_WIDTH))
        xbuf[0:HALO, :] = xbuf[ROWS:ROWS + HALO, :]
        cxb = cx.astype(BF16)
        next_piece()

        def gate(w_ref, b_ref):
            hw = D_LRU // 2
            pre = jnp.concatenate([_dot(cxb[:, 0:hw], w_ref[0:hw, 0:hw]),
                                   _dot(cxb[:, hw:], w_ref[hw:, hw:])], axis=1)
            return _sigmoid(pre + b_ref[...])

        r_gate = gate(wr_ref, br_ref)
        next_piece()
        i_gate = gate(wi_ref, bi_ref)
        next_piece()
        decay_rate = (-LRU_C) * _softplus(-lam_ref[...])
        for rows in (slice(0, ROWS // 2), slice(ROWS // 2, ROWS)):
            log_a = r_gate[rows] * decay_rate
            a_s[rows, :] = jnp.exp(log_a)
            th = jnp.tanh(log_a)
            u_s[rows, :] = jnp.sqrt(-2.0 * th / (1.0 - th)) * (i_gate[rows] * cx[rows])
            next_piece()
        hcur = hc_s[...]
        for t in range(nt):
            sl = slice(t * BATCH, (t + 1) * BATCH)
            hcur = a_s[sl, :] * hcur + u_s[sl, :]
            hs_s[sl, :] = hcur
        hc_s[...] = hcur
        next_piece()
        lru_out = hs_s[...] * jax.nn.gelu(proj(C_LY, C_Q), approximate=True)
        next_piece()

        gate_lr = proj(C_GL, D_IN_PAD).astype(BF16)
        z = _dot(gate_lr, wa2_ref[...]) + ba_ref[...]
        la_s[...] = -_softplus(-z) * (1.0 / GLA_GATE_NORM)
        next_piece()
        bcur = jnp.zeros((BATCH, QK), F32)
        for t in range(nt):
            sl = slice(t * BATCH, (t + 1) * BATCH)
            bcur = bcur + la_s[sl, :]
            bc_s[sl, :] = bcur
        next_piece()
        bc = bc_s[...]
        e_last = jnp.exp(bcur)
        q_dec = proj(C_Q, C_K) * (GLA_DK ** -0.5) * jnp.exp(bc)
        kk = proj(C_K, C_V)
        k_dec = kk * jnp.exp(-bc)
        k_last = kk * jnp.exp(_slab_bcast(bcur, nt) - bc)
        next_piece()
        vv = proj(C_V, C_G)
        gg = proj(C_G, C_GL)
        next_piece()
        hr = ROWS // 2
        causal_top = cm_s[0:hr, 0:hr] > 0.5
        causal_bot = cm_s[hr:ROWS, :] > 0.5

        lane = lax.broadcasted_iota(jnp.int32, (SUBLANES, LANES), 1)
        sub = lax.broadcasted_iota(jnp.int32, (SUBLANES, LANES), 0)
        half = lane >> 6
        seq_sel = [(sub == 2 * jj + half).astype(F32) for jj in range(BATCH // 2)]

        def expand(m):
            mb = m.astype(BF16)
            return jnp.concatenate([mb * sel_s[jj] for jj in range(BATCH // 2)], axis=1)

        def both_halves(m, hh):
            keep = (lax.broadcasted_iota(jnp.int32, m.shape, 1) >> 6) == hh
            mh = jnp.where(keep, m, 0.0)
            return mh, mh + pltpu.roll(mh, GLA_DK, axis=1)

        gla_parts = []
        for hd in range(GLA_HEADS):
            p, hh = hd // 2, hd % 2
            pc = slice(p * LANES, (p + 1) * LANES)
            qh, q_both = both_halves(q_dec[:, pc], hh)
            _, k_both = both_halves(k_last[:, pc], hh)
            _, e_both = both_halves(e_last[:, pc], hh)
            v_h = vv[:, hd * GLA_DV:(hd + 1) * GLA_DV]
            v_hb = v_h.astype(BF16)
            qb = qh.astype(BF16)
            kb = k_dec[:, pc].astype(BF16)
            s_top = jnp.where(causal_top, _dot_nt(qb[0:hr], kb[0:hr]), 0.0)
            s_bot = jnp.where(causal_bot, _dot_nt(qb[hr:ROWS], kb), 0.0)
            o_h = jnp.concatenate([_dot(s_top.astype(BF16), v_hb[0:hr]),
                                   _dot(s_bot.astype(BF16), v_hb)], axis=0)
            st = st_s[hd]
            o_h = o_h + _dot_nt(expand(q_both), st.astype(BF16))
            kv_t = _dot(v_h.T.astype(BF16), expand(k_both))
            decay = jnp.concatenate(
                [jnp.sum(e_both * sq, axis=0, keepdims=True) for sq in seq_sel], axis=1)
            st_s[hd] = st * decay + kv_t
            o_n = _rms(o_h) * gn_ref[...]
            g_h = gg[:, hd * GLA_DV:(hd + 1) * GLA_DV]
            gla_parts.append(o_n * (g_h * _sigmoid(g_h)))
            if hd < 2:
                next_piece()

        mix_in = jnp.concatenate([lru_out] + gla_parts, axis=1).astype(BF16)
        mix = _dot(mix_in, wout_ref[...])
        o_ref[out_rows, :] = x_s[...] + _slab_bcast(gt_ref[...], nt) * mix
        assert not fill

    @pl.when(j == 0)
    def _init():
        _start_all(copies(0, 0))
        _start_all(copies(1, 1))
        xbuf[0:HALO, :] = jnp.zeros((HALO, D_LRU), F32)
        hc_s[...] = jnp.zeros_like(hc_s)
        st_s[...] = jnp.zeros_like(st_s)
        r = lax.broadcasted_iota(jnp.int32, (ROWS, ROWS), 0)
        c = lax.broadcasted_iota(jnp.int32, (ROWS, ROWS), 1)
        same_seq = (r & (BATCH - 1)) == (c & (BATCH - 1))
        cm_s[...] = jnp.where(same_seq & ((c >> 3) <= (r >> 3)), 1.0, 0.0).astype(F32)
        rr = lax.broadcasted_iota(jnp.int32, (ROWS, LANES), 0) & (BATCH - 1)
        ll = lax.broadcasted_iota(jnp.int32, (ROWS, LANES), 1) >> 6
        for jj in range(BATCH // 2):
            sel_s[jj] = jnp.where(rr == 2 * jj + ll, 1.0, 0.0).astype(BF16)
        base_s[...] = jnp.zeros_like(base_s)
        tri_s[...] = jnp.where(r < c, 1.0, 0.0).astype(BF16)
        _wait_all(copies(0, 0))
        for piece in input_half(0):
            piece()

    @pl.when(2 * j + 2 < n_tiles)
    def _fetch_next_even():
        _start_all(copies(2 * j + 2, 0))

    _wait_all(copies(2 * j + 1, 1))
    recurrent_half(0, slice(0, ROWS), input_half(1))

    @pl.when(2 * j + 2 < n_tiles)
    def _next_even_arrived():
        _wait_all(copies(2 * j + 2, 0))

    @pl.when(2 * j + 3 < n_tiles)
    def _next_odd():
        _start_all(copies(2 * j + 3, 1))

    recurrent_half(1, slice(ROWS, 2 * ROWS), input_half(0) + route_pieces(0))
    for piece in route_pieces(1):
        piece()


def _const_spec(shape):
    nd = len(shape)
    return pl.BlockSpec(shape, lambda i: (0,) * nd)


def _mixer(x, sc1p, sh1, gt1, g_mix, w_in_p, conv_w, conv_b, wr_d, br, wi_d, bi, lam,
           wa2_p, ba, gn, w_out_b, sc2p, sh2, g_ffn, w_rt, b_rt):
    n = x.shape[0] * x.shape[1]
    consts = [sc1p, sh1, gt1, g_mix, w_in_p, conv_w, conv_b, wr_d, br, wi_d, bi, lam,
              wa2_p, ba, gn, w_out_b, sc2p, sh2, g_ffn, w_rt, b_rt]
    seq_buf = pltpu.VMEM((SLAB, BATCH * XPITCH, LANES), F32)
    return pl.pallas_call(
        _mixer_kernel,
        grid=(n // (2 * ROWS),),
        in_specs=[pl.BlockSpec(memory_space=pl.ANY)]
        + [_const_spec(a.shape) for a in consts],
        out_specs=[pl.BlockSpec((2 * ROWS, D_MODEL), lambda i: (i, 0)),
                   pl.BlockSpec((2 * ROWS * PACK, LANES), lambda i: (i, 0)),
                   pl.BlockSpec((2 * ROWS, LANES), lambda i: (i, 0)),
                   pl.BlockSpec((SUBLANES, 2 * ROWS), lambda i: (0, i)),
                   pl.BlockSpec((N_EXPERTS, LANES), lambda i: (0, 0))],
        out_shape=[jax.ShapeDtypeStruct((n, D_MODEL), F32),
                   jax.ShapeDtypeStruct((n * PACK, LANES), jnp.uint32),
                   jax.ShapeDtypeStruct((n, LANES), F32),
                   jax.ShapeDtypeStruct((SUBLANES, n), F32),
                   jax.ShapeDtypeStruct((N_EXPERTS, LANES), F32)],
        scratch_shapes=[
            pltpu.VMEM((HALO + ROWS, D_LRU), F32),
            pltpu.VMEM((ROWS, D_LRU), F32),
            pltpu.VMEM((ROWS, D_LRU), F32),
            pltpu.VMEM((ROWS, D_LRU), F32),
            pltpu.VMEM((BATCH, D_LRU), F32),
            pltpu.VMEM((ROWS, QK), F32),
            pltpu.VMEM((ROWS, QK), F32),
            pltpu.VMEM((GLA_HEADS, GLA_DV, BATCH * GLA_DK), F32),
            pltpu.VMEM((ROWS, ROWS), F32),
            pltpu.VMEM((BATCH // 2, ROWS, LANES), BF16),
            pltpu.VMEM((ROWS, D_MODEL), BF16),
            seq_buf, seq_buf,
            pltpu.VMEM((ROWS, D_MODEL), F32),
            pltpu.VMEM((ROWS, D_MODEL), F32),
            pltpu.VMEM((ROWS, D_IN_PAD), F32),
            pltpu.VMEM((ROWS, D_IN_PAD), F32),
            pltpu.VMEM((N_EXPERTS, LANES), F32),
            pltpu.VMEM((ROWS, ROWS), BF16),
            pltpu.SemaphoreType.DMA((2,)),
        ],
        compiler_params=pltpu.CompilerParams(dimension_semantics=("arbitrary",),
                                             vmem_limit_bytes=VMEM_LIMIT_BYTES),
        name="mixer",
    )(x, *consts)


def _slots_kernel(offs_ref, mt_ref, dest_ref):
    mt = mt_ref[...]
    e = mt[0:2, :].astype(jnp.int32)
    acc = mt[2:4, :].astype(jnp.int32)
    for j in range(N_EXPERTS):
        acc = acc + jnp.where(e == j, offs_ref[j], 0)
    dest_ref[...] = acc


def _slots(offs, meta_t):
    n = meta_t.shape[1]
    tl = min(n, 8192)
    grid_spec = pltpu.PrefetchScalarGridSpec(
        num_scalar_prefetch=1,
        grid=(n // tl,),
        in_specs=[pl.BlockSpec((SUBLANES, tl), lambda i, offs: (0, i))],
        out_specs=pl.BlockSpec((2, tl), lambda i, offs: (0, i)),
    )
    return pl.pallas_call(
        _slots_kernel,
        grid_spec=grid_spec,
        out_shape=jax.ShapeDtypeStruct((2, n), jnp.int32),
        name="slots",
    )(offs, meta_t)


def _row_copy(src, dst, sem):
    return pltpu.make_async_copy(src, dst, sem)


def _dispatch_kernel(nv_ref, dest_ref, h_ref, xs_ref, zbuf, hbuf, zsem, lsem, sem):
    n_granules = xs_ref.shape[0] // ZERO_ROWS
    per_block = MOE_BLOCK // ZERO_ROWS

    @pl.when(pl.program_id(0) == 0)
    def _zero_unused_slots():
        zbuf[...] = jnp.zeros_like(zbuf)

        def fill(g):
            return _row_copy(zbuf, xs_ref.at[pl.ds(pl.multiple_of(g * ZERO_ROWS, ZERO_ROWS),
                                                   ZERO_ROWS)], zsem)

        def has_unused(g):
            shift = per_block.bit_length() - 1
            assert per_block == 1 << shift
            return nv_ref[g >> shift] < ((g & (per_block - 1)) + 1) * ZERO_ROWS

        def start(g, carry):
            @pl.when(has_unused(g))
            def _():
                fill(g).start()
            return carry

        def wait(g, carry):
            @pl.when(has_unused(g))
            def _():
                fill(g).wait()
            return carry

        lax.fori_loop(0, n_granules, start, 0)
        lax.fori_loop(0, n_granules, wait, 0)

    i = pl.program_id(0)
    n_steps = pl.num_programs(0)
    cur = lax.rem(i, 2)

    def load(step, slot):
        rows = pl.ds(pl.multiple_of(step * TOK_TILE, TOK_TILE), TOK_TILE)
        return _row_copy(h_ref.at[rows], hbuf.at[slot], lsem.at[slot])

    def wait_rows(slot):
        for k in range(2):
            _row_copy(hbuf.at[slot], xs_ref.at[pl.ds(0, TOK_TILE)], sem.at[slot]).wait()

    @pl.when(i == 0)
    def _first_load():
        load(0, 0).start()

    @pl.when(i > 0)
    def _drain_previous():
        wait_rows(1 - cur)

    @pl.when(i + 1 < n_steps)
    def _prefetch():
        load(i + 1, 1 - cur).start()

    load(i, cur).wait()

    for r in range(TOK_TILE):
        for k in range(2):
            _row_copy(hbuf.at[cur, r], xs_ref.at[dest_ref[k, r]], sem.at[cur]).start(priority=k)

    @pl.when(i == n_steps - 1)
    def _drain_last():
        wait_rows(cur)


def _dispatch(n_valid, dest, h2s, cap):
    n = h2s.shape[0]
    grid_spec = pltpu.PrefetchScalarGridSpec(
        num_scalar_prefetch=1,
        grid=(n // TOK_TILE,),
        in_specs=[pl.BlockSpec((2, TOK_TILE), lambda i, nv: (0, i), memory_space=pltpu.SMEM),
                  pl.BlockSpec(memory_space=pl.ANY)],
        out_specs=pl.BlockSpec(memory_space=pl.ANY),
        scratch_shapes=[pltpu.VMEM((ZERO_ROWS, PACK, LANES), jnp.uint32),
                        pltpu.VMEM((2, TOK_TILE, PACK, LANES), jnp.uint32),
                        pltpu.SemaphoreType.DMA(()),
                        pltpu.SemaphoreType.DMA((2,)),
                        pltpu.SemaphoreType.DMA((2,))],
    )
    return pl.pallas_call(
        _dispatch_kernel,
        grid_spec=grid_spec,
        out_shape=jax.ShapeDtypeStruct((cap, PACK, LANES), jnp.uint32),
        compiler_params=pltpu.CompilerParams(dimension_semantics=("arbitrary",)),
        name="dispatch",
    )(n_valid, dest, h2s)


def _expert_kernel(be_ref, nv_ref, xs_ref, wg_ref, wu_ref, wd_ref, ys_ref, wg_b, wu_b, wd_b):
    i = pl.program_id(0)
    nv = nv_ref[i]

    @pl.when((i == 0) | (be_ref[i] != be_ref[jnp.maximum(i - 1, 0)]))
    def _cast_weights():
        wg_b[...] = wg_ref[0].astype(BF16)
        wu_b[...] = wu_ref[0].astype(BF16)
        wd_b[...] = wd_ref[0].astype(BF16)

    @pl.when(nv > 0)
    def _compute():
        xb = _unpack_rows(xs_ref, MOE_BLOCK).astype(BF16)
        g = _dot(xb, wg_b[...])
        u = _dot(xb, wu_b[...])
        hid = (g * _sigmoid(g) * u).astype(BF16)
        for c in range(D_MODEL // DOWN_CHUNK):
            cols = slice(c * DOWN_CHUNK, (c + 1) * DOWN_CHUNK)
            out = _dot(hid, wd_b[:, cols])
            for j in range(DOWN_CHUNK // LANES):
                s = c * (DOWN_CHUNK // LANES) + j
                ys_ref[pl.ds(s, MOE_BLOCK, stride=SLAB), :] = out[:, j * LANES:(j + 1) * LANES]

    @pl.when(nv == 0)
    def _skip():
        ys_ref[...] = jnp.zeros_like(ys_ref)


def _experts(block_e, n_valid, xs, wg, wu, wd):
    n_blocks = xs.shape[0] // (MOE_BLOCK * PACK)
    grid_spec = pltpu.PrefetchScalarGridSpec(
        num_scalar_prefetch=2,
        grid=(n_blocks,),
        in_specs=[pl.BlockSpec((MOE_BLOCK * PACK, LANES), lambda i, be, nv: (i, 0)),
                  pl.BlockSpec((1, D_MODEL, D_EXPERT), lambda i, be, nv: (be[i], 0, 0)),
                  pl.BlockSpec((1, D_MODEL, D_EXPERT), lambda i, be, nv: (be[i], 0, 0)),
                  pl.BlockSpec((1, D_EXPERT, D_MODEL), lambda i, be, nv: (be[i], 0, 0))],
        out_specs=pl.BlockSpec((MOE_BLOCK * SLAB, LANES), lambda i, be, nv: (i, 0)),
        scratch_shapes=[pltpu.VMEM((D_MODEL, D_EXPERT), BF16),
                        pltpu.VMEM((D_MODEL, D_EXPERT), BF16),
                        pltpu.VMEM((D_EXPERT, D_MODEL), BF16)],
    )
    return pl.pallas_call(
        _expert_kernel,
        grid_spec=grid_spec,
        out_shape=jax.ShapeDtypeStruct((n_blocks * MOE_BLOCK * SLAB, LANES), F32),
        compiler_params=pltpu.CompilerParams(dimension_semantics=("arbitrary",),
                                             vmem_limit_bytes=VMEM_LIMIT_BYTES),
        name="experts",
    )(block_e, n_valid, xs, wg, wu, wd)


def _combine_kernel(dcur_ref, dnxt_ref, x_ref, meta_ref, gt_ref, gfin_ref, ys_ref, ys2d_ref,
                    o_ref, buf, sem, obuf, osem):
    i = pl.program_id(0)
    n_steps = pl.num_programs(0)
    cur = lax.rem(i, 2)
    nxt = 1 - cur

    def start_gather(dest_ref, slot, r):
        for k in range(2):
            dst = buf.at[slot, k, pl.ds(r * SLAB, SLAB)]
            _row_copy(ys_ref.at[dest_ref[k, r]], dst, sem.at[slot]).start(priority=k)

    def wait_gather(slot):
        for k in range(2):
            _row_copy(ys2d_ref.at[pl.ds(0, TOK_TILE * SLAB)], buf.at[slot, k], sem.at[slot]).wait()

    @pl.when(i == 0)
    def _first_gather():
        for r in range(TOK_TILE):
            start_gather(dcur_ref, 0, r)

    wait_gather(cur)

    @pl.when(i >= 2)
    def _drain_older():
        _wait_all(_seq_major_copies(o_ref, obuf.at[cur], osem.at[cur], i - 2, True))

    chunk = TOK_TILE // COMBINE_CHUNKS
    for c in range(COMBINE_CHUNKS):
        for r in range(c * chunk, (c + 1) * chunk):
            start_gather(dnxt_ref, nxt, r)
        rows = slice(c * chunk, (c + 1) * chunk)
        meta = meta_ref[rows, :]

        def expert_rows(k):
            return jnp.concatenate(
                [buf[cur, k, pl.ds(c * chunk * SLAB + s, chunk, stride=SLAB), :]
                 for s in range(SLAB)], axis=1)

        y = meta[:, 4:5] * expert_rows(0) + meta[:, 5:6] * expert_rows(1)
        x2 = x_ref[rows, :] + _slab_bcast(gt_ref[...], chunk // BATCH) * y
        out = _rms(x2) * gfin_ref[...]
        for tt in range(chunk // BATCH):
            t = c * (chunk // BATCH) + tt
            for l in range(SLAB):
                obuf[cur, l, pl.ds(t, BATCH, stride=XPITCH), :] = (
                    out[tt * BATCH:(tt + 1) * BATCH, l * LANES:(l + 1) * LANES])
    _start_all(_seq_major_copies(o_ref, obuf.at[cur], osem.at[cur], i, True))

    @pl.when(i == n_steps - 1)
    def _drain_last():
        wait_gather(nxt)

        @pl.when(i >= 1)
        def _():
            _wait_all(_seq_major_copies(o_ref, obuf.at[1 - cur], osem.at[1 - cur], i - 1, True))

        _wait_all(_seq_major_copies(o_ref, obuf.at[cur], osem.at[cur], i, True))


def _combine(dest, x1, meta, gt2, g_final, ys):
    n = x1.shape[0]
    assert TOK_TILE == ROWS
    last = n // TOK_TILE - 1
    return pl.pallas_call(
        _combine_kernel,
        grid=(n // TOK_TILE,),
        in_specs=[pl.BlockSpec((2, TOK_TILE), lambda i: (0, i), memory_space=pltpu.SMEM),
                  pl.BlockSpec((2, TOK_TILE), lambda i: (0, jnp.minimum(i + 1, last)),
                               memory_space=pltpu.SMEM),
                  pl.BlockSpec((TOK_TILE, D_MODEL), lambda i: (i, 0)),
                  pl.BlockSpec((TOK_TILE, LANES), lambda i: (i, 0)),
                  _const_spec(gt2.shape),
                  _const_spec(g_final.shape),
                  pl.BlockSpec(memory_space=pl.ANY),
                  pl.BlockSpec(memory_space=pl.ANY)],
        out_specs=pl.BlockSpec(memory_space=pl.ANY),
        out_shape=jax.ShapeDtypeStruct((BATCH, n // BATCH, D_MODEL), F32),
        scratch_shapes=[pltpu.VMEM((2, 2, TOK_TILE * SLAB, LANES), F32),
                        pltpu.SemaphoreType.DMA((2,)),
                        pltpu.VMEM((2, SLAB, BATCH * XPITCH, LANES), F32),
                        pltpu.SemaphoreType.DMA((2,))],
        compiler_params=pltpu.CompilerParams(dimension_semantics=("arbitrary",),
                                             vmem_limit_bytes=VMEM_LIMIT_BYTES),
        name="combine",
    )(dest, dest, x1, meta, gt2, g_final, ys.reshape(-1, SLAB, LANES), ys)


def _block_diag(w):
    h, d, _ = w.shape
    eye = jnp.eye(h, dtype=w.dtype)
    return (eye[:, None, :, None] * w[:, :, None, :]).reshape(h * d, h * d)


def kernel(x, c, w_ada, b_ada, g_mix, g_ffn, g_final, w_in, conv_w, conv_b, lru_wr, lru_br,
           lru_wi, lru_bi, lru_lambda, gla_wa2, gla_ba, gla_gnorm, w_out, w_coarse, b_coarse,
           w_fine, b_fine, w_gate, w_up, w_down):
    bsz, seq, d = x.shape
    assert bsz == BATCH and d == D_MODEL and seq % (2 * GLA_CHUNK) == 0
    assert w_ada.shape[0] == 1, "single-layer problem"
    n = bsz * seq
    row = lambda v: v.reshape(1, -1)

    mod = _ada(c, w_ada[0], b_ada[0])
    sh1, sc1, gt1, sh2, sc2, gt2 = jnp.split(mod, 6, axis=-1)

    w_in_p = jnp.pad(w_in[0], ((0, 0), (0, D_IN_PAD - w_in.shape[2]))).astype(BF16)
    wa2_p = jnp.pad(gla_wa2[0], ((0, LANES - GLA_GATE_RANK), (0, 0))).astype(BF16)
    pad_rows = ROUTE_ROWS - N_EXPERTS - N_GROUPS
    w_rt = jnp.pad(jnp.concatenate([w_fine[0], w_coarse[0]], axis=1).T,
                   ((0, pad_rows), (0, 0))).astype(BF16)
    b_rt = jnp.pad(jnp.concatenate([b_fine[0], b_coarse[0]]), (0, pad_rows)).reshape(-1, 1)
    x1, h2, meta, meta_t, cnt = _mixer(
        x, 1.0 + sc1, sh1, gt1, row(g_mix[0]), w_in_p, conv_w[0], row(conv_b[0]),
        _block_diag(lru_wr[0]).astype(BF16), row(lru_br[0]),
        _block_diag(lru_wi[0]).astype(BF16), row(lru_bi[0]), row(lru_lambda[0]),
        wa2_p, row(gla_ba[0]), row(gla_gnorm[0]), w_out[0].astype(BF16),
        1.0 + sc2, sh2, row(g_ffn[0]), w_rt, b_rt)

    counts = cnt[:, 0].astype(jnp.int32)
    padded = (counts + MOE_BLOCK - 1) // MOE_BLOCK * MOE_BLOCK
    pends = jnp.cumsum(padded)
    offs = pends - padded
    cap = (2 * n + MOE_BLOCK - 1) // MOE_BLOCK * MOE_BLOCK + N_EXPERTS * MOE_BLOCK
    n_blocks = cap // MOE_BLOCK
    starts = jnp.arange(n_blocks, dtype=jnp.int32) * MOE_BLOCK
    block_e = jnp.minimum(jnp.sum((pends[None, :] <= starts[:, None]).astype(jnp.int32), axis=1),
                          N_EXPERTS - 1)
    own = block_e[:, None] == jnp.arange(N_EXPERTS, dtype=jnp.int32)[None, :]
    seg_end = jnp.sum(jnp.where(own, (offs + counts)[None, :], 0), axis=1)
    n_valid = jnp.clip(seg_end - starts, 0, MOE_BLOCK).astype(jnp.int32)
    dest = _slots(offs.astype(jnp.int32), meta_t)

    xs = _dispatch(n_valid, dest, h2.reshape(n, PACK, LANES), cap)
    ys = _experts(block_e, n_valid, xs.reshape(cap * PACK, LANES),
                  w_gate[0], w_up[0], w_down[0])
    return _combine(dest, x1, meta, gt2, row(g_final), ys)
```
